```python
import math
import jax, jax.numpy as jnp
from jax import lax
import numpy as np

D_MODEL = 1024
BATCH = 2
SEQ = 8192
DEPTH = 1

MIX_WIDTH = D_MODEL
DIFF_WIDTH = MIX_WIDTH // 2
RET_WIDTH = MIX_WIDTH - DIFF_WIDTH
DIFF_HEAD_DIM = 64
DIFF_V_DIM = 2 * DIFF_HEAD_DIM
DIFF_HEADS = DIFF_WIDTH // DIFF_V_DIM
RET_KEY_DIM = 64
RET_V_DIM = 2 * RET_KEY_DIM
RET_HEADS = RET_WIDTH // RET_V_DIM
D_FF = -(-(8 * D_MODEL) // (3 * 256)) * 256
ROPE_THETA = 10000.0
RET_THETA = 10000.0
Q_BLOCK = 128
RET_CHUNK = 128
NORM_EPS = 1e-6
HEAD_NORM_EPS = 1e-5
DQ_W = DIFF_HEADS * 2 * DIFF_HEAD_DIM
DK_W = DIFF_HEADS * 2 * DIFF_HEAD_DIM
DV_W = DIFF_HEADS * DIFF_V_DIM
RQ_W = RET_HEADS * RET_KEY_DIM
RK_W = RET_HEADS * RET_KEY_DIM
RV_W = RET_HEADS * RET_V_DIM
RG_W = RET_WIDTH
PROJ_WIDTH = DQ_W + DK_W + DV_W + RQ_W + RK_W + RV_W + RG_W

kernel_name = "hymba_diffattn_retention_hybrid"


def _rmsnorm(x, g, eps):
    xf = x.astype(jnp.float32)
    y = xf * lax.rsqrt(jnp.mean(xf * xf, axis=-1, keepdims=True) + eps)
    return (y * g.astype(jnp.float32)).astype(x.dtype)


def _rope_half(x, pos):
    d = x.shape[-1]
    inv = ROPE_THETA ** (-jnp.arange(0, d, 2, dtype=jnp.float32) / d)
    ang = pos[:, None] * inv[None, :]
    cos = jnp.cos(ang)[None, :, None, :]
    sin = jnp.sin(ang)[None, :, None, :]
    xf = x.astype(jnp.float32)
    x1, x2 = xf[..., : d // 2], xf[..., d // 2:]
    return jnp.concatenate([x1 * cos - x2 * sin, x1 * sin + x2 * cos], axis=-1).astype(x.dtype)


def _retnet_rotate(x, pos):
    d = x.shape[-1]
    inv = 1.0 / (RET_THETA ** jnp.linspace(0.0, 1.0, d // 2, dtype=jnp.float32))
    ang = pos[:, None] * inv[None, :]
    cos = jnp.cos(ang)[None, :, None, :]
    sin = jnp.sin(ang)[None, :, None, :]
    xf = x.astype(jnp.float32)
    xe, xo = xf[..., 0::2], xf[..., 1::2]
    out = jnp.stack([xe * cos - xo * sin, xo * cos + xe * sin], axis=-1)
    return out.reshape(x.shape).astype(x.dtype)


def _diff_attention(q, k, v, lam):
    B, H, _, S, d = q.shape
    e = v.shape[-1]
    nb = S // Q_BLOCK
    qb = q.reshape(B, H, 2, nb, Q_BLOCK, d).transpose(3, 0, 1, 2, 4, 5)
    kpos = jnp.arange(S)

    def block(args):
        qi, i = args
        s = jnp.einsum('bhcqd,bhckd->bhcqk', qi, k).astype(jnp.float32)
        qpos = i * Q_BLOCK + jnp.arange(Q_BLOCK)
        mask = kpos[None, :] <= qpos[:, None]
        p = jax.nn.softmax(jnp.where(mask, s, -jnp.inf), axis=-1)
        a = p[:, :, 0] - lam * p[:, :, 1]
        return jnp.einsum('bhqk,bhke->bhqe', a.astype(v.dtype), v)

    o = lax.map(block, (qb, jnp.arange(nb)))
    return o.transpose(1, 0, 3, 2, 4).reshape(B, S, H, e)


def _retention(q, k, v):
    B, S, H, dk = q.shape
    dv = v.shape[-1]
    C = RET_CHUNK
    N = S // C
    cdt = q.dtype

    def chunks(t):
        return t.transpose(0, 2, 1, 3).reshape(B, H, N, C, t.shape[-1])

    qc, kc, vc = chunks(q), chunks(k), chunks(v)
    log_g = jnp.log1p(-(2.0 ** (-5.0 - jnp.arange(H, dtype=jnp.float32))))
    n = jnp.arange(C, dtype=jnp.float32)
    rel = n[:, None] - n[None, :]
    dmat = jnp.where(rel >= 0, jnp.exp(log_g[:, None, None] * jnp.maximum(rel, 0.0)), 0.0)
    s = jnp.einsum('bhncd,bhnmd->bhncm', qc, kc) * dmat[None, :, None].astype(cdt)
    inner = jnp.einsum('bhncm,bhnme->bhnce', s, vc)
    k_dec = jnp.exp(log_g[:, None] * (C - 1.0 - n)[None, :]).astype(cdt)
    kv = jnp.einsum('bhnmd,bhnme->nbhde', kc * k_dec[None, :, None, :, None], vc).astype(jnp.float32)
    chunk_decay = jnp.exp(log_g * C)[None, :, None, None]

    def step(r, kv_i):
        return chunk_decay * r + kv_i, r

    _, r_prev = lax.scan(step, jnp.zeros((B, H, dk, dv), jnp.float32), kv)
    q_dec = jnp.exp(log_g[:, None] * (n + 1.0)[None, :]).astype(cdt)
    cross = jnp.einsum('bhncd,nbhde->bhnce', qc * q_dec[None, :, None, :, None], r_prev.astype(cdt))
    o = (inner + cross).reshape(B, H, S, dv)
    return o.transpose(0, 2, 1, 3)


def _head_groupnorm(o, g):
    B, S, H, dv = o.shape
    of = o.astype(jnp.float32)
    mu = jnp.mean(of, axis=-1, keepdims=True)
    c = of - mu
    y = c * lax.rsqrt(jnp.mean(c * c, axis=-1, keepdims=True) + HEAD_NORM_EPS)
    return (y.reshape(B, S, H * dv) * g.astype(jnp.float32)).astype(o.dtype)


def setup_inputs(seed: int = 0) -> dict:
    key = jax.random.key(seed)
    ks = jax.random.split(key, 16)
    f32 = jnp.float32
    x = jax.random.normal(ks[0], (BATCH, SEQ, D_MODEL), f32)
    attn_norm_g = 1.0 + 0.02 * jax.random.normal(ks[1], (DEPTH, D_MODEL), f32)
    w_in = jax.random.normal(ks[2], (DEPTH, D_MODEL, PROJ_WIDTH), f32) * D_MODEL ** -0.5
    lambda_q1 = 0.1 * jax.random.normal(ks[3], (DEPTH, DIFF_HEAD_DIM), f32)
    lambda_k1 = 0.1 * jax.random.normal(ks[4], (DEPTH, DIFF_HEAD_DIM), f32)
    lambda_q2 = 0.1 * jax.random.normal(ks[5], (DEPTH, DIFF_HEAD_DIM), f32)
    lambda_k2 = 0.1 * jax.random.normal(ks[6], (DEPTH, DIFF_HEAD_DIM), f32)
    diff_subln_g = 1.0 + 0.02 * jax.random.normal(ks[7], (DEPTH, DIFF_V_DIM), f32)
    ret_norm_g = 1.0 + 0.02 * jax.random.normal(ks[8], (DEPTH, RET_WIDTH), f32)
    w_out = jax.random.normal(ks[9], (DEPTH, MIX_WIDTH, D_MODEL), f32) * MIX_WIDTH ** -0.5
    ffn_norm_g = 1.0 + 0.02 * jax.random.normal(ks[10], (DEPTH, D_MODEL), f32)
    w_gate = jax.random.normal(ks[11], (DEPTH, D_MODEL, D_FF), f32) * D_MODEL ** -0.5
    w_up = jax.random.normal(ks[12], (DEPTH, D_MODEL, D_FF), f32) * D_MODEL ** -0.5
    w_down = jax.random.normal(ks[13], (DEPTH, D_FF, D_MODEL), f32) * D_FF ** -0.5
    final_norm_g = 1.0 + 0.02 * jax.random.normal(ks[14], (D_MODEL,), f32)
    return {"x": x, "attn_norm_g": attn_norm_g, "w_in": w_in, "lambda_q1": lambda_q1,
            "lambda_k1": lambda_k1, "lambda_q2": lambda_q2, "lambda_k2": lambda_k2,
            "diff_subln_g": diff_subln_g, "ret_norm_g": ret_norm_g, "w_out": w_out,
            "ffn_norm_g": ffn_norm_g, "w_gate": w_gate, "w_up": w_up, "w_down": w_down,
            "final_norm_g": final_norm_g}


def reference(x, attn_norm_g, w_in, lambda_q1, lambda_k1, lambda_q2, lambda_k2, diff_subln_g,
              ret_norm_g, w_out, ffn_norm_g, w_gate, w_up, w_down, final_norm_g):
    B, S, _ = x.shape
    pos = jnp.arange(S, dtype=jnp.float32)
    splits = np.cumsum([DQ_W, DK_W, DV_W, RQ_W, RK_W, RV_W])
    for l in range(DEPTH):
        lam_init = 0.8 - 0.6 * math.exp(-0.3 * l)
        h = _rmsnorm(x, attn_norm_g[l], NORM_EPS)
        proj = h @ w_in[l]
        dq, dk, dv, rq, rk, rv, rg = jnp.split(proj, splits, axis=-1)

        dq = _rope_half(dq.reshape(B, S, DIFF_HEADS * 2, DIFF_HEAD_DIM), pos) * (DIFF_HEAD_DIM ** -0.5)
        dk = _rope_half(dk.reshape(B, S, DIFF_HEADS * 2, DIFF_HEAD_DIM), pos)
        dq = dq.reshape(B, S, DIFF_HEADS, 2, DIFF_HEAD_DIM).transpose(0, 2, 3, 1, 4)
        dk = dk.reshape(B, S, DIFF_HEADS, 2, DIFF_HEAD_DIM).transpose(0, 2, 3, 1, 4)
        dv = dv.reshape(B, S, DIFF_HEADS, DIFF_V_DIM).transpose(0, 2, 1, 3)
        lam = (jnp.exp(jnp.sum(lambda_q1[l].astype(jnp.float32) * lambda_k1[l].astype(jnp.float32)))
               - jnp.exp(jnp.sum(lambda_q2[l].astype(jnp.float32) * lambda_k2[l].astype(jnp.float32)))
               + lam_init)
        a = _diff_attention(dq, dk, dv, lam)
        a = (_rmsnorm(a, diff_subln_g[l], HEAD_NORM_EPS) * (1.0 - lam_init)).reshape(B, S, DIFF_WIDTH)

        rq = _retnet_rotate(rq.reshape(B, S, RET_HEADS, RET_KEY_DIM), pos)
        rk = _retnet_rotate(rk.reshape(B, S, RET_HEADS, RET_KEY_DIM), pos) * (RET_KEY_DIM ** -0.5)
        rv = rv.reshape(B, S, RET_HEADS, RET_V_DIM)
        r = _head_groupnorm(_retention(rq, rk, rv), ret_norm_g[l])
        r = jax.nn.silu(rg) * r

        x = x + jnp.concatenate([a, r], axis=-1) @ w_out[l]

        h = _rmsnorm(x, ffn_norm_g[l], NORM_EPS)
        x = x + (jax.nn.silu(h @ w_gate[l]) * (h @ w_up[l])) @ w_down[l]
    return _rmsnorm(x, final_norm_g, NORM_EPS)
```

```python
import functools
import math

import jax
import jax.numpy as jnp
import numpy as np
from jax import lax
from jax.experimental import pallas as pl
from jax.experimental.pallas import tpu as pltpu

F32 = jnp.float32
BF16 = jnp.bfloat16

DIFF_HEAD_DIM = 64
DIFF_V_DIM = 128
RET_KEY_DIM = 64
RET_V_DIM = 128
ROPE_THETA = 10000.0
RET_THETA = 10000.0
NORM_EPS = 1e-6
HEAD_NORM_EPS = 1e-5
LANES = 128

NT_DIMS = (((1,), (1,)), ((), ()))
TN_DIMS = (((0,), (0,)), ((), ()))


def _dot(a, b):
    return jnp.dot(a, b, preferred_element_type=F32)


def _rms(x, g, eps):
    ms = jnp.mean(x * x, axis=-1, keepdims=True)
    return x * lax.rsqrt(ms + eps) * g


def _inproj_body(x_ref, g_ref, w_ref, ch_ref, sh_ref, ci_ref, si_ref,
                 q_ref, k_ref, v_ref, rq_ref, rk_ref, rv_ref, rg_ref, *, widths):
    dq_w, dk_w, dv_w, rq_w, rk_w, rv_w, rg_w = widths
    h = _rms(x_ref[...], g_ref[...], NORM_EPS).astype(BF16)
    ch, sh = ch_ref[...], sh_ref[...]
    ci, si = ci_ref[...], si_ref[...]
    lane = lax.broadcasted_iota(jnp.int32, ch.shape, 1)
    first_half = (lane & (DIFF_HEAD_DIM // 2)) == 0
    even = (lane & 1) == 0

    def rope(xc):
        sw = jnp.where(first_half, pltpu.roll(xc, LANES - 32, 1), pltpu.roll(xc, 32, 1))
        return xc * ch + sw * sh

    def pair_rot(xc):
        sw = jnp.where(even, pltpu.roll(xc, LANES - 1, 1), pltpu.roll(xc, 1, 1))
        return xc * ci + sw * si

    def emit(out_ref, c0, width, fn):
        p = _dot(h, w_ref[:, c0:c0 + width])
        for j in range(width // LANES):
            sl = slice(j * LANES, (j + 1) * LANES)
            out_ref[:, sl] = fn(p[:, sl]).astype(out_ref.dtype)

    c = 0
    emit(q_ref, c, dq_w, lambda t: rope(t) * (DIFF_HEAD_DIM ** -0.5)); c += dq_w
    emit(k_ref, c, dk_w, rope); c += dk_w
    emit(v_ref, c, dv_w, lambda t: t); c += dv_w
    emit(rq_ref, c, rq_w, pair_rot); c += rq_w
    emit(rk_ref, c, rk_w, lambda t: pair_rot(t) * (RET_KEY_DIM ** -0.5)); c += rk_w
    emit(rv_ref, c, rv_w, lambda t: t); c += rv_w
    emit(rg_ref, c, rg_w, lambda t: t)


def _inproj(x2d, g, w, tabs, seq, widths, tm):
    T, D = x2d.shape
    P = w.shape[1]
    nps = seq // tm
    row = lambda t: (t, 0)
    tab = lambda t: (t % nps, 0)
    const = lambda t: (0, 0)
    out_shape = [jax.ShapeDtypeStruct((T, wd), BF16) for wd in widths]
    out_specs = [pl.BlockSpec((tm, wd), row) for wd in widths]
    return pl.pallas_call(
        functools.partial(_inproj_body, widths=widths),
        grid=(T // tm,),
        in_specs=[pl.BlockSpec((tm, D), row),
                  pl.BlockSpec((1, D), const),
                  pl.BlockSpec((D, P), const, pipeline_mode=pl.Buffered(1))]
                 + [pl.BlockSpec((tm, LANES), tab)] * 4,
        out_specs=out_specs,
        out_shape=out_shape,
        compiler_params=pltpu.CompilerParams(
            dimension_semantics=("arbitrary",), vmem_limit_bytes=48 * 1024 * 1024),
        name="inproj",
    )(x2d, g, w, *tabs)


def _attn_body(lam_ref, g_ref, q_ref, k_ref, v_ref, o_ref, m_ref, l_ref, acc_ref,
               *, tq, tk, lam_init):
    i = pl.program_id(2)
    mq = 2 * tq
    q = q_ref[...]
    lane = lax.broadcasted_iota(jnp.int32, q.shape, 1)
    zero = jnp.zeros_like(q)
    q2 = jnp.concatenate([jnp.where(lane < DIFF_HEAD_DIM, q, zero),
                          jnp.where(lane >= DIFF_HEAD_DIM, q, zero)], axis=0)

    m_ref[...] = jnp.full(m_ref.shape, -jnp.inf, F32)
    l_ref[...] = jnp.zeros(l_ref.shape, F32)
    acc_ref[...] = jnp.zeros(acc_ref.shape, F32)

    def step(j, masked):
        k0 = pl.multiple_of(j * tk, tk)
        kc = k_ref[pl.ds(k0, tk), :]
        vc = v_ref[pl.ds(k0, tk), :]
        s = lax.dot_general(kc, q2, NT_DIMS, preferred_element_type=F32)
        if masked:
            kpos = k0 + lax.broadcasted_iota(jnp.int32, s.shape, 0)
            qpos = i * tq + (lax.broadcasted_iota(jnp.int32, s.shape, 1) & (tq - 1))
            s = jnp.where(kpos <= qpos, s, -jnp.inf)
        m_old = m_ref[0:1, :]
        m_new = jnp.maximum(m_old, jnp.max(s, axis=0, keepdims=True))
        alpha = jnp.exp(m_old - m_new)
        p = jnp.exp(s - m_new)
        l_ref[0:1, :] = alpha * l_ref[0:1, :] + jnp.sum(p, axis=0, keepdims=True)
        m_ref[0:1, :] = m_new
        pv = lax.dot_general(vc, p.astype(BF16), TN_DIMS, preferred_element_type=F32)
        acc_ref[...] = alpha * acc_ref[...] + pv

    n_full = (i * tq) // tk

    def full_body(j, carry):
        step(j, False)
        return carry

    lax.fori_loop(0, n_full, full_body, 0)
    for r in range(tq // tk):
        step(n_full + r, True)

    lam = (jnp.exp(jnp.sum(lam_ref[0:1, :] * lam_ref[1:2, :], axis=-1, keepdims=True))
           - jnp.exp(jnp.sum(lam_ref[2:3, :] * lam_ref[3:4, :], axis=-1, keepdims=True))
           + lam_init)
    acc = acc_ref[...]
    l = l_ref[0:1, :]
    o_t = acc[:, :tq] / l[:, :tq] - lam * (acc[:, tq:] / l[:, tq:])
    o = o_t.T
    o_ref[...] = (_rms(o, g_ref[...], HEAD_NORM_EPS) * (1.0 - lam_init)).astype(o_ref.dtype)


def _diffattn(lam_vecs, g, q, k, v, tq, tk, lam_init):
    B, S, W = q.shape
    H = W // DIFF_V_DIM
    mq = 2 * tq
    return pl.pallas_call(
        functools.partial(_attn_body, tq=tq, tk=tk, lam_init=lam_init),
        grid=(B, H, S // tq),
        in_specs=[pl.BlockSpec(lam_vecs.shape, lambda b, h, i: (0, 0)),
                  pl.BlockSpec(g.shape, lambda b, h, i: (0, 0)),
                  pl.BlockSpec((None, tq, LANES), lambda b, h, i: (b, i, h)),
                  pl.BlockSpec((None, S, LANES), lambda b, h, i: (b, 0, h)),
                  pl.BlockSpec((None, S, LANES), lambda b, h, i: (b, 0, h))],
        out_specs=pl.BlockSpec((None, tq, LANES), lambda b, h, i: (b, i, h)),
        out_shape=jax.ShapeDtypeStruct((B, S, W), BF16),
        scratch_shapes=[pltpu.VMEM((8, mq), F32), pltpu.VMEM((8, mq), F32),
                        pltpu.VMEM((DIFF_V_DIM, mq), F32)],
        compiler_params=pltpu.CompilerParams(
            dimension_semantics=("arbitrary", "arbitrary", "arbitrary"),
            vmem_limit_bytes=48 * 1024 * 1024),
        name="diffattn",
    )(lam_vecs, g, q, k, v)


def _ret_body(dmat_ref, qdec_ref, kdec_ref, cdec_ref, g_ref, rq_ref, rk_ref, rv_ref, rg_ref,
              o_ref, state_ref):
    @pl.when(pl.program_id(1) == 0)
    def _():
        state_ref[...] = jnp.zeros(state_ref.shape, F32)

    q = rq_ref[...]
    k = rk_ref[...]
    C = q.shape[0]
    lane = lax.broadcasted_iota(jnp.int32, (C, LANES), 1)
    row = lax.broadcasted_iota(jnp.int32, (LANES, LANES), 0)
    n_pairs = q.shape[1] // LANES
    for p in range(n_pairs):
        qp = q[:, p * LANES:(p + 1) * LANES]
        kp = k[:, p * LANES:(p + 1) * LANES]
        state = state_ref[p]
        state_b = state.astype(BF16)
        upd = []
        for hh in range(2):
            h = 2 * p + hh
            hs = slice(h * RET_V_DIM, (h + 1) * RET_V_DIM)
            sel = (lane < RET_KEY_DIM) if hh == 0 else (lane >= RET_KEY_DIM)
            qm = jnp.where(sel, qp, jnp.zeros_like(qp))
            vh = rv_ref[:, hs]
            s = lax.dot_general(qm, kp, NT_DIMS, preferred_element_type=F32) * dmat_ref[h]
            inner = _dot(s.astype(BF16), vh)
            cross = _dot(qm, state_b) * qdec_ref[h]
            o = inner + cross
            mu = jnp.mean(o, axis=-1, keepdims=True)
            cen = o - mu
            var = jnp.mean(cen * cen, axis=-1, keepdims=True)
            y = cen * lax.rsqrt(var + HEAD_NORM_EPS) * g_ref[:, hs]
            gate = rg_ref[:, hs].astype(F32)
            o_ref[:, hs] = (gate * jax.nn.sigmoid(gate) * y).astype(o_ref.dtype)
            vd = (vh.astype(F32) * kdec_ref[h]).astype(BF16)
            upd.append(lax.dot_general(kp, vd, TN_DIMS, preferred_element_type=F32))
        state_ref[p] = cdec_ref[p] * state + jnp.where(row < RET_KEY_DIM, upd[0], upd[1])


def _retention(consts, g, rq, rk, rv, rg, chunk):
    B, S, KW = rq.shape
    VW = rv.shape[2]
    dmat, qdec, kdec, cdec = consts
    full = lambda a: pl.BlockSpec(a.shape, lambda b, c: (0,) * a.ndim)
    tok = lambda w: pl.BlockSpec((None, chunk, w), lambda b, c: (b, c, 0))
    return pl.pallas_call(
        _ret_body,
        grid=(B, S // chunk),
        in_specs=[full(dmat), full(qdec), full(kdec), full(cdec), full(g),
                  tok(KW), tok(KW), tok(VW), tok(VW)],
        out_specs=tok(VW),
        out_shape=jax.ShapeDtypeStruct((B, S, VW), BF16),
        scratch_shapes=[pltpu.VMEM((KW // LANES, LANES, LANES), F32)],
        compiler_params=pltpu.CompilerParams(
            dimension_semantics=("arbitrary", "arbitrary"), vmem_limit_bytes=32 * 1024 * 1024),
        name="retention",
    )(dmat, qdec, kdec, cdec, g, rq, rk, rv, rg)


def _retention_consts(n_heads, chunk):
    log_g = np.log1p(-(2.0 ** (-5.0 - np.arange(n_heads, dtype=np.float64))))
    n = np.arange(chunk, dtype=np.float64)
    rel = n[:, None] - n[None, :]
    dmat = np.where(rel >= 0, np.exp(log_g[:, None, None] * np.maximum(rel, 0.0)), 0.0)
    qdec = np.exp(log_g[:, None] * (n + 1.0)[None, :])
    kdec = np.exp(log_g[:, None] * (chunk - 1.0 - n)[None, :])
    cdec = np.exp(log_g * chunk)
    bl = lambda a: np.broadcast_to(a[:, :, None], a.shape + (LANES,))
    cdec_rows = np.repeat(cdec, RET_KEY_DIM).reshape(n_heads // 2, LANES)
    return tuple(jnp.asarray(a, F32) for a in (dmat, bl(qdec), bl(kdec), bl(cdec_rows)))


def _outffn_body(x_ref, a_ref, r_ref, wo_ref, g2_ref, wg_ref, wu_ref, wd_ref, gf_ref, o_ref):
    mix = jnp.concatenate([a_ref[...], r_ref[...]], axis=1)
    x1 = x_ref[...] + _dot(mix, wo_ref[...])
    h = _rms(x1, g2_ref[...], NORM_EPS).astype(BF16)
    gate = _dot(h, wg_ref[...])
    up = _dot(h, wu_ref[...])
    act = (gate * jax.nn.sigmoid(gate) * up).astype(BF16)
    x2 = x1 + _dot(act, wd_ref[...])
    o_ref[...] = _rms(x2, gf_ref[...], NORM_EPS)


def _outffn(x2d, a, r, wo, g2, wg, wu, wd, gf, tm):
    T, D = x2d.shape
    row = lambda t: (t, 0)
    const = lambda t: (0, 0)
    resident = lambda arr: pl.BlockSpec(arr.shape, const, pipeline_mode=pl.Buffered(1))
    return pl.pallas_call(
        _outffn_body,
        grid=(T // tm,),
        in_specs=[pl.BlockSpec((tm, D), row),
                  pl.BlockSpec((tm, a.shape[1]), row),
                  pl.BlockSpec((tm, r.shape[1]), row),
                  resident(wo), pl.BlockSpec(g2.shape, const),
                  resident(wg), resident(wu), resident(wd),
                  pl.BlockSpec(gf.shape, const)],
        out_specs=pl.BlockSpec((tm, D), row),
        out_shape=jax.ShapeDtypeStruct((T, D), F32),
        compiler_params=pltpu.CompilerParams(
            dimension_semantics=("arbitrary",), vmem_limit_bytes=56 * 1024 * 1024),
        name="outffn",
    )(x2d, a, r, wo, g2, wg, wu, wd, gf)


def _rotation_tables(seq):
    pos = jnp.arange(seq, dtype=F32)
    d = DIFF_HEAD_DIM
    inv = ROPE_THETA ** (-jnp.arange(0, d, 2, dtype=F32) / d)
    ang = pos[:, None] * inv[None, :]
    cos, sin = jnp.cos(ang), jnp.sin(ang)
    reps = LANES // d
    cos_h = jnp.tile(jnp.concatenate([cos, cos], axis=-1), (1, reps))
    sin_h = jnp.tile(jnp.concatenate([-sin, sin], axis=-1), (1, reps))
    dr = RET_KEY_DIM
    inv_r = 1.0 / (RET_THETA ** jnp.linspace(0.0, 1.0, dr // 2, dtype=F32))
    ang_r = pos[:, None] * inv_r[None, :]
    cos_r, sin_r = jnp.cos(ang_r), jnp.sin(ang_r)
    cos_i = jnp.tile(jnp.repeat(cos_r, 2, axis=-1), (1, LANES // dr))
    sin_i = jnp.tile(jnp.stack([-sin_r, sin_r], axis=-1).reshape(seq, dr), (1, LANES // dr))
    return cos_h, sin_h, cos_i, sin_i


def kernel(x, attn_norm_g, w_in, lambda_q1, lambda_k1, lambda_q2, lambda_k2, diff_subln_g,
           ret_norm_g, w_out, ffn_norm_g, w_gate, w_up, w_down, final_norm_g):
    B, S, D = x.shape
    depth = w_in.shape[0]
    diff_w = D // 2
    ret_w = D - diff_w
    ret_heads = ret_w // RET_V_DIM
    widths = (diff_w, diff_w, diff_w, ret_heads * RET_KEY_DIM, ret_heads * RET_KEY_DIM, ret_w, ret_w)
    tabs = _rotation_tables(S)
    ret_chunk = 128
    ret_consts = _retention_consts(ret_heads, ret_chunk)
    xs = x.reshape(B * S, D)
    for l in range(depth):
        lam_init = 0.8 - 0.6 * math.exp(-0.3 * l)
        q, k, v, rq, rk, rv, rg = _inproj(
            xs, attn_norm_g[l][None, :], w_in[l].astype(BF16), tabs, S, widths, tm=512)
        tok3 = lambda t: t.reshape(B, S, t.shape[-1])
        lam_vecs = jnp.stack([lambda_q1[l], lambda_k1[l], lambda_q2[l], lambda_k2[l]]).astype(F32)
        a = _diffattn(lam_vecs, diff_subln_g[l][None, :], tok3(q), tok3(k), tok3(v),
                      tq=512, tk=512, lam_init=lam_init)
        r = _retention(ret_consts, ret_norm_g[l][None, :], tok3(rq), tok3(rk), tok3(rv), tok3(rg),
                       ret_chunk)
        is_last = l == depth - 1
        gf = final_norm_g[None, :] if is_last else jnp.ones((1, D), F32)
        xs = _outffn(xs, a.reshape(B * S, diff_w), r.reshape(B * S, ret_w), w_out[l].astype(BF16),
                     ffn_norm_g[l][None, :], w_gate[l].astype(BF16), w_up[l].astype(BF16),
                     w_down[l].astype(BF16), gf, tm=512)
        if not is_last:
            raise NotImplementedError("multi-layer stacks need an un-normalised residual output")
    return xs.reshape(B, S, D)
```

```python
import functools
import math

import jax
import jax.numpy as jnp
import numpy as np
from jax import lax
from jax.experimental import pallas as pl
from jax.experimental.pallas import tpu as pltpu

F32 = jnp.float32
BF16 = jnp.bfloat16

DIFF_HEAD_DIM = 64
DIFF_V_DIM = 128
RET_KEY_DIM = 64
RET_V_DIM = 128
ROPE_THETA = 10000.0
RET_THETA = 10000.0
NORM_EPS = 1e-6
HEAD_NORM_EPS = 1e-5
LANES = 128
BF16_ROWS = 16
LOG2E = math.log2(math.e)

NT_DIMS = (((1,), (1,)), ((), ()))
TN_DIMS = (((0,), (0,)), ((), ()))


def _dot(a, b):
    return jnp.dot(a, b, preferred_element_type=F32)


def _rms(x, g, eps):
    ms = jnp.mean(x * x, axis=-1, keepdims=True)
    return x * lax.rsqrt(ms + eps) * g


def _inproj_body(x_ref, g_ref, w_ref, wvt_ref, ch_ref, sh_ref, ci_ref, si_ref,
                 q_ref, k_ref, vt_ref, rq_ref, rk_ref, rv_ref, rg_ref, *, widths):
    dq_w, dk_w, dv_w, rq_w, rk_w, rv_w, rg_w = widths
    h = _rms(x_ref[...], g_ref[...], NORM_EPS).astype(BF16)
    ch, sh = ch_ref[...], sh_ref[...]
    ci, si = ci_ref[...], si_ref[...]
    lane = lax.broadcasted_iota(jnp.int32, ch.shape, 1)
    first_half = (lane & (DIFF_HEAD_DIM // 2)) == 0
    even = (lane & 1) == 0

    def rope(xc):
        sw = jnp.where(first_half, pltpu.roll(xc, LANES - 32, 1), pltpu.roll(xc, 32, 1))
        return xc * ch + sw * sh

    def pair_rot(xc):
        sw = jnp.where(even, pltpu.roll(xc, LANES - 1, 1), pltpu.roll(xc, 1, 1))
        return xc * ci + sw * si

    def emit(out_ref, c0, width, fn):
        p = _dot(h, w_ref[:, c0:c0 + width])
        for j in range(width // LANES):
            sl = slice(j * LANES, (j + 1) * LANES)
            out_ref[:, sl] = fn(p[:, sl]).astype(out_ref.dtype)

    c = 0
    emit(q_ref, c, dq_w, lambda t: rope(t) * (DIFF_HEAD_DIM ** -0.5 * LOG2E)); c += dq_w
    emit(k_ref, c, dk_w, rope); c += dk_w
    vt = lax.dot_general(wvt_ref[...], h, NT_DIMS, preferred_element_type=F32)
    tkc = vt_ref.shape[-1]
    for j in range(vt_ref.shape[0]):
        vt_ref[j] = vt[:, j * tkc:(j + 1) * tkc].astype(vt_ref.dtype)
    c += dv_w
    emit(rq_ref, c, rq_w, pair_rot); c += rq_w
    emit(rk_ref, c, rk_w, lambda t: pair_rot(t) * (RET_KEY_DIM ** -0.5)); c += rk_w
    emit(rv_ref, c, rv_w, lambda t: t); c += rv_w
    emit(rg_ref, c, rg_w, lambda t: t)


def _inproj(x2d, g, w, wvt, tabs, seq, widths, tm, tk):
    T, D = x2d.shape
    P = w.shape[1]
    nps = seq // tm
    row = lambda t: (t, 0)
    tab = lambda t: (t % nps, 0)
    const = lambda t: (0, 0)
    vt_index = 2
    out_shape = [jax.ShapeDtypeStruct((T, wd), BF16) for wd in widths]
    out_specs = [pl.BlockSpec((tm, wd), row) for wd in widths]
    out_shape[vt_index] = jax.ShapeDtypeStruct((T // seq, seq // tk, widths[vt_index], tk), BF16)
    out_specs[vt_index] = pl.BlockSpec((None, tm // tk, widths[vt_index], tk),
                                       lambda t: (t // nps, t % nps, 0, 0))
    return pl.pallas_call(
        functools.partial(_inproj_body, widths=widths),
        grid=(T // tm,),
        in_specs=[pl.BlockSpec((tm, D), row),
                  pl.BlockSpec((1, D), const),
                  pl.BlockSpec((D, P), const, pipeline_mode=pl.Buffered(1)),
                  pl.BlockSpec(wvt.shape, const, pipeline_mode=pl.Buffered(1))]
                 + [pl.BlockSpec((tm, LANES), tab)] * 4,
        out_specs=out_specs,
        out_shape=out_shape,
        compiler_params=pltpu.CompilerParams(
            dimension_semantics=("arbitrary",), vmem_limit_bytes=48 * 1024 * 1024),
        name="inproj",
    )(x2d, g, w, wvt, *tabs)


def _attn_body(lam_ref, g_ref, q_ref, k_ref, vt_ref, o_ref, q2_ref, m_ref, acc_ref,
               sa_ref, sb_ref, mxa_ref, mxb_ref, *, tq, lam_init):
    i = pl.program_id(2)
    tk = tq // 2
    mq = 2 * tq
    groups = [slice(g * tk, (g + 1) * tk) for g in range(mq // tk)]
    all_groups = tuple(range(len(groups)))
    q = q_ref[...]
    lane = lax.broadcasted_iota(jnp.int32, q.shape, 1)
    zero = jnp.zeros_like(q)
    q2_ref[0:tq, :] = jnp.where(lane < DIFF_HEAD_DIM, q, zero)
    q2_ref[tq:mq, :] = jnp.where(lane >= DIFF_HEAD_DIM, q, zero)
    m_ref[...] = jnp.full(m_ref.shape, -jnp.inf, F32)
    acc_ref[...] = jnp.zeros(acc_ref.shape, F32)
    ones = jnp.ones((BF16_ROWS, tk), BF16)
    causal = (lax.broadcasted_iota(jnp.int32, (tk, tk), 0)
              <= lax.broadcasted_iota(jnp.int32, (tk, tk), 1))

    def produce(j, g, s_ref, mx_ref):
        kc = k_ref[pl.ds(pl.multiple_of(j * tk, tk), tk), :]
        s = lax.dot_general(kc, q2_ref[groups[g], :], NT_DIMS, preferred_element_type=F32)
        s_ref[:, groups[g]] = s
        mx_ref[0:1, groups[g]] = jnp.max(s, axis=0, keepdims=True)

    def consume(va, g, s_ref, mx_ref, diagonal):
        cs = groups[g]
        m_old = m_ref[0:1, cs]
        if diagonal:
            s = jnp.where(causal, s_ref[:, cs], -jnp.inf)
            m_new = jnp.maximum(m_old, jnp.max(s, axis=0, keepdims=True))
        else:
            s = s_ref[:, cs]
            m_new = jnp.maximum(m_old, mx_ref[0:1, cs])
        alpha = jnp.exp2(m_old - m_new)
        p = jnp.exp2(s - m_new).astype(BF16)
        m_ref[0:1, cs] = m_new
        acc_ref[:, cs] = alpha * acc_ref[:, cs] + _dot(va, p)

    def step(j, cur, nxt, use=all_groups, diagonal=(), prefetch=all_groups):
        va = jnp.concatenate([vt_ref[j], ones], axis=0)
        for g in all_groups:
            if g in prefetch:
                produce(j + 1, g, *nxt)
            if g in use:
                consume(va, g, *cur, diagonal=g in diagonal)

    buf_a = (sa_ref, mxa_ref)
    buf_b = (sb_ref, mxb_ref)
    for g in all_groups:
        produce(0, g, *buf_a)

    def pair_body(t, carry):
        step(2 * t, buf_a, buf_b)
        step(2 * t + 1, buf_b, buf_a)
        return carry

    lax.fori_loop(0, i, pair_body, 0)
    step(2 * i, buf_a, buf_b, diagonal=(0, 2), prefetch=(1, 3))
    step(2 * i + 1, buf_b, buf_a, use=(1, 3), diagonal=(1, 3), prefetch=())

    lam = (jnp.exp(jnp.sum(lam_ref[0:1, :] * lam_ref[1:2, :], axis=-1, keepdims=True))
           - jnp.exp(jnp.sum(lam_ref[2:3, :] * lam_ref[3:4, :], axis=-1, keepdims=True))
           + lam_init)
    num = acc_ref[0:DIFF_V_DIM, :]
    den = acc_ref[DIFF_V_DIM:DIFF_V_DIM + 1, :]
    o_t = num[:, :tq] / den[:, :tq] - lam * (num[:, tq:] / den[:, tq:])
    o = o_t.T
    o_ref[...] = (_rms(o, g_ref[...], HEAD_NORM_EPS) * (1.0 - lam_init)).astype(o_ref.dtype)


def _diffattn(lam_vecs, g, q, k, vt, tq, lam_init):
    B, S, W = q.shape
    tk = vt.shape[3]
    assert tq == 2 * tk
    mq = 2 * tq
    return pl.pallas_call(
        functools.partial(_attn_body, tq=tq, lam_init=lam_init),
        grid=(B, W // DIFF_V_DIM, S // tq),
        in_specs=[pl.BlockSpec(lam_vecs.shape, lambda b, h, i: (0, 0)),
                  pl.BlockSpec(g.shape, lambda b, h, i: (0, 0)),
                  pl.BlockSpec((None, tq, LANES), lambda b, h, i: (b, i, h)),
                  pl.BlockSpec((None, S, LANES), lambda b, h, i: (b, 0, h)),
                  pl.BlockSpec((None, S // tk, DIFF_V_DIM, tk), lambda b, h, i: (b, 0, h, 0))],
        out_specs=pl.BlockSpec((None, tq, LANES), lambda b, h, i: (b, i, h)),
        out_shape=jax.ShapeDtypeStruct((B, S, W), BF16),
        scratch_shapes=[pltpu.VMEM((mq, LANES), BF16),
                        pltpu.VMEM((8, mq), F32),
                        pltpu.VMEM((DIFF_V_DIM + BF16_ROWS, mq), F32),
                        pltpu.VMEM((tk, mq), F32), pltpu.VMEM((tk, mq), F32),
                        pltpu.VMEM((8, mq), F32), pltpu.VMEM((8, mq), F32)],
        compiler_params=pltpu.CompilerParams(
            dimension_semantics=("arbitrary", "arbitrary", "arbitrary"),
            vmem_limit_bytes=48 * 1024 * 1024),
        name="diffattn",
    )(lam_vecs, g, q, k, vt)


def _ret_body(dmat_ref, qdec_ref, kdec_ref, cdec_ref, g_ref, rq_ref, rk_ref, rv_ref, rg_ref,
              o_ref, state_ref):
    @pl.when(pl.program_id(1) == 0)
    def _():
        state_ref[...] = jnp.zeros(state_ref.shape, F32)

    q = rq_ref[...]
    k = rk_ref[...]
    C = q.shape[0]
    lane = lax.broadcasted_iota(jnp.int32, (C, LANES), 1)
    row = lax.broadcasted_iota(jnp.int32, (LANES, LANES), 0)
    n_pairs = q.shape[1] // LANES
    for p in range(n_pairs):
        qp = q[:, p * LANES:(p + 1) * LANES]
        kp = k[:, p * LANES:(p + 1) * LANES]
        state = state_ref[p]
        state_b = state.astype(BF16)
        upd = []
        for hh in range(2):
            h = 2 * p + hh
            hs = slice(h * RET_V_DIM, (h + 1) * RET_V_DIM)
            sel = (lane < RET_KEY_DIM) if hh == 0 else (lane >= RET_KEY_DIM)
            qm = jnp.where(sel, qp, jnp.zeros_like(qp))
            vh = rv_ref[:, hs]
            s = lax.dot_general(qm, kp, NT_DIMS, preferred_element_type=F32) * dmat_ref[h]
            inner = _dot(s.astype(BF16), vh)
            cross = _dot(qm, state_b) * qdec_ref[h]
            o = inner + cross
            mu = jnp.mean(o, axis=-1, keepdims=True)
            cen = o - mu
            var = jnp.mean(cen * cen, axis=-1, keepdims=True)
            y = cen * lax.rsqrt(var + HEAD_NORM_EPS) * g_ref[:, hs]
            gate = rg_ref[:, hs].astype(F32)
            o_ref[:, hs] = (gate * jax.nn.sigmoid(gate) * y).astype(o_ref.dtype)
            vd = (vh.astype(F32) * kdec_ref[h]).astype(BF16)
            upd.append(lax.dot_general(kp, vd, TN_DIMS, preferred_element_type=F32))
        state_ref[p] = cdec_ref[p] * state + jnp.where(row < RET_KEY_DIM, upd[0], upd[1])


def _retention(consts, g, rq, rk, rv, rg, chunk):
    B, S, KW = rq.shape
    VW = rv.shape[2]
    dmat, qdec, kdec, cdec = consts
    full = lambda a: pl.BlockSpec(a.shape, lambda b, c: (0,) * a.ndim)
    tok = lambda w: pl.BlockSpec((None, chunk, w), lambda b, c: (b, c, 0))
    return pl.pallas_call(
        _ret_body,
        grid=(B, S // chunk),
        in_specs=[full(dmat), full(qdec), full(kdec), full(cdec), full(g),
                  tok(KW), tok(KW), tok(VW), tok(VW)],
        out_specs=tok(VW),
        out_shape=jax.ShapeDtypeStruct((B, S, VW), BF16),
        scratch_shapes=[pltpu.VMEM((KW // LANES, LANES, LANES), F32)],
        compiler_params=pltpu.CompilerParams(
            dimension_semantics=("arbitrary", "arbitrary"), vmem_limit_bytes=32 * 1024 * 1024),
        name="retention",
    )(dmat, qdec, kdec, cdec, g, rq, rk, rv, rg)


def _retention_consts(n_heads, chunk):
    log_g = np.log1p(-(2.0 ** (-5.0 - np.arange(n_heads, dtype=np.float64))))
    n = np.arange(chunk, dtype=np.float64)
    rel = n[:, None] - n[None, :]
    dmat = np.where(rel >= 0, np.exp(log_g[:, None, None] * np.maximum(rel, 0.0)), 0.0)
    qdec = np.exp(log_g[:, None] * (n + 1.0)[None, :])
    kdec = np.exp(log_g[:, None] * (chunk - 1.0 - n)[None, :])
    cdec = np.exp(log_g * chunk)
    bl = lambda a: np.broadcast_to(a[:, :, None], a.shape + (LANES,))
    cdec_rows = np.repeat(cdec, RET_KEY_DIM).reshape(n_heads // 2, LANES)
    return tuple(jnp.asarray(a, F32) for a in (dmat, bl(qdec), bl(kdec), bl(cdec_rows)))


def _outffn_body(x_ref, a_ref, r_ref, wo_ref, g2_ref, wg_ref, wu_ref, wd_ref, gf_ref, o_ref):
    mix = jnp.concatenate([a_ref[...], r_ref[...]], axis=1)
    x1 = x_ref[...] + _dot(mix, wo_ref[...])
    h = _rms(x1, g2_ref[...], NORM_EPS).astype(BF16)
    gate = _dot(h, wg_ref[...])
    up = _dot(h, wu_ref[...])
    act = (gate * jax.nn.sigmoid(gate) * up).astype(BF16)
    x2 = x1 + _dot(act, wd_ref[...])
    o_ref[...] = _rms(x2, gf_ref[...], NORM_EPS)


def _outffn(x2d, a, r, wo, g2, wg, wu, wd, gf, tm):
    T, D = x2d.shape
    row = lambda t: (t, 0)
    const = lambda t: (0, 0)
    resident = lambda arr: pl.BlockSpec(arr.shape, const, pipeline_mode=pl.Buffered(1))
    return pl.pallas_call(
        _outffn_body,
        grid=(T // tm,),
        in_specs=[pl.BlockSpec((tm, D), row),
                  pl.BlockSpec((tm, a.shape[1]), row),
                  pl.BlockSpec((tm, r.shape[1]), row),
                  resident(wo), pl.BlockSpec(g2.shape, const),
                  resident(wg), resident(wu), resident(wd),
                  pl.BlockSpec(gf.shape, const)],
        out_specs=pl.BlockSpec((tm, D), row),
        out_shape=jax.ShapeDtypeStruct((T, D), F32),
        compiler_params=pltpu.CompilerParams(
            dimension_semantics=("arbitrary",), vmem_limit_bytes=56 * 1024 * 1024),
        name="outffn",
    )(x2d, a, r, wo, g2, wg, wu, wd, gf)


def _rotation_tables(seq):
    pos = jnp.arange(seq, dtype=F32)
    d = DIFF_HEAD_DIM
    inv = ROPE_THETA ** (-jnp.arange(0, d, 2, dtype=F32) / d)
    ang = pos[:, None] * inv[None, :]
    cos, sin = jnp.cos(ang), jnp.sin(ang)
    reps = LANES // d
    cos_h = jnp.tile(jnp.concatenate([cos, cos], axis=-1), (1, reps))
    sin_h = jnp.tile(jnp.concatenate([-sin, sin], axis=-1), (1, reps))
    dr = RET_KEY_DIM
    inv_r = 1.0 / (RET_THETA ** jnp.linspace(0.0, 1.0, dr // 2, dtype=F32))
    ang_r = pos[:, None] * inv_r[None, :]
    cos_r, sin_r = jnp.cos(ang_r), jnp.sin(ang_r)
    cos_i = jnp.tile(jnp.repeat(cos_r, 2, axis=-1), (1, LANES // dr))
    sin_i = jnp.tile(jnp.stack([-sin_r, sin_r], axis=-1).reshape(seq, dr), (1, LANES // dr))
    return cos_h, sin_h, cos_i, sin_i


def kernel(x, attn_norm_g, w_in, lambda_q1, lambda_k1, lambda_q2, lambda_k2, diff_subln_g,
           ret_norm_g, w_out, ffn_norm_g, w_gate, w_up, w_down, final_norm_g):
    B, S, D = x.shape
    assert w_in.shape[0] == 1, "single-layer block only"
    l = 0
    diff_w = D // 2
    ret_w = D - diff_w
    ret_heads = ret_w // RET_V_DIM
    widths = (diff_w, diff_w, diff_w, ret_heads * RET_KEY_DIM, ret_heads * RET_KEY_DIM, ret_w, ret_w)
    tabs = _rotation_tables(S)
    ret_chunk = 128
    ret_consts = _retention_consts(ret_heads, ret_chunk)
    xs = x.reshape(B * S, D)
    lam_init = 0.8 - 0.6 * math.exp(-0.3 * l)
    w_in_b = w_in[l].astype(BF16)
    wvt = w_in_b[:, 2 * diff_w:3 * diff_w].T
    q, k, vt, rq, rk, rv, rg = _inproj(
        xs, attn_norm_g[l][None, :], w_in_b, wvt, tabs, S, widths, tm=512, tk=256)
    tok3 = lambda t: t.reshape(B, S, t.shape[-1])
    lam_vecs = jnp.stack([lambda_q1[l], lambda_k1[l], lambda_q2[l], lambda_k2[l]]).astype(F32)
    a = _diffattn(lam_vecs, diff_subln_g[l][None, :], tok3(q), tok3(k), vt, tq=512, lam_init=lam_init)
    r = _retention(ret_consts, ret_norm_g[l][None, :], tok3(rq), tok3(rk), tok3(rv), tok3(rg),
                   ret_chunk)
    out = _outffn(xs, a.reshape(B * S, diff_w), r.reshape(B * S, ret_w), w_out[l].astype(BF16),
                  ffn_norm_g[l][None, :], w_gate[l].astype(BF16), w_up[l].astype(BF16),
                  w_down[l].astype(BF16), final_norm_g[None, :], tm=512)
    return out.reshape(B, S, D)
```

```python
import functools
import math

import jax
import jax.numpy as jnp
import numpy as np
from jax import lax
from jax.experimental import pallas as pl
from jax.experimental.pallas import tpu as pltpu

F32 = jnp.float32
BF16 = jnp.bfloat16

DIFF_HEAD_DIM = 64
DIFF_V_DIM = 128
RET_KEY_DIM = 64
RET_V_DIM = 128
ROPE_THETA = 10000.0
RET_THETA = 10000.0
NORM_EPS = 1e-6
HEAD_NORM_EPS = 1e-5
LANES = 128
BF16_ROWS = 16
LOG2E = math.log2(math.e)

NT_DIMS = (((1,), (1,)), ((), ()))
TN_DIMS = (((0,), (0,)), ((), ()))


def _dot(a, b):
    return jnp.dot(a, b, preferred_element_type=F32)


def _rms(x, g, eps):
    ms = jnp.mean(x * x, axis=-1, keepdims=True)
    return x * lax.rsqrt(ms + eps) * g


def _inproj_body(x_ref, g_ref, w_ref, wvt_ref, ch_ref, sh_ref, ci_ref, si_ref,
                 q_ref, k_ref, vt_ref, rq_ref, rk_ref, rv_ref, rg_ref, *, widths):
    dq_w, dk_w, dv_w, rq_w, rk_w, rv_w, rg_w = widths
    h = _rms(x_ref[...], g_ref[...], NORM_EPS).astype(BF16)
    ch, sh = ch_ref[...], sh_ref[...]
    ci, si = ci_ref[...], si_ref[...]
    lane = lax.broadcasted_iota(jnp.int32, ch.shape, 1)
    first_half = (lane & (DIFF_HEAD_DIM // 2)) == 0
    even = (lane & 1) == 0

    def rope(xc):
        sw = jnp.where(first_half, pltpu.roll(xc, LANES - 32, 1), pltpu.roll(xc, 32, 1))
        return xc * ch + sw * sh

    def pair_rot(xc):
        sw = jnp.where(even, pltpu.roll(xc, LANES - 1, 1), pltpu.roll(xc, 1, 1))
        return xc * ci + sw * si

    def emit(out_ref, c0, width, fn):
        p = _dot(h, w_ref[:, c0:c0 + width])
        for j in range(width // LANES):
            sl = slice(j * LANES, (j + 1) * LANES)
            out_ref[:, sl] = fn(p[:, sl]).astype(out_ref.dtype)

    c = 0
    emit(q_ref, c, dq_w, lambda t: rope(t) * (DIFF_HEAD_DIM ** -0.5 * LOG2E)); c += dq_w
    emit(k_ref, c, dk_w, rope); c += dk_w
    vt = lax.dot_general(wvt_ref[...], h, NT_DIMS, preferred_element_type=F32)
    tkc = vt_ref.shape[-1]
    for j in range(vt_ref.shape[0]):
        vt_ref[j] = vt[:, j * tkc:(j + 1) * tkc].astype(vt_ref.dtype)
    c += dv_w
    emit(rq_ref, c, rq_w, pair_rot); c += rq_w
    emit(rk_ref, c, rk_w, lambda t: pair_rot(t) * (RET_KEY_DIM ** -0.5)); c += rk_w
    emit(rv_ref, c, rv_w, lambda t: t); c += rv_w
    emit(rg_ref, c, rg_w, lambda t: t)


def _inproj(x2d, g, w, wvt, tabs, seq, widths, tm, tk):
    T, D = x2d.shape
    P = w.shape[1]
    nps = seq // tm
    row = lambda t: (t, 0)
    tab = lambda t: (t % nps, 0)
    const = lambda t: (0, 0)
    vt_index = 2
    out_shape = [jax.ShapeDtypeStruct((T, wd), BF16) for wd in widths]
    out_specs = [pl.BlockSpec((tm, wd), row) for wd in widths]
    out_shape[vt_index] = jax.ShapeDtypeStruct((T // seq, seq // tk, widths[vt_index], tk), BF16)
    out_specs[vt_index] = pl.BlockSpec((None, tm // tk, widths[vt_index], tk),
                                       lambda t: (t // nps, t % nps, 0, 0))
    return pl.pallas_call(
        functools.partial(_inproj_body, widths=widths),
        grid=(T // tm,),
        in_specs=[pl.BlockSpec((tm, D), row),
                  pl.BlockSpec((1, D), const),
                  pl.BlockSpec((D, P), const, pipeline_mode=pl.Buffered(1)),
                  pl.BlockSpec(wvt.shape, const, pipeline_mode=pl.Buffered(1))]
                 + [pl.BlockSpec((tm, LANES), tab)] * 4,
        out_specs=out_specs,
        out_shape=out_shape,
        compiler_params=pltpu.CompilerParams(
            dimension_semantics=("arbitrary",), vmem_limit_bytes=48 * 1024 * 1024),
        name="inproj",
    )(x2d, g, w, wvt, *tabs)


def _attn_body(lam_ref, g_ref, q_ref, k_ref, vt_ref, o_ref, q2_ref, m_ref, acc_ref,
               sa_ref, sb_ref, mxa_ref, mxb_ref, *, tq, gw, lam_init):
    i = pl.program_id(2)
    tk = tq // 2
    mq = 2 * tq
    groups = [slice(g * gw, (g + 1) * gw) for g in range(mq // gw)]
    all_groups = tuple(range(len(groups)))
    q = q_ref[...]
    lane = lax.broadcasted_iota(jnp.int32, q.shape, 1)
    zero = jnp.zeros_like(q)
    q2_ref[0:tq, :] = jnp.where(lane < DIFF_HEAD_DIM, q, zero)
    q2_ref[tq:mq, :] = jnp.where(lane >= DIFF_HEAD_DIM, q, zero)
    m_ref[...] = jnp.full(m_ref.shape, -jnp.inf, F32)
    acc_ref[...] = jnp.zeros(acc_ref.shape, F32)
    ones = jnp.ones((BF16_ROWS, tk), BF16)
    key_off = lax.broadcasted_iota(jnp.int32, (tk, gw), 0)
    qry_off = lax.broadcasted_iota(jnp.int32, (tk, gw), 1)

    def visibility(r, g):
        q0 = (g * gw) % tq
        k0 = r * tk
        if k0 + tk - 1 <= q0:
            return "all", None
        if k0 > q0 + gw - 1:
            return "none", None
        return "some", k0 - q0

    def produce(j, g, s_ref, mx_ref):
        kc = k_ref[pl.ds(pl.multiple_of(j * tk, tk), tk), :]
        s = lax.dot_general(kc, q2_ref[groups[g], :], NT_DIMS, preferred_element_type=F32)
        s_ref[:, groups[g]] = s
        mx_ref[0:1, groups[g]] = jnp.max(s, axis=0, keepdims=True)

    def consume(va, g, s_ref, mx_ref, shift):
        cs = groups[g]
        m_old = m_ref[0:1, cs]
        if shift is None:
            s = s_ref[:, cs]
            m_new = jnp.maximum(m_old, mx_ref[0:1, cs])
        else:
            s = jnp.where(key_off + shift <= qry_off, s_ref[:, cs], -jnp.inf)
            m_new = jnp.maximum(m_old, jnp.max(s, axis=0, keepdims=True))
        alpha = jnp.exp2(m_old - m_new)
        p = jnp.exp2(s - m_new).astype(BF16)
        m_ref[0:1, cs] = m_new
        acc_ref[:, cs] = alpha * acc_ref[:, cs] + _dot(va, p)

    def step(j, cur, nxt, diag=None):
        va = jnp.concatenate([vt_ref[j], ones], axis=0)
        for g in all_groups:
            if diag is None or (diag == 0 and visibility(1, g)[0] != "none"):
                produce(j + 1, g, *nxt)
            kind, shift = ("all", None) if diag is None else visibility(diag, g)
            if kind != "none":
                consume(va, g, *cur, shift=shift)

    buf_a = (sa_ref, mxa_ref)
    buf_b = (sb_ref, mxb_ref)
    for g in all_groups:
        produce(0, g, *buf_a)

    def pair_body(t, carry):
        step(2 * t, buf_a, buf_b)
        step(2 * t + 1, buf_b, buf_a)
        return carry

    lax.fori_loop(0, i, pair_body, 0)
    step(2 * i, buf_a, buf_b, diag=0)
    step(2 * i + 1, buf_b, buf_a, diag=1)

    lam = (jnp.exp(jnp.sum(lam_ref[0:1, :] * lam_ref[1:2, :], axis=-1, keepdims=True))
           - jnp.exp(jnp.sum(lam_ref[2:3, :] * lam_ref[3:4, :], axis=-1, keepdims=True))
           + lam_init)
    num = acc_ref[0:DIFF_V_DIM, :]
    den = acc_ref[DIFF_V_DIM:DIFF_V_DIM + 1, :]
    o_t = num[:, :tq] / den[:, :tq] - lam * (num[:, tq:] / den[:, tq:])
    o = o_t.T
    o_ref[...] = (_rms(o, g_ref[...], HEAD_NORM_EPS) * (1.0 - lam_init)).astype(o_ref.dtype)


def _diffattn(lam_vecs, g, q, k, vt, tq, gw, lam_init):
    B, S, W = q.shape
    tk = vt.shape[3]
    assert tq == 2 * tk and tq % gw == 0
    mq = 2 * tq
    return pl.pallas_call(
        functools.partial(_attn_body, tq=tq, gw=gw, lam_init=lam_init),
        grid=(B, W // DIFF_V_DIM, S // tq),
        in_specs=[pl.BlockSpec(lam_vecs.shape, lambda b, h, i: (0, 0)),
                  pl.BlockSpec(g.shape, lambda b, h, i: (0, 0)),
                  pl.BlockSpec((None, tq, LANES), lambda b, h, i: (b, i, h)),
                  pl.BlockSpec((None, S, LANES), lambda b, h, i: (b, 0, h)),
                  pl.BlockSpec((None, S // tk, DIFF_V_DIM, tk), lambda b, h, i: (b, 0, h, 0))],
        out_specs=pl.BlockSpec((None, tq, LANES), lambda b, h, i: (b, i, h)),
        out_shape=jax.ShapeDtypeStruct((B, S, W), BF16),
        scratch_shapes=[pltpu.VMEM((mq, LANES), BF16),
                        pltpu.VMEM((8, mq), F32),
                        pltpu.VMEM((DIFF_V_DIM + BF16_ROWS, mq), F32),
                        pltpu.VMEM((tk, mq), F32), pltpu.VMEM((tk, mq), F32),
                        pltpu.VMEM((8, mq), F32), pltpu.VMEM((8, mq), F32)],
        compiler_params=pltpu.CompilerParams(
            dimension_semantics=("arbitrary", "arbitrary", "arbitrary"),
            vmem_limit_bytes=48 * 1024 * 1024),
        name="diffattn",
    )(lam_vecs, g, q, k, vt)


def _ret_body(dmat_ref, qdec_ref, kdec_ref, cdec_ref, g_ref, rq_ref, rk_ref, rv_ref, rg_ref,
              o_ref, state_ref):
    @pl.when(pl.program_id(1) == 0)
    def _():
        state_ref[...] = jnp.zeros(state_ref.shape, F32)

    q = rq_ref[...]
    k = rk_ref[...]
    C = q.shape[0]
    lane = lax.broadcasted_iota(jnp.int32, (C, LANES), 1)
    row = lax.broadcasted_iota(jnp.int32, (LANES, LANES), 0)
    n_pairs = q.shape[1] // LANES
    for p in range(n_pairs):
        qp = q[:, p * LANES:(p + 1) * LANES]
        kp = k[:, p * LANES:(p + 1) * LANES]
        state = state_ref[p]
        state_b = state.astype(BF16)
        upd = []
        for hh in range(2):
            h = 2 * p + hh
            hs = slice(h * RET_V_DIM, (h + 1) * RET_V_DIM)
            sel = (lane < RET_KEY_DIM) if hh == 0 else (lane >= RET_KEY_DIM)
            qm = jnp.where(sel, qp, jnp.zeros_like(qp))
            vh = rv_ref[:, hs]
            s = lax.dot_general(qm, kp, NT_DIMS, preferred_element_type=F32) * dmat_ref[h]
            inner = _dot(s.astype(BF16), vh)
            cross = _dot(qm, state_b) * qdec_ref[h]
            o = inner + cross
            mu = jnp.mean(o, axis=-1, keepdims=True)
            cen = o - mu
            var = jnp.mean(cen * cen, axis=-1, keepdims=True)
            y = cen * lax.rsqrt(var + HEAD_NORM_EPS) * g_ref[:, hs]
            gate = rg_ref[:, hs].astype(F32)
            o_ref[:, hs] = (gate * jax.nn.sigmoid(gate) * y).astype(o_ref.dtype)
            vd = (vh.astype(F32) * kdec_ref[h]).astype(BF16)
            upd.append(lax.dot_general(kp, vd, TN_DIMS, preferred_element_type=F32))
        state_ref[p] = cdec_ref[p] * state + jnp.where(row < RET_KEY_DIM, upd[0], upd[1])


def _retention(consts, g, rq, rk, rv, rg, chunk):
    B, S, KW = rq.shape
    VW = rv.shape[2]
    dmat, qdec, kdec, cdec = consts
    full = lambda a: pl.BlockSpec(a.shape, lambda b, c: (0,) * a.ndim)
    tok = lambda w: pl.BlockSpec((None, chunk, w), lambda b, c: (b, c, 0))
    return pl.pallas_call(
        _ret_body,
        grid=(B, S // chunk),
        in_specs=[full(dmat), full(qdec), full(kdec), full(cdec), full(g),
                  tok(KW), tok(KW), tok(VW), tok(VW)],
        out_specs=tok(VW),
        out_shape=jax.ShapeDtypeStruct((B, S, VW), BF16),
        scratch_shapes=[pltpu.VMEM((KW // LANES, LANES, LANES), F32)],
        compiler_params=pltpu.CompilerParams(
            dimension_semantics=("arbitrary", "arbitrary"), vmem_limit_bytes=32 * 1024 * 1024),
        name="retention",
    )(dmat, qdec, kdec, cdec, g, rq, rk, rv, rg)


def _retention_consts(n_heads, chunk):
    log_g = np.log1p(-(2.0 ** (-5.0 - np.arange(n_heads, dtype=np.float64))))
    n = np.arange(chunk, dtype=np.float64)
    rel = n[:, None] - n[None, :]
    dmat = np.where(rel >= 0, np.exp(log_g[:, None, None] * np.maximum(rel, 0.0)), 0.0)
    qdec = np.exp(log_g[:, None] * (n + 1.0)[None, :])
    kdec = np.exp(log_g[:, None] * (chunk - 1.0 - n)[None, :])
    cdec = np.exp(log_g * chunk)
    bl = lambda a: np.broadcast_to(a[:, :, None], a.shape + (LANES,))
    cdec_rows = np.repeat(cdec, RET_KEY_DIM).reshape(n_heads // 2, LANES)
    return tuple(jnp.asarray(a, F32) for a in (dmat, bl(qdec), bl(kdec), bl(cdec_rows)))


def _outffn_body(x_ref, a_ref, r_ref, wo_ref, g2_ref, wg_ref, wu_ref, wd_ref, gf_ref, o_ref):
    mix = jnp.concatenate([a_ref[...], r_ref[...]], axis=1)
    x1 = x_ref[...] + _dot(mix, wo_ref[...])
    h = _rms(x1, g2_ref[...], NORM_EPS).astype(BF16)
    gate = _dot(h, wg_ref[...])
    up = _dot(h, wu_ref[...])
    act = (gate * jax.nn.sigmoid(gate) * up).astype(BF16)
    x2 = x1 + _dot(act, wd_ref[...])
    o_ref[...] = _rms(x2, gf_ref[...], NORM_EPS)


def _outffn(x2d, a, r, wo, g2, wg, wu, wd, gf, tm):
    T, D = x2d.shape
    row = lambda t: (t, 0)
    const = lambda t: (0, 0)
    resident = lambda arr: pl.BlockSpec(arr.shape, const, pipeline_mode=pl.Buffered(1))
    return pl.pallas_call(
        _outffn_body,
        grid=(T // tm,),
        in_specs=[pl.BlockSpec((tm, D), row),
                  pl.BlockSpec((tm, a.shape[1]), row),
                  pl.BlockSpec((tm, r.shape[1]), row),
                  resident(wo), pl.BlockSpec(g2.shape, const),
                  resident(wg), resident(wu), resident(wd),
                  pl.BlockSpec(gf.shape, const)],
        out_specs=pl.BlockSpec((tm, D), row),
        out_shape=jax.ShapeDtypeStruct((T, D), F32),
        compiler_params=pltpu.CompilerParams(
            dimension_semantics=("arbitrary",), vmem_limit_bytes=56 * 1024 * 1024),
        name="outffn",
    )(x2d, a, r, wo, g2, wg, wu, wd, gf)


def _rotation_tables(seq):
    pos = jnp.arange(seq, dtype=F32)
    d = DIFF_HEAD_DIM
    inv = ROPE_THETA ** (-jnp.arange(0, d, 2, dtype=F32) / d)
    ang = pos[:, None] * inv[None, :]
    cos, sin = jnp.cos(ang), jnp.sin(ang)
    reps = LANES // d
    cos_h = jnp.tile(jnp.concatenate([cos, cos], axis=-1), (1, reps))
    sin_h = jnp.tile(jnp.concatenate([-sin, sin], axis=-1), (1, reps))
    dr = RET_KEY_DIM
    inv_r = 1.0 / (RET_THETA ** jnp.linspace(0.0, 1.0, dr // 2, dtype=F32))
    ang_r = pos[:, None] * inv_r[None, :]
    cos_r, sin_r = jnp.cos(ang_r), jnp.sin(ang_r)
    cos_i = jnp.tile(jnp.repeat(cos_r, 2, axis=-1), (1, LANES // dr))
    sin_i = jnp.tile(jnp.stack([-sin_r, sin_r], axis=-1).reshape(seq, dr), (1, LANES // dr))
    return cos_h, sin_h, cos_i, sin_i


def kernel(x, attn_norm_g, w_in, lambda_q1, lambda_k1, lambda_q2, lambda_k2, diff_subln_g,
           ret_norm_g, w_out, ffn_norm_g, w_gate, w_up, w_down, final_norm_g):
    B, S, D = x.shape
    assert w_in.shape[0] == 1, "single-layer block only"
    l = 0
    diff_w = D // 2
    ret_w = D - diff_w
    ret_heads = ret_w // RET_V_DIM
    widths = (diff_w, diff_w, diff_w, ret_heads * RET_KEY_DIM, ret_heads * RET_KEY_DIM, ret_w, ret_w)
    tabs = _rotation_tables(S)
    ret_chunk = 128
    ret_consts = _retention_consts(ret_heads, ret_chunk)
    xs = x.reshape(B * S, D)
    lam_init = 0.8 - 0.6 * math.exp(-0.3 * l)
    w_in_b = w_in[l].astype(BF16)
    wvt = w_in_b[:, 2 * diff_w:3 * diff_w].T
    q, k, vt, rq, rk, rv, rg = _inproj(
        xs, attn_norm_g[l][None, :], w_in_b, wvt, tabs, S, widths, tm=512, tk=512)
    tok3 = lambda t: t.reshape(B, S, t.shape[-1])
    lam_vecs = jnp.stack([lambda_q1[l], lambda_k1[l], lambda_q2[l], lambda_k2[l]]).astype(F32)
    a = _diffattn(lam_vecs, diff_subln_g[l][None, :], tok3(q), tok3(k), vt, tq=1024, gw=256,
                  lam_init=lam_init)
    r = _retention(ret_consts, ret_norm_g[l][None, :], tok3(rq), tok3(rk), tok3(rv), tok3(rg),
                   ret_chunk)
    out = _outffn(xs, a.reshape(B * S, diff_w), r.reshape(B * S, ret_w), w_out[l].astype(BF16),
                  ffn_norm_g[l][None, :], w_gate[l].astype(BF16), w_up[l].astype(BF16),
                  w_down[l].astype(BF16), final_norm_g[None, :], tm=512)
    return out.reshape(B, S, D)
```

```python
import functools
import math

import jax
import jax.numpy as jnp
import numpy as np
from jax import lax
from jax.experimental import pallas as pl
from jax.experimental.pallas import tpu as pltpu

F32 = jnp.float32
BF16 = jnp.bfloat16

DIFF_HEAD_DIM = 64
DIFF_V_DIM = 128
RET_KEY_DIM = 64
RET_V_DIM = 128
ROPE_THETA = 10000.0
RET_THETA = 10000.0
NORM_EPS = 1e-6
HEAD_NORM_EPS = 1e-5
LANES = 128
BF16_ROWS = 16
LOG2E = math.log2(math.e)

NT_DIMS = (((1,), (1,)), ((), ()))
TN_DIMS = (((0,), (0,)), ((), ()))


def _dot(a, b):
    return jnp.dot(a, b, preferred_element_type=F32)


def _rms(x, g, eps):
    ms = jnp.mean(x * x, axis=-1, keepdims=True)
    return x * lax.rsqrt(ms + eps) * g


def _retention_tile(rq_ref, rk_ref, rv_ref, rg_ref, o_ref, state_ref,
                    dmat_ref, qdec_ref, kdec_ref, cdec_ref, g_ref, chunk):
    n_chunks = rq_ref.shape[0] // chunk
    n_pairs = rq_ref.shape[1] // LANES
    lane = lax.broadcasted_iota(jnp.int32, (chunk, LANES), 1)
    row = lax.broadcasted_iota(jnp.int32, (LANES, LANES), 0)
    rows = [slice(c * chunk, (c + 1) * chunk) for c in range(n_chunks)]
    head_lanes = [lane < RET_KEY_DIM, lane >= RET_KEY_DIM]

    def q_masked(c, p, hh):
        qp = rq_ref[rows[c], p * LANES:(p + 1) * LANES]
        return jnp.where(head_lanes[hh], qp, jnp.zeros_like(qp))

    scores, updates = {}, {}
    for c in range(n_chunks):
        for p in range(n_pairs):
            kp = rk_ref[rows[c], p * LANES:(p + 1) * LANES]
            for hh in range(2):
                h = 2 * p + hh
                vh = rv_ref[rows[c], h * RET_V_DIM:(h + 1) * RET_V_DIM]
                s = lax.dot_general(q_masked(c, p, hh), kp, NT_DIMS, preferred_element_type=F32)
                scores[c, h] = (s * dmat_ref[h]).astype(BF16)
                vd = (vh.astype(F32) * kdec_ref[h]).astype(BF16)
                updates[c, h] = lax.dot_general(kp, vd, TN_DIMS, preferred_element_type=F32)

    states = {}
    for p in range(n_pairs):
        st = state_ref[p]
        for c in range(n_chunks):
            states[c, p] = st.astype(BF16)
            st = cdec_ref[p] * st + jnp.where(row < RET_KEY_DIM, updates[c, 2 * p], updates[c, 2 * p + 1])
        state_ref[p] = st

    for c in range(n_chunks):
        for p in range(n_pairs):
            for hh in range(2):
                h = 2 * p + hh
                hs = slice(h * RET_V_DIM, (h + 1) * RET_V_DIM)
                cross = _dot(q_masked(c, p, hh), states[c, p]) * qdec_ref[h]
                o = _dot(scores[c, h], rv_ref[rows[c], hs]) + cross
                mu = jnp.mean(o, axis=-1, keepdims=True)
                cen = o - mu
                var = jnp.mean(cen * cen, axis=-1, keepdims=True)
                y = cen * lax.rsqrt(var + HEAD_NORM_EPS) * g_ref[:, hs]
                gate = rg_ref[rows[c], hs]
                o_ref[rows[c], hs] = (gate * jax.nn.sigmoid(gate) * y).astype(o_ref.dtype)


def _inproj_body(x_ref, g_ref, w_ref, wvt_ref, ch_ref, sh_ref, ci_ref, si_ref,
                 dmat_ref, qdec_ref, kdec_ref, cdec_ref, gr_ref,
                 q_ref, k_ref, vt_ref, r_ref,
                 rq_ref, rk_ref, rv_ref, rg_ref, state_ref, *, widths, tiles_per_seq, chunk):
    dq_w, dk_w, dv_w, rq_w, rk_w, rv_w, rg_w = widths

    @pl.when(pl.program_id(0) % tiles_per_seq == 0)
    def _():
        state_ref[...] = jnp.zeros(state_ref.shape, F32)

    h = _rms(x_ref[...], g_ref[...], NORM_EPS).astype(BF16)
    ch, sh = ch_ref[...], sh_ref[...]
    ci, si = ci_ref[...], si_ref[...]
    lane = lax.broadcasted_iota(jnp.int32, ch.shape, 1)
    first_half = (lane & (DIFF_HEAD_DIM // 2)) == 0
    even = (lane & 1) == 0

    def rope(xc):
        sw = jnp.where(first_half, pltpu.roll(xc, LANES - 32, 1), pltpu.roll(xc, 32, 1))
        return xc * ch + sw * sh

    def pair_rot(xc):
        sw = jnp.where(even, pltpu.roll(xc, LANES - 1, 1), pltpu.roll(xc, 1, 1))
        return xc * ci + sw * si

    def emit(out_ref, c0, width, fn):
        p = _dot(h, w_ref[:, c0:c0 + width])
        for j in range(width // LANES):
            sl = slice(j * LANES, (j + 1) * LANES)
            out_ref[:, sl] = fn(p[:, sl]).astype(out_ref.dtype)

    c = 0
    emit(q_ref, c, dq_w, lambda t: rope(t) * (DIFF_HEAD_DIM ** -0.5 * LOG2E)); c += dq_w
    emit(k_ref, c, dk_w, rope); c += dk_w
    vt = lax.dot_general(wvt_ref[...], h, NT_DIMS, preferred_element_type=F32)
    tkc = vt_ref.shape[-1]
    for j in range(vt_ref.shape[0]):
        vt_ref[j] = vt[:, j * tkc:(j + 1) * tkc].astype(vt_ref.dtype)
    c += dv_w
    emit(rq_ref, c, rq_w, pair_rot); c += rq_w
    emit(rk_ref, c, rk_w, lambda t: pair_rot(t) * (RET_KEY_DIM ** -0.5)); c += rk_w
    emit(rv_ref, c, rv_w, lambda t: t); c += rv_w
    emit(rg_ref, c, rg_w, lambda t: t)
    _retention_tile(rq_ref, rk_ref, rv_ref, rg_ref, r_ref, state_ref,
                    dmat_ref, qdec_ref, kdec_ref, cdec_ref, gr_ref, chunk)


def _inproj(x2d, g, w, wvt, tabs, ret_consts, gr, seq, widths, tm, tk, chunk):
    T, D = x2d.shape
    P = w.shape[1]
    nps = seq // tm
    dq_w, dk_w, dv_w, rq_w, rk_w, rv_w, rg_w = widths
    row = lambda t: (t, 0)
    tab = lambda t: (t % nps, 0)
    const = lambda t: (0, 0)
    full = lambda a: pl.BlockSpec(a.shape, lambda t: (0,) * a.ndim)
    out_shape = [jax.ShapeDtypeStruct((T, dq_w), BF16),
                 jax.ShapeDtypeStruct((T, dk_w), BF16),
                 jax.ShapeDtypeStruct((T // seq, seq // tk, dv_w, tk), BF16),
                 jax.ShapeDtypeStruct((T, rv_w), BF16)]
    out_specs = [pl.BlockSpec((tm, dq_w), row),
                 pl.BlockSpec((tm, dk_w), row),
                 pl.BlockSpec((None, tm // tk, dv_w, tk), lambda t: (t // nps, t % nps, 0, 0)),
                 pl.BlockSpec((tm, rv_w), row)]
    return pl.pallas_call(
        functools.partial(_inproj_body, widths=widths, tiles_per_seq=nps, chunk=chunk),
        grid=(T // tm,),
        in_specs=[pl.BlockSpec((tm, D), row),
                  pl.BlockSpec((1, D), const),
                  pl.BlockSpec((D, P), const, pipeline_mode=pl.Buffered(1)),
                  pl.BlockSpec(wvt.shape, const, pipeline_mode=pl.Buffered(1))]
                 + [pl.BlockSpec((tm, LANES), tab)] * 4
                 + [full(a) for a in ret_consts] + [full(gr)],
        out_specs=out_specs,
        out_shape=out_shape,
        scratch_shapes=[pltpu.VMEM((tm, rq_w), BF16), pltpu.VMEM((tm, rk_w), BF16),
                        pltpu.VMEM((tm, rv_w), BF16), pltpu.VMEM((tm, rg_w), F32),
                        pltpu.VMEM((rq_w // LANES, LANES, LANES), F32)],
        compiler_params=pltpu.CompilerParams(
            dimension_semantics=("arbitrary",), vmem_limit_bytes=48 * 1024 * 1024),
        name="inproj",
    )(x2d, g, w, wvt, *tabs, *ret_consts, gr)


def _attn_body(lam_ref, g_ref, q_ref, k_ref, vt_ref, o_ref, q2_ref, m_ref, acc_ref,
               sa_ref, sb_ref, mxa_ref, mxb_ref, *, tq, gw, lam_init):
    i = pl.program_id(2)
    tk = tq // 2
    mq = 2 * tq
    groups = [slice(g * gw, (g + 1) * gw) for g in range(mq // gw)]
    all_groups = tuple(range(len(groups)))
    q = q_ref[...]
    lane = lax.broadcasted_iota(jnp.int32, q.shape, 1)
    zero = jnp.zeros_like(q)
    q2_ref[0:tq, :] = jnp.where(lane < DIFF_HEAD_DIM, q, zero)
    q2_ref[tq:mq, :] = jnp.where(lane >= DIFF_HEAD_DIM, q, zero)
    m_ref[...] = jnp.full(m_ref.shape, -jnp.inf, F32)
    acc_ref[...] = jnp.zeros(acc_ref.shape, F32)
    ones = jnp.ones((BF16_ROWS, tk), BF16)
    key_off = lax.broadcasted_iota(jnp.int32, (tk, gw), 0)
    qry_off = lax.broadcasted_iota(jnp.int32, (tk, gw), 1)

    def visibility(r, g):
        q0 = (g * gw) % tq
        k0 = r * tk
        if k0 + tk - 1 <= q0:
            return "all", None
        if k0 > q0 + gw - 1:
            return "none", None
        return "some", k0 - q0

    def produce(j, g, s_ref, mx_ref):
        kc = k_ref[pl.ds(pl.multiple_of(j * tk, tk), tk), :]
        s = lax.dot_general(kc, q2_ref[groups[g], :], NT_DIMS, preferred_element_type=F32)
        s_ref[:, groups[g]] = s
        mx_ref[0:1, groups[g]] = jnp.max(s, axis=0, keepdims=True)

    def consume(va, g, s_ref, mx_ref, shift):
        cs = groups[g]
        m_old = m_ref[0:1, cs]
        if shift is None:
            s = s_ref[:, cs]
            m_new = jnp.maximum(m_old, mx_ref[0:1, cs])
        else:
            s = jnp.where(key_off + shift <= qry_off, s_ref[:, cs], -jnp.inf)
            m_new = jnp.maximum(m_old, jnp.max(s, axis=0, keepdims=True))
        alpha = jnp.exp2(m_old - m_new)
        p = jnp.exp2(s - m_new).astype(BF16)
        m_ref[0:1, cs] = m_new
        acc_ref[:, cs] = alpha * acc_ref[:, cs] + _dot(va, p)

    def step(j, cur, nxt, diag=None):
        va = jnp.concatenate([vt_ref[j], ones], axis=0)
        for g in all_groups:
            if diag is None or (diag == 0 and visibility(1, g)[0] != "none"):
                produce(j + 1, g, *nxt)
            kind, shift = ("all", None) if diag is None else visibility(diag, g)
            if kind != "none":
                consume(va, g, *cur, shift=shift)

    buf_a = (sa_ref, mxa_ref)
    buf_b = (sb_ref, mxb_ref)
    for g in all_groups:
        produce(0, g, *buf_a)

    def pair_body(t, carry):
        step(2 * t, buf_a, buf_b)
        step(2 * t + 1, buf_b, buf_a)
        return carry

    lax.fori_loop(0, i, pair_body, 0)
    step(2 * i, buf_a, buf_b, diag=0)
    step(2 * i + 1, buf_b, buf_a, diag=1)

    lam = (jnp.exp(jnp.sum(lam_ref[0:1, :] * lam_ref[1:2, :], axis=-1, keepdims=True))
           - jnp.exp(jnp.sum(lam_ref[2:3, :] * lam_ref[3:4, :], axis=-1, keepdims=True))
           + lam_init)
    num = acc_ref[0:DIFF_V_DIM, :]
    den = acc_ref[DIFF_V_DIM:DIFF_V_DIM + 1, :]
    o_t = num[:, :tq] / den[:, :tq] - lam * (num[:, tq:] / den[:, tq:])
    o = o_t.T
    o_ref[...] = (_rms(o, g_ref[...], HEAD_NORM_EPS) * (1.0 - lam_init)).astype(o_ref.dtype)


def _diffattn(lam_vecs, g, q, k, vt, tq, gw, lam_init):
    B, S, W = q.shape
    tk = vt.shape[3]
    assert tq == 2 * tk and tq % gw == 0
    mq = 2 * tq
    return pl.pallas_call(
        functools.partial(_attn_body, tq=tq, gw=gw, lam_init=lam_init),
        grid=(B, W // DIFF_V_DIM, S // tq),
        in_specs=[pl.BlockSpec(lam_vecs.shape, lambda b, h, i: (0, 0)),
                  pl.BlockSpec(g.shape, lambda b, h, i: (0, 0)),
                  pl.BlockSpec((None, tq, LANES), lambda b, h, i: (b, i, h)),
                  pl.BlockSpec((None, S, LANES), lambda b, h, i: (b, 0, h)),
                  pl.BlockSpec((None, S // tk, DIFF_V_DIM, tk), lambda b, h, i: (b, 0, h, 0))],
        out_specs=pl.BlockSpec((None, tq, LANES), lambda b, h, i: (b, i, h)),
        out_shape=jax.ShapeDtypeStruct((B, S, W), BF16),
        scratch_shapes=[pltpu.VMEM((mq, LANES), BF16),
                        pltpu.VMEM((8, mq), F32),
                        pltpu.VMEM((DIFF_V_DIM + BF16_ROWS, mq), F32),
                        pltpu.VMEM((tk, mq), F32), pltpu.VMEM((tk, mq), F32),
                        pltpu.VMEM((8, mq), F32), pltpu.VMEM((8, mq), F32)],
        compiler_params=pltpu.CompilerParams(
            dimension_semantics=("arbitrary", "arbitrary", "arbitrary"),
            vmem_limit_bytes=48 * 1024 * 1024),
        name="diffattn",
    )(lam_vecs, g, q, k, vt)


def _retention_consts(n_heads, chunk):
    log_g = np.log1p(-(2.0 ** (-5.0 - np.arange(n_heads, dtype=np.float64))))
    n = np.arange(chunk, dtype=np.float64)
    rel = n[:, None] - n[None, :]
    dmat = np.where(rel >= 0, np.exp(log_g[:, None, None] * np.maximum(rel, 0.0)), 0.0)
    qdec = np.exp(log_g[:, None] * (n + 1.0)[None, :])
    kdec = np.exp(log_g[:, None] * (chunk - 1.0 - n)[None, :])
    cdec = np.exp(log_g * chunk)
    bl = lambda a: np.broadcast_to(a[:, :, None], a.shape + (LANES,))
    cdec_rows = np.repeat(cdec, RET_KEY_DIM).reshape(n_heads // 2, LANES)
    return tuple(jnp.asarray(a, F32) for a in (dmat, bl(qdec), bl(kdec), bl(cdec_rows)))


def _outffn_body(x_ref, a_ref, r_ref, wo_ref, g2_ref, wg_ref, wu_ref, wd_ref, gf_ref, o_ref):
    mix = jnp.concatenate([a_ref[...], r_ref[...]], axis=1)
    x1 = x_ref[...] + _dot(mix, wo_ref[...])
    h = _rms(x1, g2_ref[...], NORM_EPS).astype(BF16)
    gate = _dot(h, wg_ref[...])
    up = _dot(h, wu_ref[...])
    act = (gate * jax.nn.sigmoid(gate) * up).astype(BF16)
    x2 = x1 + _dot(act, wd_ref[...])
    o_ref[...] = _rms(x2, gf_ref[...], NORM_EPS)


def _outffn(x2d, a, r, wo, g2, wg, wu, wd, gf, tm):
    T, D = x2d.shape
    row = lambda t: (t, 0)
    const = lambda t: (0, 0)
    resident = lambda arr: pl.BlockSpec(arr.shape, const, pipeline_mode=pl.Buffered(1))
    return pl.pallas_call(
        _outffn_body,
        grid=(T // tm,),
        in_specs=[pl.BlockSpec((tm, D), row),
                  pl.BlockSpec((tm, a.shape[1]), row),
                  pl.BlockSpec((tm, r.shape[1]), row),
                  resident(wo), pl.BlockSpec(g2.shape, const),
                  resident(wg), resident(wu), resident(wd),
                  pl.BlockSpec(gf.shape, const)],
        out_specs=pl.BlockSpec((tm, D), row),
        out_shape=jax.ShapeDtypeStruct((T, D), F32),
        compiler_params=pltpu.CompilerParams(
            dimension_semantics=("arbitrary",), vmem_limit_bytes=56 * 1024 * 1024),
        name="outffn",
    )(x2d, a, r, wo, g2, wg, wu, wd, gf)


def _rotation_tables(seq):
    pos = jnp.arange(seq, dtype=F32)
    d = DIFF_HEAD_DIM
    inv = ROPE_THETA ** (-jnp.arange(0, d, 2, dtype=F32) / d)
    ang = pos[:, None] * inv[None, :]
    cos, sin = jnp.cos(ang), jnp.sin(ang)
    reps = LANES // d
    cos_h = jnp.tile(jnp.concatenate([cos, cos], axis=-1), (1, reps))
    sin_h = jnp.tile(jnp.concatenate([-sin, sin], axis=-1), (1, reps))
    dr = RET_KEY_DIM
    inv_r = 1.0 / (RET_THETA ** jnp.linspace(0.0, 1.0, dr // 2, dtype=F32))
    ang_r = pos[:, None] * inv_r[None, :]
    cos_r, sin_r = jnp.cos(ang_r), jnp.sin(ang_r)
    cos_i = jnp.tile(jnp.repeat(cos_r, 2, axis=-1), (1, LANES // dr))
    sin_i = jnp.tile(jnp.stack([-sin_r, sin_r], axis=-1).reshape(seq, dr), (1, LANES // dr))
    return cos_h, sin_h, cos_i, sin_i


def kernel(x, attn_norm_g, w_in, lambda_q1, lambda_k1, lambda_q2, lambda_k2, diff_subln_g,
           ret_norm_g, w_out, ffn_norm_g, w_gate, w_up, w_down, final_norm_g):
    B, S, D = x.shape
    assert w_in.shape[0] == 1, "single-layer block only"
    l = 0
    diff_w = D // 2
    ret_w = D - diff_w
    ret_heads = ret_w // RET_V_DIM
    widths = (diff_w, diff_w, diff_w, ret_heads * RET_KEY_DIM, ret_heads * RET_KEY_DIM, ret_w, ret_w)
    tabs = _rotation_tables(S)
    ret_chunk = 128
    ret_consts = _retention_consts(ret_heads, ret_chunk)
    xs = x.reshape(B * S, D)
    lam_init = 0.8 - 0.6 * math.exp(-0.3 * l)
    w_in_b = w_in[l].astype(BF16)
    wvt = w_in_b[:, 2 * diff_w:3 * diff_w].T
    q, k, vt, r = _inproj(
        xs, attn_norm_g[l][None, :], w_in_b, wvt, tabs, ret_consts, ret_norm_g[l][None, :], S, widths,
        tm=512, tk=512, chunk=ret_chunk)
    tok3 = lambda t: t.reshape(B, S, t.shape[-1])
    lam_vecs = jnp.stack([lambda_q1[l], lambda_k1[l], lambda_q2[l], lambda_k2[l]]).astype(F32)
    a = _diffattn(lam_vecs, diff_subln_g[l][None, :], tok3(q), tok3(k), vt, tq=1024, gw=256,
                  lam_init=lam_init)
    out = _outffn(xs, a.reshape(B * S, diff_w), r, w_out[l].astype(BF16),
                  ffn_norm_g[l][None, :], w_gate[l].astype(BF16), w_up[l].astype(BF16),
                  w_down[l].astype(BF16), final_norm_g[None, :], tm=512)
    return out.reshape(B, S, D)
```

```python
import functools
import math

import jax
import jax.numpy as jnp
import numpy as np
from jax import lax
from jax.experimental import pallas as pl
from jax.experimental.pallas import tpu as pltpu

F32 = jnp.float32
BF16 = jnp.bfloat16

DIFF_HEAD_DIM = 64
DIFF_V_DIM = 128
RET_KEY_DIM = 64
RET_V_DIM = 128
ROPE_THETA = 10000.0
RET_THETA = 10000.0
NORM_EPS = 1e-6
HEAD_NORM_EPS = 1e-5
LANES = 128
BF16_ROWS = 16
LOG2E = math.log2(math.e)

NT_DIMS = (((1,), (1,)), ((), ()))
TN_DIMS = (((0,), (0,)), ((), ()))


def _dot(a, b):
    return jnp.dot(a, b, preferred_element_type=F32)


def _rms(x, g, eps):
    ms = jnp.mean(x * x, axis=-1, keepdims=True)
    return x * lax.rsqrt(ms + eps) * g


def _retention_tile(rq_ref, rk_ref, rv_ref, rg_ref, o_ref, state_ref,
                    dmat_ref, qdec_ref, kdec_ref, cdec_ref, g_ref, chunk):
    n_chunks = rq_ref.shape[0] // chunk
    n_pairs = rq_ref.shape[1] // LANES
    lane = lax.broadcasted_iota(jnp.int32, (chunk, LANES), 1)
    row = lax.broadcasted_iota(jnp.int32, (LANES, LANES), 0)
    rows = [slice(c * chunk, (c + 1) * chunk) for c in range(n_chunks)]
    head_lanes = [lane < RET_KEY_DIM, lane >= RET_KEY_DIM]

    def q_masked(c, p, hh):
        qp = rq_ref[rows[c], p * LANES:(p + 1) * LANES]
        return jnp.where(head_lanes[hh], qp, jnp.zeros_like(qp))

    scores, updates = {}, {}
    for c in range(n_chunks):
        for p in range(n_pairs):
            kp = rk_ref[rows[c], p * LANES:(p + 1) * LANES]
            for hh in range(2):
                h = 2 * p + hh
                vh = rv_ref[rows[c], h * RET_V_DIM:(h + 1) * RET_V_DIM]
                s = lax.dot_general(q_masked(c, p, hh), kp, NT_DIMS, preferred_element_type=F32)
                scores[c, h] = (s * dmat_ref[h]).astype(BF16)
                vd = (vh.astype(F32) * kdec_ref[h]).astype(BF16)
                updates[c, h] = lax.dot_general(kp, vd, TN_DIMS, preferred_element_type=F32)

    states = {}
    for p in range(n_pairs):
        st = state_ref[p]
        for c in range(n_chunks):
            states[c, p] = st.astype(BF16)
            st = cdec_ref[p] * st + jnp.where(row < RET_KEY_DIM, updates[c, 2 * p], updates[c, 2 * p + 1])
        state_ref[p] = st

    for c in range(n_chunks):
        for p in range(n_pairs):
            for hh in range(2):
                h = 2 * p + hh
                hs = slice(h * RET_V_DIM, (h + 1) * RET_V_DIM)
                cross = _dot(q_masked(c, p, hh), states[c, p]) * qdec_ref[h]
                o = _dot(scores[c, h], rv_ref[rows[c], hs]) + cross
                mu = jnp.mean(o, axis=-1, keepdims=True)
                cen = o - mu
                var = jnp.mean(cen * cen, axis=-1, keepdims=True)
                y = cen * lax.rsqrt(var + HEAD_NORM_EPS) * g_ref[:, hs]
                gate = rg_ref[rows[c], hs]
                o_ref[rows[c], hs] = (gate * jax.nn.sigmoid(gate) * y).astype(o_ref.dtype)


def _inproj_body(x_ref, g_ref, w_ref, base_ref, off_ref,
                 dmat_ref, qdec_ref, kdec_ref, cdec_ref, gr_ref,
                 q_ref, k_ref, vt_ref, r_ref,
                 rq_ref, rk_ref, rv_ref, rg_ref, state_ref, wvt_ref,
                 *, widths, tiles_per_seq, chunk):
    dq_w, dk_w, dv_w, rq_w, rk_w, rv_w, rg_w = widths
    tile_in_seq = pl.program_id(0) % tiles_per_seq

    @pl.when(pl.program_id(0) == 0)
    def _():
        wv = w_ref[:, dq_w + dk_w:dq_w + dk_w + dv_w]
        wvt_ref[...] = wv.astype(F32).T.astype(wvt_ref.dtype)

    @pl.when(tile_in_seq == 0)
    def _():
        state_ref[...] = jnp.zeros(state_ref.shape, F32)

    h = _rms(x_ref[...], g_ref[...], NORM_EPS).astype(BF16)
    lane = lax.broadcasted_iota(jnp.int32, off_ref.shape[1:], 1)
    first_half = (lane & (DIFF_HEAD_DIM // 2)) == 0
    even = (lane & 1) == 0

    def angle_tables(kind, negate_where):
        cb = base_ref[2 * kind, pl.ds(tile_in_seq, 1), :]
        sb = base_ref[2 * kind + 1, pl.ds(tile_in_seq, 1), :]
        co, so = off_ref[2 * kind], off_ref[2 * kind + 1]
        sin = sb * co + cb * so
        return cb * co - sb * so, jnp.where(negate_where, -sin, sin)

    ch, sh = angle_tables(0, first_half)
    ci, si = angle_tables(1, even)

    def rope(xc):
        sw = jnp.where(first_half, pltpu.roll(xc, LANES - 32, 1), pltpu.roll(xc, 32, 1))
        return xc * ch + sw * sh

    def pair_rot(xc):
        sw = jnp.where(even, pltpu.roll(xc, LANES - 1, 1), pltpu.roll(xc, 1, 1))
        return xc * ci + sw * si

    def emit(out_ref, c0, width, fn):
        p = _dot(h, w_ref[:, c0:c0 + width])
        for j in range(width // LANES):
            sl = slice(j * LANES, (j + 1) * LANES)
            out_ref[:, sl] = fn(p[:, sl]).astype(out_ref.dtype)

    c = 0
    emit(q_ref, c, dq_w, lambda t: rope(t) * (DIFF_HEAD_DIM ** -0.5 * LOG2E)); c += dq_w
    emit(k_ref, c, dk_w, rope); c += dk_w
    vt = lax.dot_general(wvt_ref[...], h, NT_DIMS, preferred_element_type=F32)
    tkc = vt_ref.shape[-1]
    for j in range(vt_ref.shape[0]):
        vt_ref[j] = vt[:, j * tkc:(j + 1) * tkc].astype(vt_ref.dtype)
    c += dv_w
    emit(rq_ref, c, rq_w, pair_rot); c += rq_w
    emit(rk_ref, c, rk_w, lambda t: pair_rot(t) * (RET_KEY_DIM ** -0.5)); c += rk_w
    emit(rv_ref, c, rv_w, lambda t: t); c += rv_w
    emit(rg_ref, c, rg_w, lambda t: t)
    _retention_tile(rq_ref, rk_ref, rv_ref, rg_ref, r_ref, state_ref,
                    dmat_ref, qdec_ref, kdec_ref, cdec_ref, gr_ref, chunk)


def _inproj(x2d, g, w, tabs, ret_consts, gr, seq, widths, tm, tk, chunk):
    T, D = x2d.shape
    P = w.shape[1]
    nps = seq // tm
    dq_w, dk_w, dv_w, rq_w, rk_w, rv_w, rg_w = widths
    row = lambda t: (t, 0)
    const = lambda t: (0, 0)
    full = lambda a: pl.BlockSpec(a.shape, lambda t: (0,) * a.ndim)
    out_shape = [jax.ShapeDtypeStruct((T, dq_w), BF16),
                 jax.ShapeDtypeStruct((T, dk_w), BF16),
                 jax.ShapeDtypeStruct((T // seq, seq // tk, dv_w, tk), BF16),
                 jax.ShapeDtypeStruct((T, rv_w), BF16)]
    out_specs = [pl.BlockSpec((tm, dq_w), row),
                 pl.BlockSpec((tm, dk_w), row),
                 pl.BlockSpec((None, tm // tk, dv_w, tk), lambda t: (t // nps, t % nps, 0, 0)),
                 pl.BlockSpec((tm, rv_w), row)]
    return pl.pallas_call(
        functools.partial(_inproj_body, widths=widths, tiles_per_seq=nps, chunk=chunk),
        grid=(T // tm,),
        in_specs=[pl.BlockSpec((tm, D), row),
                  pl.BlockSpec((1, D), const),
                  pl.BlockSpec((D, P), const, pipeline_mode=pl.Buffered(1))]
                 + [full(a) for a in tabs] + [full(a) for a in ret_consts] + [full(gr)],
        out_specs=out_specs,
        out_shape=out_shape,
        scratch_shapes=[pltpu.VMEM((tm, rq_w), BF16), pltpu.VMEM((tm, rk_w), BF16),
                        pltpu.VMEM((tm, rv_w), BF16), pltpu.VMEM((tm, rg_w), F32),
                        pltpu.VMEM((rq_w // LANES, LANES, LANES), F32),
                        pltpu.VMEM((dv_w, D), BF16)],
        compiler_params=pltpu.CompilerParams(
            dimension_semantics=("arbitrary",), vmem_limit_bytes=48 * 1024 * 1024),
        name="inproj",
    )(x2d, g, w, *tabs, *ret_consts, gr)


def _attn_body(lq1_ref, lk1_ref, lq2_ref, lk2_ref, g_ref, q_ref, k_ref, vt_ref, o_ref,
               q2_ref, m_ref, acc_ref,
               sa_ref, sb_ref, mxa_ref, mxb_ref, *, tq, gw, lam_init):
    i = pl.program_id(2)
    tk = tq // 2
    mq = 2 * tq
    groups = [slice(g * gw, (g + 1) * gw) for g in range(mq // gw)]
    all_groups = tuple(range(len(groups)))
    q = q_ref[...]
    lane = lax.broadcasted_iota(jnp.int32, q.shape, 1)
    zero = jnp.zeros_like(q)
    q2_ref[0:tq, :] = jnp.where(lane < DIFF_HEAD_DIM, q, zero)
    q2_ref[tq:mq, :] = jnp.where(lane >= DIFF_HEAD_DIM, q, zero)
    m_ref[...] = jnp.full(m_ref.shape, -jnp.inf, F32)
    acc_ref[...] = jnp.zeros(acc_ref.shape, F32)
    ones = jnp.ones((BF16_ROWS, tk), BF16)
    key_off = lax.broadcasted_iota(jnp.int32, (tk, gw), 0)
    qry_off = lax.broadcasted_iota(jnp.int32, (tk, gw), 1)

    def visibility(r, g):
        q0 = (g * gw) % tq
        k0 = r * tk
        if k0 + tk - 1 <= q0:
            return "all", None
        if k0 > q0 + gw - 1:
            return "none", None
        return "some", k0 - q0

    def produce(j, g, s_ref, mx_ref):
        kc = k_ref[pl.ds(pl.multiple_of(j * tk, tk), tk), :]
        s = lax.dot_general(kc, q2_ref[groups[g], :], NT_DIMS, preferred_element_type=F32)
        s_ref[:, groups[g]] = s
        mx_ref[0:1, groups[g]] = jnp.max(s, axis=0, keepdims=True)

    def consume(va, g, s_ref, mx_ref, shift):
        cs = groups[g]
        m_old = m_ref[0:1, cs]
        if shift is None:
            s = s_ref[:, cs]
            m_new = jnp.maximum(m_old, mx_ref[0:1, cs])
        else:
            s = jnp.where(key_off + shift <= qry_off, s_ref[:, cs], -jnp.inf)
            m_new = jnp.maximum(m_old, jnp.max(s, axis=0, keepdims=True))
        alpha = jnp.exp2(m_old - m_new)
        p = jnp.exp2(s - m_new).astype(BF16)
        m_ref[0:1, cs] = m_new
        acc_ref[:, cs] = alpha * acc_ref[:, cs] + _dot(va, p)

    def step(j, cur, nxt, diag=None):
        va = jnp.concatenate([vt_ref[j], ones], axis=0)
        for g in all_groups:
            if diag is None or (diag == 0 and visibility(1, g)[0] != "none"):
                produce(j + 1, g, *nxt)
            kind, shift = ("all", None) if diag is None else visibility(diag, g)
            if kind != "none":
                consume(va, g, *cur, shift=shift)

    buf_a = (sa_ref, mxa_ref)
    buf_b = (sb_ref, mxb_ref)
    for g in all_groups:
        produce(0, g, *buf_a)

    def pair_body(t, carry):
        step(2 * t, buf_a, buf_b)
        step(2 * t + 1, buf_b, buf_a)
        return carry

    lax.fori_loop(0, i, pair_body, 0)
    step(2 * i, buf_a, buf_b, diag=0)
    step(2 * i + 1, buf_b, buf_a, diag=1)

    lam = (jnp.exp(jnp.sum(lq1_ref[...] * lk1_ref[...], axis=-1, keepdims=True))
           - jnp.exp(jnp.sum(lq2_ref[...] * lk2_ref[...], axis=-1, keepdims=True))
           + lam_init)
    num = acc_ref[0:DIFF_V_DIM, :]
    den = acc_ref[DIFF_V_DIM:DIFF_V_DIM + 1, :]
    o_t = num[:, :tq] / den[:, :tq] - lam * (num[:, tq:] / den[:, tq:])
    o = o_t.T
    o_ref[...] = (_rms(o, g_ref[...], HEAD_NORM_EPS) * (1.0 - lam_init)).astype(o_ref.dtype)


def _diffattn(lam_vecs, g, q, k, vt, tq, gw, lam_init):
    B, S, W = q.shape
    tk = vt.shape[3]
    assert tq == 2 * tk and tq % gw == 0
    mq = 2 * tq
    small = lambda a: pl.BlockSpec(a.shape, lambda b, h, i: (0, 0))
    return pl.pallas_call(
        functools.partial(_attn_body, tq=tq, gw=gw, lam_init=lam_init),
        grid=(B, W // DIFF_V_DIM, S // tq),
        in_specs=[small(a) for a in lam_vecs] + [
                  small(g),
                  pl.BlockSpec((None, tq, LANES), lambda b, h, i: (b, i, h)),
                  pl.BlockSpec((None, S, LANES), lambda b, h, i: (b, 0, h)),
                  pl.BlockSpec((None, S // tk, DIFF_V_DIM, tk), lambda b, h, i: (b, 0, h, 0))],
        out_specs=pl.BlockSpec((None, tq, LANES), lambda b, h, i: (b, i, h)),
        out_shape=jax.ShapeDtypeStruct((B, S, W), BF16),
        scratch_shapes=[pltpu.VMEM((mq, LANES), BF16),
                        pltpu.VMEM((8, mq), F32),
                        pltpu.VMEM((DIFF_V_DIM + BF16_ROWS, mq), F32),
                        pltpu.VMEM((tk, mq), F32), pltpu.VMEM((tk, mq), F32),
                        pltpu.VMEM((8, mq), F32), pltpu.VMEM((8, mq), F32)],
        compiler_params=pltpu.CompilerParams(
            dimension_semantics=("arbitrary", "arbitrary", "arbitrary"),
            vmem_limit_bytes=48 * 1024 * 1024),
        name="diffattn",
    )(*lam_vecs, g, q, k, vt)


def _retention_consts(n_heads, chunk):
    log_g = np.log1p(-(2.0 ** (-5.0 - np.arange(n_heads, dtype=np.float64))))
    n = np.arange(chunk, dtype=np.float64)
    rel = n[:, None] - n[None, :]
    dmat = np.where(rel >= 0, np.exp(log_g[:, None, None] * np.maximum(rel, 0.0)), 0.0)
    qdec = np.exp(log_g[:, None] * (n + 1.0)[None, :])
    kdec = np.exp(log_g[:, None] * (chunk - 1.0 - n)[None, :])
    cdec = np.exp(log_g * chunk)
    bl = lambda a: np.ascontiguousarray(np.broadcast_to(a[:, :, None], a.shape + (LANES,)))
    cdec_rows = np.repeat(cdec, RET_KEY_DIM).reshape(n_heads // 2, LANES)
    return tuple(jnp.asarray(a, F32) for a in (dmat, bl(qdec), bl(kdec), bl(cdec_rows)))


def _outffn_body(x_ref, a_ref, r_ref, wo_ref, g2_ref, wg_ref, wu_ref, wd_ref, gf_ref, o_ref):
    mix = jnp.concatenate([a_ref[...], r_ref[...]], axis=1)
    x1 = x_ref[...] + _dot(mix, wo_ref[...])
    h = _rms(x1, g2_ref[...], NORM_EPS).astype(BF16)
    gate = _dot(h, wg_ref[...])
    up = _dot(h, wu_ref[...])
    act = (gate * jax.nn.sigmoid(gate) * up).astype(BF16)
    x2 = x1 + _dot(act, wd_ref[...])
    o_ref[...] = _rms(x2, gf_ref[...], NORM_EPS)


def _outffn(x2d, a, r, wo, g2, wg, wu, wd, gf, tm):
    T, D = x2d.shape
    row = lambda t: (t, 0)
    const = lambda t: (0, 0)
    resident = lambda arr: pl.BlockSpec(arr.shape, const, pipeline_mode=pl.Buffered(1))
    return pl.pallas_call(
        _outffn_body,
        grid=(T // tm,),
        in_specs=[pl.BlockSpec((tm, D), row),
                  pl.BlockSpec((tm, a.shape[1]), row),
                  pl.BlockSpec((tm, r.shape[1]), row),
                  resident(wo), pl.BlockSpec(g2.shape, const),
                  resident(wg), resident(wu), resident(wd),
                  pl.BlockSpec(gf.shape, const)],
        out_specs=pl.BlockSpec((tm, D), row),
        out_shape=jax.ShapeDtypeStruct((T, D), F32),
        compiler_params=pltpu.CompilerParams(
            dimension_semantics=("arbitrary",), vmem_limit_bytes=56 * 1024 * 1024),
        name="outffn",
    )(x2d, a, r, wo, g2, wg, wu, wd, gf)


def _rotation_tables(seq, tm):
    d = DIFF_HEAD_DIM
    inv_h = ROPE_THETA ** (-np.arange(0, d, 2, dtype=np.float64) / d)
    lanes_h = np.tile(np.concatenate([inv_h, inv_h]), LANES // d)
    dr = RET_KEY_DIM
    inv_r = 1.0 / (RET_THETA ** np.linspace(0.0, 1.0, dr // 2, dtype=np.float64))
    lanes_r = np.tile(np.repeat(inv_r, 2), LANES // dr)

    def cos_sin(pos):
        out = []
        for freq in (lanes_h, lanes_r):
            ang = pos[:, None] * freq[None, :]
            out += [np.cos(ang), np.sin(ang)]
        return jnp.asarray(np.stack(out), F32)

    starts = np.arange(seq // tm, dtype=np.float64) * tm
    return cos_sin(starts), cos_sin(np.arange(tm, dtype=np.float64))


def kernel(x, attn_norm_g, w_in, lambda_q1, lambda_k1, lambda_q2, lambda_k2, diff_subln_g,
           ret_norm_g, w_out, ffn_norm_g, w_gate, w_up, w_down, final_norm_g):
    B, S, D = x.shape
    assert w_in.shape[0] == 1, "single-layer block only"
    l = 0
    diff_w = D // 2
    ret_w = D - diff_w
    ret_heads = ret_w // RET_V_DIM
    widths = (diff_w, diff_w, diff_w, ret_heads * RET_KEY_DIM, ret_heads * RET_KEY_DIM, ret_w, ret_w)
    tm_in = 512
    tabs = _rotation_tables(S, tm_in)
    ret_chunk = 128
    ret_consts = _retention_consts(ret_heads, ret_chunk)
    xs = x.reshape(B * S, D)
    lam_init = 0.8 - 0.6 * math.exp(-0.3 * l)
    q, k, vt, r = _inproj(
        xs, attn_norm_g[l][None, :], w_in[l].astype(BF16), tabs, ret_consts, ret_norm_g[l][None, :],
        S, widths, tm=tm_in, tk=512, chunk=ret_chunk)
    tok3 = lambda t: t.reshape(B, S, t.shape[-1])
    lam_vecs = [v[l:l + 1].astype(F32) for v in (lambda_q1, lambda_k1, lambda_q2, lambda_k2)]
    a = _diffattn(lam_vecs, diff_subln_g[l][None, :], tok3(q), tok3(k), vt, tq=1024, gw=256,
                  lam_init=lam_init)
    out = _outffn(xs, a.reshape(B * S, diff_w), r, w_out[l].astype(BF16),
                  ffn_norm_g[l][None, :], w_gate[l].astype(BF16), w_up[l].astype(BF16),
                  w_down[l].astype(BF16), final_norm_g[None, :], tm=512)
    return out.reshape(B, S, D)
```

```python
import functools
import math

import jax
import jax.numpy as jnp
import numpy as np
from jax import lax
from jax.experimental import pallas as pl
from jax.experimental.pallas import tpu as pltpu

F32 = jnp.float32
BF16 = jnp.bfloat16

DIFF_HEAD_DIM = 64
DIFF_V_DIM = 128
RET_KEY_DIM = 64
RET_V_DIM = 128
ROPE_THETA = 10000.0
RET_THETA = 10000.0
NORM_EPS = 1e-6
HEAD_NORM_EPS = 1e-5
LANES = 128
BF16_ROWS = 16
LOG2E = math.log2(math.e)

NT_DIMS = (((1,), (1,)), ((), ()))
TN_DIMS = (((0,), (0,)), ((), ()))


def _dot(a, b):
    return jnp.dot(a, b, preferred_element_type=F32)


def _rms(x, g, eps):
    ms = jnp.mean(x * x, axis=-1, keepdims=True)
    return x * lax.rsqrt(ms + eps) * g


def _retention_tile(rq_ref, rk_ref, rv_ref, rg_ref, o_ref, state_ref,
                    dmat_ref, qdec_ref, kdec_ref, cdec_ref, g_ref, chunk):
    n_chunks = rq_ref.shape[0] // chunk
    n_pairs = rq_ref.shape[1] // LANES
    lane = lax.broadcasted_iota(jnp.int32, (chunk, LANES), 1)
    row = lax.broadcasted_iota(jnp.int32, (LANES, LANES), 0)
    rows = [slice(c * chunk, (c + 1) * chunk) for c in range(n_chunks)]
    head_lanes = [lane < RET_KEY_DIM, lane >= RET_KEY_DIM]

    def q_masked(c, p, hh):
        qp = rq_ref[rows[c], p * LANES:(p + 1) * LANES]
        return jnp.where(head_lanes[hh], qp, jnp.zeros_like(qp))

    scores, updates = {}, {}
    for c in range(n_chunks):
        for p in range(n_pairs):
            kp = rk_ref[rows[c], p * LANES:(p + 1) * LANES]
            for hh in range(2):
                h = 2 * p + hh
                vh = rv_ref[rows[c], h * RET_V_DIM:(h + 1) * RET_V_DIM]
                s = lax.dot_general(q_masked(c, p, hh), kp, NT_DIMS, preferred_element_type=F32)
                scores[c, h] = (s * dmat_ref[h]).astype(BF16)
                vd = (vh.astype(F32) * kdec_ref[h]).astype(BF16)
                updates[c, h] = lax.dot_general(kp, vd, TN_DIMS, preferred_element_type=F32)

    states = {}
    for p in range(n_pairs):
        st = state_ref[p]
        for c in range(n_chunks):
            states[c, p] = st.astype(BF16)
            st = cdec_ref[p] * st + jnp.where(row < RET_KEY_DIM, updates[c, 2 * p], updates[c, 2 * p + 1])
        state_ref[p] = st

    for c in range(n_chunks):
        for p in range(n_pairs):
            for hh in range(2):
                h = 2 * p + hh
                hs = slice(h * RET_V_DIM, (h + 1) * RET_V_DIM)
                cross = _dot(q_masked(c, p, hh), states[c, p]) * qdec_ref[h]
                o = _dot(scores[c, h], rv_ref[rows[c], hs]) + cross
                mu = jnp.mean(o, axis=-1, keepdims=True)
                cen = o - mu
                var = jnp.mean(cen * cen, axis=-1, keepdims=True)
                y = cen * lax.rsqrt(var + HEAD_NORM_EPS) * g_ref[:, hs]
                gate = rg_ref[rows[c], hs]
                o_ref[rows[c], hs] = (gate * jax.nn.sigmoid(gate) * y).astype(o_ref.dtype)


def _inproj_body(x_ref, g_ref, w_ref, base_ref, off_ref,
                 dmat_ref, qdec_ref, kdec_ref, cdec_ref, gr_ref,
                 q_ref, k_ref, vt_ref, r_ref,
                 rq_ref, rk_ref, rv_ref, rg_ref, state_ref, wvt_ref,
                 *, widths, tiles_per_seq, chunk):
    dq_w, dk_w, dv_w, rq_w, rk_w, rv_w, rg_w = widths
    tile_in_seq = pl.program_id(0) % tiles_per_seq

    @pl.when(pl.program_id(0) == 0)
    def _():
        wv = w_ref[:, dq_w + dk_w:dq_w + dk_w + dv_w]
        wvt_ref[...] = wv.astype(F32).T.astype(wvt_ref.dtype)

    @pl.when(tile_in_seq == 0)
    def _():
        state_ref[...] = jnp.zeros(state_ref.shape, F32)

    h = _rms(x_ref[...], g_ref[...], NORM_EPS).astype(BF16)
    lane = lax.broadcasted_iota(jnp.int32, off_ref.shape[1:], 1)
    first_half = (lane & (DIFF_HEAD_DIM // 2)) == 0
    even = (lane & 1) == 0

    def angle_tables(kind, negate_where):
        cb = base_ref[2 * kind, pl.ds(tile_in_seq, 1), :]
        sb = base_ref[2 * kind + 1, pl.ds(tile_in_seq, 1), :]
        co, so = off_ref[2 * kind], off_ref[2 * kind + 1]
        sin = sb * co + cb * so
        return cb * co - sb * so, jnp.where(negate_where, -sin, sin)

    ch, sh = angle_tables(0, first_half)
    ci, si = angle_tables(1, even)

    def rope(xc):
        sw = jnp.where(first_half, pltpu.roll(xc, LANES - 32, 1), pltpu.roll(xc, 32, 1))
        return xc * ch + sw * sh

    def pair_rot(xc):
        sw = jnp.where(even, pltpu.roll(xc, LANES - 1, 1), pltpu.roll(xc, 1, 1))
        return xc * ci + sw * si

    def emit(out_ref, c0, width, fn):
        p = _dot(h, w_ref[:, c0:c0 + width])
        for j in range(width // LANES):
            sl = slice(j * LANES, (j + 1) * LANES)
            out_ref[:, sl] = fn(p[:, sl]).astype(out_ref.dtype)

    c = 0
    emit(q_ref, c, dq_w, lambda t: rope(t) * (DIFF_HEAD_DIM ** -0.5 * LOG2E)); c += dq_w
    emit(k_ref, c, dk_w, rope); c += dk_w
    vt = lax.dot_general(wvt_ref[...], h, NT_DIMS, preferred_element_type=F32)
    tkc = vt_ref.shape[-1]
    for j in range(vt_ref.shape[0]):
        vt_ref[j] = vt[:, j * tkc:(j + 1) * tkc].astype(vt_ref.dtype)
    c += dv_w
    emit(rq_ref, c, rq_w, pair_rot); c += rq_w
    emit(rk_ref, c, rk_w, lambda t: pair_rot(t) * (RET_KEY_DIM ** -0.5)); c += rk_w
    emit(rv_ref, c, rv_w, lambda t: t); c += rv_w
    emit(rg_ref, c, rg_w, lambda t: t)
    _retention_tile(rq_ref, rk_ref, rv_ref, rg_ref, r_ref, state_ref,
                    dmat_ref, qdec_ref, kdec_ref, cdec_ref, gr_ref, chunk)


def _inproj(x2d, g, w, tabs, ret_consts, gr, seq, widths, tm, tk, chunk):
    T, D = x2d.shape
    P = w.shape[1]
    nps = seq // tm
    dq_w, dk_w, dv_w, rq_w, rk_w, rv_w, rg_w = widths
    row = lambda t: (t, 0)
    const = lambda t: (0, 0)
    full = lambda a: pl.BlockSpec(a.shape, lambda t: (0,) * a.ndim)
    out_shape = [jax.ShapeDtypeStruct((T, dq_w), BF16),
                 jax.ShapeDtypeStruct((T, dk_w), BF16),
                 jax.ShapeDtypeStruct((T // seq, seq // tk, dv_w, tk), BF16),
                 jax.ShapeDtypeStruct((T, rv_w), BF16)]
    out_specs = [pl.BlockSpec((tm, dq_w), row),
                 pl.BlockSpec((tm, dk_w), row),
                 pl.BlockSpec((None, tm // tk, dv_w, tk), lambda t: (t // nps, t % nps, 0, 0)),
                 pl.BlockSpec((tm, rv_w), row)]
    return pl.pallas_call(
        functools.partial(_inproj_body, widths=widths, tiles_per_seq=nps, chunk=chunk),
        grid=(T // tm,),
        in_specs=[pl.BlockSpec((tm, D), row),
                  pl.BlockSpec((1, D), const),
                  pl.BlockSpec((D, P), const, pipeline_mode=pl.Buffered(1))]
                 + [full(a) for a in tabs] + [full(a) for a in ret_consts] + [full(gr)],
        out_specs=out_specs,
        out_shape=out_shape,
        scratch_shapes=[pltpu.VMEM((tm, rq_w), BF16), pltpu.VMEM((tm, rk_w), BF16),
                        pltpu.VMEM((tm, rv_w), BF16), pltpu.VMEM((tm, rg_w), F32),
                        pltpu.VMEM((rq_w // LANES, LANES, LANES), F32),
                        pltpu.VMEM((dv_w, D), BF16)],
        compiler_params=pltpu.CompilerParams(
            dimension_semantics=("arbitrary",), vmem_limit_bytes=48 * 1024 * 1024),
        name="inproj",
    )(x2d, g, w, *tabs, *ret_consts, gr)


def _attn_body(lq1_ref, lk1_ref, lq2_ref, lk2_ref, g_ref, q_ref, k_ref, vt_ref, o_ref,
               q2_ref, m_ref, acc_ref,
               sa_ref, sb_ref, mxa_ref, mxb_ref, *, tq, gw, lam_init):
    i = pl.program_id(2)
    tk = tq // 2
    mq = 2 * tq
    groups = [slice(g * gw, (g + 1) * gw) for g in range(mq // gw)]
    all_groups = tuple(range(len(groups)))
    q = q_ref[...]
    lane = lax.broadcasted_iota(jnp.int32, q.shape, 1)
    zero = jnp.zeros_like(q)
    q2_ref[0:tq, :] = jnp.where(lane < DIFF_HEAD_DIM, q, zero)
    q2_ref[tq:mq, :] = jnp.where(lane >= DIFF_HEAD_DIM, q, zero)
    m_ref[...] = jnp.full(m_ref.shape, -jnp.inf, F32)
    acc_ref[...] = jnp.zeros(acc_ref.shape, F32)
    ones = jnp.ones((BF16_ROWS, tk), BF16)
    key_off = lax.broadcasted_iota(jnp.int32, (tk, gw), 0)
    qry_off = lax.broadcasted_iota(jnp.int32, (tk, gw), 1)

    def visibility(r, g):
        q0 = (g * gw) % tq
        k0 = r * tk
        if k0 + tk - 1 <= q0:
            return "all", None
        if k0 > q0 + gw - 1:
            return "none", None
        return "some", k0 - q0

    def produce(j, g, s_ref, mx_ref):
        kc = k_ref[pl.ds(pl.multiple_of(j * tk, tk), tk), :]
        s = lax.dot_general(kc, q2_ref[groups[g], :], NT_DIMS, preferred_element_type=F32)
        s_ref[:, groups[g]] = s
        mx_ref[0:1, groups[g]] = jnp.max(s, axis=0, keepdims=True)

    def consume(va, g, s_ref, mx_ref, shift):
        cs = groups[g]
        m_old = m_ref[0:1, cs]
        if shift is None:
            s = s_ref[:, cs]
            m_new = jnp.maximum(m_old, mx_ref[0:1, cs])
        else:
            s = jnp.where(key_off + shift <= qry_off, s_ref[:, cs], -jnp.inf)
            m_new = jnp.maximum(m_old, jnp.max(s, axis=0, keepdims=True))
        alpha = jnp.exp2(m_old - m_new)
        p = jnp.exp2(s - m_new).astype(BF16)
        m_ref[0:1, cs] = m_new
        acc_ref[:, cs] = alpha * acc_ref[:, cs] + _dot(va, p)

    def step(j, cur, nxt, diag=None):
        va = jnp.concatenate([vt_ref[j], ones], axis=0)
        for g in all_groups:
            if diag is None or (diag == 0 and visibility(1, g)[0] != "none"):
                produce(j + 1, g, *nxt)
            kind, shift = ("all", None) if diag is None else visibility(diag, g)
            if kind != "none":
                consume(va, g, *cur, shift=shift)

    buf_a = (sa_ref, mxa_ref)
    buf_b = (sb_ref, mxb_ref)
    for g in all_groups:
        produce(0, g, *buf_a)

    def pair(t):
        step(2 * t, buf_a, buf_b)
        step(2 * t + 1, buf_b, buf_a)

    def two_pairs_body(u, carry):
        pair(2 * u)
        pair(2 * u + 1)
        return carry

    lax.fori_loop(0, i // 2, two_pairs_body, 0)

    @pl.when(i % 2 == 1)
    def _():
        pair(i - 1)

    step(2 * i, buf_a, buf_b, diag=0)
    step(2 * i + 1, buf_b, buf_a, diag=1)

    lam = (jnp.exp(jnp.sum(lq1_ref[...] * lk1_ref[...], axis=-1, keepdims=True))
           - jnp.exp(jnp.sum(lq2_ref[...] * lk2_ref[...], axis=-1, keepdims=True))
           + lam_init)
    num = acc_ref[0:DIFF_V_DIM, :]
    den = acc_ref[DIFF_V_DIM:DIFF_V_DIM + 1, :]
    o_t = num[:, :tq] / den[:, :tq] - lam * (num[:, tq:] / den[:, tq:])
    o = o_t.T
    o_ref[...] = (_rms(o, g_ref[...], HEAD_NORM_EPS) * (1.0 - lam_init)).astype(o_ref.dtype)


def _diffattn(lam_vecs, g, q, k, vt, tq, gw, lam_init):
    B, S, W = q.shape
    tk = vt.shape[3]
    assert tq == 2 * tk and tq % gw == 0
    mq = 2 * tq
    small = lambda a: pl.BlockSpec(a.shape, lambda b, h, i: (0, 0))
    return pl.pallas_call(
        functools.partial(_attn_body, tq=tq, gw=gw, lam_init=lam_init),
        grid=(B, W // DIFF_V_DIM, S // tq),
        in_specs=[small(a) for a in lam_vecs] + [
                  small(g),
                  pl.BlockSpec((None, tq, LANES), lambda b, h, i: (b, i, h)),
                  pl.BlockSpec((None, S, LANES), lambda b, h, i: (b, 0, h)),
                  pl.BlockSpec((None, S // tk, DIFF_V_DIM, tk), lambda b, h, i: (b, 0, h, 0))],
        out_specs=pl.BlockSpec((None, tq, LANES), lambda b, h, i: (b, i, h)),
        out_shape=jax.ShapeDtypeStruct((B, S, W), BF16),
        scratch_shapes=[pltpu.VMEM((mq, LANES), BF16),
                        pltpu.VMEM((8, mq), F32),
                        pltpu.VMEM((DIFF_V_DIM + BF16_ROWS, mq), F32),
                        pltpu.VMEM((tk, mq), F32), pltpu.VMEM((tk, mq), F32),
                        pltpu.VMEM((8, mq), F32), pltpu.VMEM((8, mq), F32)],
        compiler_params=pltpu.CompilerParams(
            dimension_semantics=("arbitrary", "arbitrary", "arbitrary"),
            vmem_limit_bytes=48 * 1024 * 1024),
        name="diffattn",
    )(*lam_vecs, g, q, k, vt)


def _retention_consts(n_heads, chunk):
    log_g = np.log1p(-(2.0 ** (-5.0 - np.arange(n_heads, dtype=np.float64))))
    n = np.arange(chunk, dtype=np.float64)
    rel = n[:, None] - n[None, :]
    dmat = np.where(rel >= 0, np.exp(log_g[:, None, None] * np.maximum(rel, 0.0)), 0.0)
    qdec = np.exp(log_g[:, None] * (n + 1.0)[None, :])
    kdec = np.exp(log_g[:, None] * (chunk - 1.0 - n)[None, :])
    cdec = np.exp(log_g * chunk)
    bl = lambda a: np.ascontiguousarray(np.broadcast_to(a[:, :, None], a.shape + (LANES,)))
    cdec_rows = np.repeat(cdec, RET_KEY_DIM).reshape(n_heads // 2, LANES)
    return tuple(jnp.asarray(a, F32) for a in (dmat, bl(qdec), bl(kdec), bl(cdec_rows)))


def _outffn_body(x_ref, a_ref, r_ref, wo_ref, g2_ref, wg_ref, wu_ref, wd_ref, gf_ref, o_ref):
    mix = jnp.concatenate([a_ref[...], r_ref[...]], axis=1)
    x1 = x_ref[...] + _dot(mix, wo_ref[...])
    h = _rms(x1, g2_ref[...], NORM_EPS).astype(BF16)
    gate = _dot(h, wg_ref[...])
    up = _dot(h, wu_ref[...])
    act = (gate * jax.nn.sigmoid(gate) * up).astype(BF16)
    x2 = x1 + _dot(act, wd_ref[...])
    o_ref[...] = _rms(x2, gf_ref[...], NORM_EPS)


def _outffn(x2d, a, r, wo, g2, wg, wu, wd, gf, tm):
    T, D = x2d.shape
    row = lambda t: (t, 0)
    const = lambda t: (0, 0)
    resident = lambda arr: pl.BlockSpec(arr.shape, const, pipeline_mode=pl.Buffered(1))
    return pl.pallas_call(
        _outffn_body,
        grid=(T // tm,),
        in_specs=[pl.BlockSpec((tm, D), row),
                  pl.BlockSpec((tm, a.shape[1]), row),
                  pl.BlockSpec((tm, r.shape[1]), row),
                  resident(wo), pl.BlockSpec(g2.shape, const),
                  resident(wg), resident(wu), resident(wd),
                  pl.BlockSpec(gf.shape, const)],
        out_specs=pl.BlockSpec((tm, D), row),
        out_shape=jax.ShapeDtypeStruct((T, D), F32),
        compiler_params=pltpu.CompilerParams(
            dimension_semantics=("arbitrary",), vmem_limit_bytes=56 * 1024 * 1024),
        name="outffn",
    )(x2d, a, r, wo, g2, wg, wu, wd, gf)


def _rotation_tables(seq, tm):
    d = DIFF_HEAD_DIM
    inv_h = ROPE_THETA ** (-np.arange(0, d, 2, dtype=np.float64) / d)
    lanes_h = np.tile(np.concatenate([inv_h, inv_h]), LANES // d)
    dr = RET_KEY_DIM
    inv_r = 1.0 / (RET_THETA ** np.linspace(0.0, 1.0, dr // 2, dtype=np.float64))
    lanes_r = np.tile(np.repeat(inv_r, 2), LANES // dr)

    def cos_sin(pos):
        out = []
        for freq in (lanes_h, lanes_r):
            ang = pos[:, None] * freq[None, :]
            out += [np.cos(ang), np.sin(ang)]
        return jnp.asarray(np.stack(out), F32)

    starts = np.arange(seq // tm, dtype=np.float64) * tm
    return cos_sin(starts), cos_sin(np.arange(tm, dtype=np.float64))


def kernel(x, attn_norm_g, w_in, lambda_q1, lambda_k1, lambda_q2, lambda_k2, diff_subln_g,
           ret_norm_g, w_out, ffn_norm_g, w_gate, w_up, w_down, final_norm_g):
    B, S, D = x.shape
    assert w_in.shape[0] == 1, "single-layer block only"
    l = 0
    diff_w = D // 2
    ret_w = D - diff_w
    ret_heads = ret_w // RET_V_DIM
    widths = (diff_w, diff_w, diff_w, ret_heads * RET_KEY_DIM, ret_heads * RET_KEY_DIM, ret_w, ret_w)
    tm_in = 512
    tabs = _rotation_tables(S, tm_in)
    ret_chunk = 128
    ret_consts = _retention_consts(ret_heads, ret_chunk)
    xs = x.reshape(B * S, D)
    lam_init = 0.8 - 0.6 * math.exp(-0.3 * l)
    q, k, vt, r = _inproj(
        xs, attn_norm_g[l][None, :], w_in[l].astype(BF16), tabs, ret_consts, ret_norm_g[l][None, :],
        S, widths, tm=tm_in, tk=512, chunk=ret_chunk)
    tok3 = lambda t: t.reshape(B, S, t.shape[-1])
    lam_vecs = [v[l:l + 1].astype(F32) for v in (lambda_q1, lambda_k1, lambda_q2, lambda_k2)]
    a = _diffattn(lam_vecs, diff_subln_g[l][None, :], tok3(q), tok3(k), vt, tq=1024, gw=256,
                  lam_init=lam_init)
    out = _outffn(xs, a.reshape(B * S, diff_w), r, w_out[l].astype(BF16),
                  ffn_norm_g[l][None, :], w_gate[l].astype(BF16), w_up[l].astype(BF16),
                  w_down[l].astype(BF16), final_norm_g[None, :], tm=512)
    return out.reshape(B, S, D)
```

```python
import functools
import math

import jax
import jax.numpy as jnp
import numpy as np
from jax import lax
from jax.experimental import pallas as pl
from jax.experimental.pallas import tpu as pltpu

F32 = jnp.float32
BF16 = jnp.bfloat16

DIFF_HEAD_DIM = 64
DIFF_V_DIM = 128
RET_KEY_DIM = 64
RET_V_DIM = 128
ROPE_THETA = 10000.0
RET_THETA = 10000.0
NORM_EPS = 1e-6
HEAD_NORM_EPS = 1e-5
LANES = 128
BF16_ROWS = 16
LOG2E = math.log2(math.e)

NT_DIMS = (((1,), (1,)), ((), ()))
TN_DIMS = (((0,), (0,)), ((), ()))


def _dot(a, b):
    return jnp.dot(a, b, preferred_element_type=F32)


def _rms(x, g, eps):
    ms = jnp.mean(x * x, axis=-1, keepdims=True)
    return x * lax.rsqrt(ms + eps) * g


def _retention_tile(rq_ref, rk_ref, rv_ref, rg_ref, o_ref, state_ref,
                    dmat_ref, qdec_ref, kdec_ref, cdec_ref, g_ref, chunk):
    n_chunks = rq_ref.shape[0] // chunk
    n_pairs = rq_ref.shape[1] // LANES
    lane = lax.broadcasted_iota(jnp.int32, (chunk, LANES), 1)
    row = lax.broadcasted_iota(jnp.int32, (LANES, LANES), 0)
    rows = [slice(c * chunk, (c + 1) * chunk) for c in range(n_chunks)]
    head_lanes = [lane < RET_KEY_DIM, lane >= RET_KEY_DIM]

    def q_masked(c, p, hh):
        qp = rq_ref[rows[c], p * LANES:(p + 1) * LANES]
        return jnp.where(head_lanes[hh], qp, jnp.zeros_like(qp))

    scores, updates = {}, {}
    for c in range(n_chunks):
        for p in range(n_pairs):
            kp = rk_ref[rows[c], p * LANES:(p + 1) * LANES]
            for hh in range(2):
                h = 2 * p + hh
                vh = rv_ref[rows[c], h * RET_V_DIM:(h + 1) * RET_V_DIM]
                s = lax.dot_general(q_masked(c, p, hh), kp, NT_DIMS, preferred_element_type=F32)
                scores[c, h] = (s * dmat_ref[h]).astype(BF16)
                vd = (vh.astype(F32) * kdec_ref[h]).astype(BF16)
                updates[c, h] = lax.dot_general(kp, vd, TN_DIMS, preferred_element_type=F32)

    states = {}
    for p in range(n_pairs):
        st = state_ref[p]
        for c in range(n_chunks):
            states[c, p] = st.astype(BF16)
            st = cdec_ref[p] * st + jnp.where(row < RET_KEY_DIM, updates[c, 2 * p], updates[c, 2 * p + 1])
        state_ref[p] = st

    for c in range(n_chunks):
        for p in range(n_pairs):
            for hh in range(2):
                h = 2 * p + hh
                hs = slice(h * RET_V_DIM, (h + 1) * RET_V_DIM)
                cross = _dot(q_masked(c, p, hh), states[c, p]) * qdec_ref[h]
                o = _dot(scores[c, h], rv_ref[rows[c], hs]) + cross
                mu = jnp.mean(o, axis=-1, keepdims=True)
                cen = o - mu
                var = jnp.mean(cen * cen, axis=-1, keepdims=True)
                y = cen * lax.rsqrt(var + HEAD_NORM_EPS) * g_ref[:, hs]
                gate = rg_ref[rows[c], hs]
                o_ref[rows[c], hs] = (gate * jax.nn.sigmoid(gate) * y).astype(o_ref.dtype)


def _inproj_body(x_ref, g_ref, w_ref, base_ref, off_ref,
                 dmat_ref, qdec_ref, kdec_ref, cdec_ref, gr_ref,
                 q_ref, k_ref, vt_ref, r_ref,
                 rq_ref, rk_ref, rv_ref, rg_ref, state_ref, wvt_ref,
                 *, widths, tiles_per_seq, chunk):
    dq_w, dk_w, dv_w, rq_w, rk_w, rv_w, rg_w = widths
    tile_in_seq = pl.program_id(0) % tiles_per_seq

    @pl.when(pl.program_id(0) == 0)
    def _():
        wv = w_ref[:, dq_w + dk_w:dq_w + dk_w + dv_w]
        wvt_ref[...] = wv.astype(F32).T.astype(wvt_ref.dtype)

    @pl.when(tile_in_seq == 0)
    def _():
        state_ref[...] = jnp.zeros(state_ref.shape, F32)

    h = _rms(x_ref[...], g_ref[...], NORM_EPS).astype(BF16)
    lane = lax.broadcasted_iota(jnp.int32, off_ref.shape[1:], 1)
    first_half = (lane & (DIFF_HEAD_DIM // 2)) == 0
    even = (lane & 1) == 0

    def angle_tables(kind, negate_where):
        cb = base_ref[2 * kind, pl.ds(tile_in_seq, 1), :]
        sb = base_ref[2 * kind + 1, pl.ds(tile_in_seq, 1), :]
        co, so = off_ref[2 * kind], off_ref[2 * kind + 1]
        sin = sb * co + cb * so
        return cb * co - sb * so, jnp.where(negate_where, -sin, sin)

    ch, sh = angle_tables(0, first_half)
    ci, si = angle_tables(1, even)

    def rope(xc):
        sw = jnp.where(first_half, pltpu.roll(xc, LANES - 32, 1), pltpu.roll(xc, 32, 1))
        return xc * ch + sw * sh

    def pair_rot(xc):
        sw = jnp.where(even, pltpu.roll(xc, LANES - 1, 1), pltpu.roll(xc, 1, 1))
        return xc * ci + sw * si

    def emit(out_ref, c0, width, fn):
        p = _dot(h, w_ref[:, c0:c0 + width])
        for j in range(width // LANES):
            sl = slice(j * LANES, (j + 1) * LANES)
            out_ref[:, sl] = fn(p[:, sl]).astype(out_ref.dtype)

    c = 0
    emit(q_ref, c, dq_w, lambda t: rope(t) * (DIFF_HEAD_DIM ** -0.5 * LOG2E)); c += dq_w
    emit(k_ref, c, dk_w, rope); c += dk_w
    vt = lax.dot_general(wvt_ref[...], h, NT_DIMS, preferred_element_type=F32)
    tkc = vt_ref.shape[-1]
    for j in range(vt_ref.shape[0]):
        vt_ref[j] = vt[:, j * tkc:(j + 1) * tkc].astype(vt_ref.dtype)
    c += dv_w
    emit(rq_ref, c, rq_w, pair_rot); c += rq_w
    emit(rk_ref, c, rk_w, lambda t: pair_rot(t) * (RET_KEY_DIM ** -0.5)); c += rk_w
    emit(rv_ref, c, rv_w, lambda t: t); c += rv_w
    emit(rg_ref, c, rg_w, lambda t: t)
    _retention_tile(rq_ref, rk_ref, rv_ref, rg_ref, r_ref, state_ref,
                    dmat_ref, qdec_ref, kdec_ref, cdec_ref, gr_ref, chunk)


def _inproj(x2d, g, w, tabs, ret_consts, gr, seq, widths, tm, tk, chunk):
    T, D = x2d.shape
    P = w.shape[1]
    nps = seq // tm
    dq_w, dk_w, dv_w, rq_w, rk_w, rv_w, rg_w = widths
    row = lambda t: (t, 0)
    const = lambda t: (0, 0)
    full = lambda a: pl.BlockSpec(a.shape, lambda t: (0,) * a.ndim)
    out_shape = [jax.ShapeDtypeStruct((T, dq_w), BF16),
                 jax.ShapeDtypeStruct((T, dk_w), BF16),
                 jax.ShapeDtypeStruct((T // seq, seq // tk, dv_w, tk), BF16),
                 jax.ShapeDtypeStruct((T, rv_w), BF16)]
    out_specs = [pl.BlockSpec((tm, dq_w), row),
                 pl.BlockSpec((tm, dk_w), row),
                 pl.BlockSpec((None, tm // tk, dv_w, tk), lambda t: (t // nps, t % nps, 0, 0)),
                 pl.BlockSpec((tm, rv_w), row)]
    return pl.pallas_call(
        functools.partial(_inproj_body, widths=widths, tiles_per_seq=nps, chunk=chunk),
        grid=(T // tm,),
        in_specs=[pl.BlockSpec((tm, D), row),
                  pl.BlockSpec((1, D), const),
                  pl.BlockSpec((D, P), const, pipeline_mode=pl.Buffered(1))]
                 + [full(a) for a in tabs] + [full(a) for a in ret_consts] + [full(gr)],
        out_specs=out_specs,
        out_shape=out_shape,
        scratch_shapes=[pltpu.VMEM((tm, rq_w), BF16), pltpu.VMEM((tm, rk_w), BF16),
                        pltpu.VMEM((tm, rv_w), BF16), pltpu.VMEM((tm, rg_w), F32),
                        pltpu.VMEM((rq_w // LANES, LANES, LANES), F32),
                        pltpu.VMEM((dv_w, D), BF16)],
        compiler_params=pltpu.CompilerParams(
            dimension_semantics=("arbitrary",), vmem_limit_bytes=48 * 1024 * 1024),
        name="inproj",
    )(x2d, g, w, *tabs, *ret_consts, gr)


def _attn_body(lq1_ref, lk1_ref, lq2_ref, lk2_ref, g_ref, q_ref, k_ref, vt_ref, o_ref,
               q2_ref, m_ref, acc_ref,
               sa_ref, sb_ref, mxa_ref, mxb_ref, *, tq, gw, lam_init):
    i = pl.program_id(2)
    last_block = pl.num_programs(2) - 1
    tk = tq // 2
    mq = 2 * tq
    groups = [slice(g * gw, (g + 1) * gw) for g in range(mq // gw)]
    all_groups = tuple(range(len(groups)))
    lane = lax.broadcasted_iota(jnp.int32, (tq, LANES), 1)

    def load_queries(block):
        q = q_ref[pl.ds(pl.multiple_of(block * tq, tq), tq), :]
        zero = jnp.zeros_like(q)
        q2_ref[0:tq, :] = jnp.where(lane < DIFF_HEAD_DIM, q, zero)
        q2_ref[tq:mq, :] = jnp.where(lane >= DIFF_HEAD_DIM, q, zero)

    m_ref[...] = jnp.full(m_ref.shape, -jnp.inf, F32)
    acc_ref[...] = jnp.zeros(acc_ref.shape, F32)
    ones = jnp.ones((BF16_ROWS, tk), BF16)
    key_off = lax.broadcasted_iota(jnp.int32, (tk, gw), 0)
    qry_off = lax.broadcasted_iota(jnp.int32, (tk, gw), 1)

    def visibility(r, g):
        q0 = (g * gw) % tq
        k0 = r * tk
        if k0 + tk - 1 <= q0:
            return "all", None
        if k0 > q0 + gw - 1:
            return "none", None
        return "some", k0 - q0

    def produce(j, g, s_ref, mx_ref):
        kc = k_ref[pl.ds(pl.multiple_of(j * tk, tk), tk), :]
        s = lax.dot_general(kc, q2_ref[groups[g], :], NT_DIMS, preferred_element_type=F32)
        s_ref[:, groups[g]] = s
        mx_ref[0:1, groups[g]] = jnp.max(s, axis=0, keepdims=True)

    def consume(va, g, s_ref, mx_ref, shift):
        cs = groups[g]
        m_old = m_ref[0:1, cs]
        if shift is None:
            s = s_ref[:, cs]
            m_new = jnp.maximum(m_old, mx_ref[0:1, cs])
        else:
            s = jnp.where(key_off + shift <= qry_off, s_ref[:, cs], -jnp.inf)
            m_new = jnp.maximum(m_old, jnp.max(s, axis=0, keepdims=True))
        alpha = jnp.exp2(m_old - m_new)
        p = jnp.exp2(s - m_new).astype(BF16)
        m_ref[0:1, cs] = m_new
        acc_ref[:, cs] = alpha * acc_ref[:, cs] + _dot(va, p)

    def step(j, cur, nxt, diag=None, next_block=False):
        va = jnp.concatenate([vt_ref[j], ones], axis=0)
        for g in all_groups:
            if diag is None or (diag == 0 and visibility(1, g)[0] != "none"):
                produce(j + 1, g, *nxt)
            elif diag == 1 and next_block:
                produce(0, g, *nxt)
            kind, shift = ("all", None) if diag is None else visibility(diag, g)
            if kind != "none":
                consume(va, g, *cur, shift=shift)

    buf_a = (sa_ref, mxa_ref)
    buf_b = (sb_ref, mxb_ref)

    @pl.when(i == 0)
    def _():
        load_queries(0)
        for g in all_groups:
            produce(0, g, *buf_a)

    def pair(t):
        step(2 * t, buf_a, buf_b)
        step(2 * t + 1, buf_b, buf_a)

    def two_pairs_body(u, carry):
        pair(2 * u)
        pair(2 * u + 1)
        return carry

    lax.fori_loop(0, i // 2, two_pairs_body, 0)

    @pl.when(i % 2 == 1)
    def _():
        pair(i - 1)

    step(2 * i, buf_a, buf_b, diag=0)

    def finish():
        lam = (jnp.exp(jnp.sum(lq1_ref[...] * lk1_ref[...], axis=-1, keepdims=True))
               - jnp.exp(jnp.sum(lq2_ref[...] * lk2_ref[...], axis=-1, keepdims=True))
               + lam_init)
        num = acc_ref[0:DIFF_V_DIM, :]
        den = acc_ref[DIFF_V_DIM:DIFF_V_DIM + 1, :]
        o_t = num[:, :tq] / den[:, :tq] - lam * (num[:, tq:] / den[:, tq:])
        o = o_t.T
        o_ref[...] = (_rms(o, g_ref[...], HEAD_NORM_EPS) * (1.0 - lam_init)).astype(o_ref.dtype)

    @pl.when(i < last_block)
    def _():
        load_queries(i + 1)
        step(2 * i + 1, buf_b, buf_a, diag=1, next_block=True)
        finish()

    @pl.when(i == last_block)
    def _():
        step(2 * i + 1, buf_b, buf_a, diag=1)
        finish()


def _diffattn(lam_vecs, g, q, k, vt, tq, gw, lam_init):
    B, S, W = q.shape
    tk = vt.shape[3]
    assert tq == 2 * tk and tq % gw == 0
    mq = 2 * tq
    small = lambda a: pl.BlockSpec(a.shape, lambda b, h, i: (0, 0))
    return pl.pallas_call(
        functools.partial(_attn_body, tq=tq, gw=gw, lam_init=lam_init),
        grid=(B, W // DIFF_V_DIM, S // tq),
        in_specs=[small(a) for a in lam_vecs] + [
                  small(g),
                  pl.BlockSpec((None, S, LANES), lambda b, h, i: (b, 0, h)),
                  pl.BlockSpec((None, S, LANES), lambda b, h, i: (b, 0, h)),
                  pl.BlockSpec((None, S // tk, DIFF_V_DIM, tk), lambda b, h, i: (b, 0, h, 0))],
        out_specs=pl.BlockSpec((None, tq, LANES), lambda b, h, i: (b, i, h)),
        out_shape=jax.ShapeDtypeStruct((B, S, W), BF16),
        scratch_shapes=[pltpu.VMEM((mq, LANES), BF16),
                        pltpu.VMEM((8, mq), F32),
                        pltpu.VMEM((DIFF_V_DIM + BF16_ROWS, mq), F32),
                        pltpu.VMEM((tk, mq), F32), pltpu.VMEM((tk, mq), F32),
                        pltpu.VMEM((8, mq), F32), pltpu.VMEM((8, mq), F32)],
        compiler_params=pltpu.CompilerParams(
            dimension_semantics=("arbitrary", "arbitrary", "arbitrary"),
            vmem_limit_bytes=48 * 1024 * 1024),
        name="diffattn",
    )(*lam_vecs, g, q, k, vt)


def _retention_consts(n_heads, chunk):
    log_g = np.log1p(-(2.0 ** (-5.0 - np.arange(n_heads, dtype=np.float64))))
    n = np.arange(chunk, dtype=np.float64)
    rel = n[:, None] - n[None, :]
    dmat = np.where(rel >= 0, np.exp(log_g[:, None, None] * np.maximum(rel, 0.0)), 0.0)
    qdec = np.exp(log_g[:, None] * (n + 1.0)[None, :])
    kdec = np.exp(log_g[:, None] * (chunk - 1.0 - n)[None, :])
    cdec = np.exp(log_g * chunk)
    bl = lambda a: np.ascontiguousarray(np.broadcast_to(a[:, :, None], a.shape + (LANES,)))
    cdec_rows = np.repeat(cdec, RET_KEY_DIM).reshape(n_heads // 2, LANES)
    return tuple(jnp.asarray(a, F32) for a in (dmat, bl(qdec), bl(kdec), bl(cdec_rows)))


def _outffn_body(x_ref, a_ref, r_ref, wo_ref, g2_ref, wg_ref, wu_ref, wd_ref, gf_ref, o_ref):
    mix = jnp.concatenate([a_ref[...], r_ref[...]], axis=1)
    x1 = x_ref[...] + _dot(mix, wo_ref[...])
    h = _rms(x1, g2_ref[...], NORM_EPS).astype(BF16)
    gate = _dot(h, wg_ref[...])
    up = _dot(h, wu_ref[...])
    act = (gate * jax.nn.sigmoid(gate) * up).astype(BF16)
    x2 = x1 + _dot(act, wd_ref[...])
    o_ref[...] = _rms(x2, gf_ref[...], NORM_EPS)


def _outffn(x2d, a, r, wo, g2, wg, wu, wd, gf, tm):
    T, D = x2d.shape
    row = lambda t: (t, 0)
    const = lambda t: (0, 0)
    resident = lambda arr: pl.BlockSpec(arr.shape, const, pipeline_mode=pl.Buffered(1))
    return pl.pallas_call(
        _outffn_body,
        grid=(T // tm,),
        in_specs=[pl.BlockSpec((tm, D), row),
                  pl.BlockSpec((tm, a.shape[1]), row),
                  pl.BlockSpec((tm, r.shape[1]), row),
                  resident(wo), pl.BlockSpec(g2.shape, const),
                  resident(wg), resident(wu), resident(wd),
                  pl.BlockSpec(gf.shape, const)],
        out_specs=pl.BlockSpec((tm, D), row),
        out_shape=jax.ShapeDtypeStruct((T, D), F32),
        compiler_params=pltpu.CompilerParams(
            dimension_semantics=("arbitrary",), vmem_limit_bytes=56 * 1024 * 1024),
        name="outffn",
    )(x2d, a, r, wo, g2, wg, wu, wd, gf)


def _rotation_tables(seq, tm):
    d = DIFF_HEAD_DIM
    inv_h = ROPE_THETA ** (-np.arange(0, d, 2, dtype=np.float64) / d)
    lanes_h = np.tile(np.concatenate([inv_h, inv_h]), LANES // d)
    dr = RET_KEY_DIM
    inv_r = 1.0 / (RET_THETA ** np.linspace(0.0, 1.0, dr // 2, dtype=np.float64))
    lanes_r = np.tile(np.repeat(inv_r, 2), LANES // dr)

    def cos_sin(pos):
        out = []
        for freq in (lanes_h, lanes_r):
            ang = pos[:, None] * freq[None, :]
            out += [np.cos(ang), np.sin(ang)]
        return jnp.asarray(np.stack(out), F32)

    starts = np.arange(seq // tm, dtype=np.float64) * tm
    return cos_sin(starts), cos_sin(np.arange(tm, dtype=np.float64))


def kernel(x, attn_norm_g, w_in, lambda_q1, lambda_k1, lambda_q2, lambda_k2, diff_subln_g,
           ret_norm_g, w_out, ffn_norm_g, w_gate, w_up, w_down, final_norm_g):
    B, S, D = x.shape
    assert w_in.shape[0] == 1, "single-layer block only"
    l = 0
    diff_w = D // 2
    ret_w = D - diff_w
    ret_heads = ret_w // RET_V_DIM
    widths = (diff_w, diff_w, diff_w, ret_heads * RET_KEY_DIM, ret_heads * RET_KEY_DIM, ret_w, ret_w)
    tm_in = 512
    tabs = _rotation_tables(S, tm_in)
    ret_chunk = 128
    ret_consts = _retention_consts(ret_heads, ret_chunk)
    xs = x.reshape(B * S, D)
    lam_init = 0.8 - 0.6 * math.exp(-0.3 * l)
    q, k, vt, r = _inproj(
        xs, attn_norm_g[l][None, :], w_in[l].astype(BF16), tabs, ret_consts, ret_norm_g[l][None, :],
        S, widths, tm=tm_in, tk=512, chunk=ret_chunk)
    tok3 = lambda t: t.reshape(B, S, t.shape[-1])
    lam_vecs = [v[l:l + 1].astype(F32) for v in (lambda_q1, lambda_k1, lambda_q2, lambda_k2)]
    a = _diffattn(lam_vecs, diff_subln_g[l][None, :], tok3(q), tok3(k), vt, tq=1024, gw=256,
                  lam_init=lam_init)
    out = _outffn(xs, a.reshape(B * S, diff_w), r, w_out[l].astype(BF16),
                  ffn_norm_g[l][None, :], w_gate[l].astype(BF16), w_up[l].astype(BF16),
                  w_down[l].astype(BF16), final_norm_g[None, :], tm=512)
    return out.reshape(B, S, D)
```

```python
import functools
import math

import jax
import jax.numpy as jnp
import numpy as np
from jax import lax
from jax.experimental import pallas as pl
from jax.experimental.pallas import tpu as pltpu

F32 = jnp.float32
BF16 = jnp.bfloat16

DIFF_HEAD_DIM = 64
DIFF_V_DIM = 128
RET_KEY_DIM = 64
RET_V_DIM = 128
ROPE_THETA = 10000.0
RET_THETA = 10000.0
NORM_EPS = 1e-6
HEAD_NORM_EPS = 1e-5
LANES = 128
BF16_ROWS = 16
LOG2E = math.log2(math.e)

NT_DIMS = (((1,), (1,)), ((), ()))
TN_DIMS = (((0,), (0,)), ((), ()))


def _dot(a, b):
    return jnp.dot(a, b, preferred_element_type=F32)


def _rms(x, g, eps):
    ms = jnp.mean(x * x, axis=-1, keepdims=True)
    return x * lax.rsqrt(ms + eps) * g


def _retention_tile(rq_ref, rk_ref, rv_ref, rg_ref, o_ref, state_ref,
                    dmat_ref, qdec_ref, kdec_ref, cdec_ref, g_ref, chunk):
    n_chunks = rq_ref.shape[0] // chunk
    n_pairs = rq_ref.shape[1] // LANES
    lane = lax.broadcasted_iota(jnp.int32, (chunk, LANES), 1)
    row = lax.broadcasted_iota(jnp.int32, (LANES, LANES), 0)
    rows = [slice(c * chunk, (c + 1) * chunk) for c in range(n_chunks)]
    head_lanes = [lane < RET_KEY_DIM, lane >= RET_KEY_DIM]

    def q_masked(c, p, hh):
        qp = rq_ref[rows[c], p * LANES:(p + 1) * LANES]
        return jnp.where(head_lanes[hh], qp, jnp.zeros_like(qp))

    scores, updates = {}, {}
    for c in range(n_chunks):
        for p in range(n_pairs):
            kp = rk_ref[rows[c], p * LANES:(p + 1) * LANES]
            for hh in range(2):
                h = 2 * p + hh
                vh = rv_ref[rows[c], h * RET_V_DIM:(h + 1) * RET_V_DIM]
                s = lax.dot_general(q_masked(c, p, hh), kp, NT_DIMS, preferred_element_type=F32)
                scores[c, h] = (s * dmat_ref[h]).astype(BF16)
                vd = (vh.astype(F32) * kdec_ref[h]).astype(BF16)
                updates[c, h] = lax.dot_general(kp, vd, TN_DIMS, preferred_element_type=F32)

    states = {}
    for p in range(n_pairs):
        st = state_ref[p]
        for c in range(n_chunks):
            states[c, p] = st.astype(BF16)
            st = cdec_ref[p] * st + jnp.where(row < RET_KEY_DIM, updates[c, 2 * p], updates[c, 2 * p + 1])
        state_ref[p] = st

    for c in range(n_chunks):
        for p in range(n_pairs):
            for hh in range(2):
                h = 2 * p + hh
                hs = slice(h * RET_V_DIM, (h + 1) * RET_V_DIM)
                cross = _dot(q_masked(c, p, hh), states[c, p]) * qdec_ref[h]
                o = _dot(scores[c, h], rv_ref[rows[c], hs]) + cross
                mu = jnp.mean(o, axis=-1, keepdims=True)
                cen = o - mu
                var = jnp.mean(cen * cen, axis=-1, keepdims=True)
                y = cen * lax.rsqrt(var + HEAD_NORM_EPS) * g_ref[:, hs]
                gate = rg_ref[rows[c], hs]
                o_ref[rows[c], hs] = (gate * jax.nn.sigmoid(gate) * y).astype(o_ref.dtype)


def _inproj_body(x_ref, g_ref, w_ref, base_ref, off_ref,
                 dmat_ref, qdec_ref, kdec_ref, cdec_ref, gr_ref,
                 q_ref, k_ref, vt_ref, r_ref,
                 rq_ref, rk_ref, rv_ref, rg_ref, state_ref, wvt_ref,
                 *, widths, tiles_per_seq, chunk):
    dq_w, dk_w, dv_w, rq_w, rk_w, rv_w, rg_w = widths
    tile_in_seq = pl.program_id(0) % tiles_per_seq

    @pl.when(pl.program_id(0) == 0)
    def _():
        wv = w_ref[:, dq_w + dk_w:dq_w + dk_w + dv_w]
        wvt_ref[...] = wv.astype(F32).T.astype(wvt_ref.dtype)

    @pl.when(tile_in_seq == 0)
    def _():
        state_ref[...] = jnp.zeros(state_ref.shape, F32)

    h = _rms(x_ref[...], g_ref[...], NORM_EPS).astype(BF16)
    lane = lax.broadcasted_iota(jnp.int32, off_ref.shape[1:], 1)
    first_half = (lane & (DIFF_HEAD_DIM // 2)) == 0
    even = (lane & 1) == 0

    def angle_tables(kind, negate_where):
        cb = base_ref[2 * kind, pl.ds(tile_in_seq, 1), :]
        sb = base_ref[2 * kind + 1, pl.ds(tile_in_seq, 1), :]
        co, so = off_ref[2 * kind], off_ref[2 * kind + 1]
        sin = sb * co + cb * so
        return cb * co - sb * so, jnp.where(negate_where, -sin, sin)

    ch, sh = angle_tables(0, first_half)
    ci, si = angle_tables(1, even)

    def rope(xc):
        sw = jnp.where(first_half, pltpu.roll(xc, LANES - 32, 1), pltpu.roll(xc, 32, 1))
        return xc * ch + sw * sh

    def pair_rot(xc):
        sw = jnp.where(even, pltpu.roll(xc, LANES - 1, 1), pltpu.roll(xc, 1, 1))
        return xc * ci + sw * si

    def emit(out_ref, c0, width, fn):
        p = _dot(h, w_ref[:, c0:c0 + width])
        for j in range(width // LANES):
            sl = slice(j * LANES, (j + 1) * LANES)
            out_ref[:, sl] = fn(p[:, sl]).astype(out_ref.dtype)

    c = 0
    emit(q_ref, c, dq_w, lambda t: rope(t) * (DIFF_HEAD_DIM ** -0.5 * LOG2E)); c += dq_w
    emit(k_ref, c, dk_w, rope); c += dk_w
    vt = lax.dot_general(wvt_ref[...], h, NT_DIMS, preferred_element_type=F32)
    tkc = vt_ref.shape[-1]
    for j in range(vt_ref.shape[0]):
        vt_ref[j] = vt[:, j * tkc:(j + 1) * tkc].astype(vt_ref.dtype)
    c += dv_w
    emit(rq_ref, c, rq_w, pair_rot); c += rq_w
    emit(rk_ref, c, rk_w, lambda t: pair_rot(t) * (RET_KEY_DIM ** -0.5)); c += rk_w
    emit(rv_ref, c, rv_w, lambda t: t); c += rv_w
    emit(rg_ref, c, rg_w, lambda t: t)
    _retention_tile(rq_ref, rk_ref, rv_ref, rg_ref, r_ref, state_ref,
                    dmat_ref, qdec_ref, kdec_ref, cdec_ref, gr_ref, chunk)


def _inproj(x2d, g, w, tabs, ret_consts, gr, seq, widths, tm, tk, chunk):
    T, D = x2d.shape
    P = w.shape[1]
    nps = seq // tm
    dq_w, dk_w, dv_w, rq_w, rk_w, rv_w, rg_w = widths
    row = lambda t: (t, 0)
    const = lambda t: (0, 0)
    full = lambda a: pl.BlockSpec(a.shape, lambda t: (0,) * a.ndim)
    out_shape = [jax.ShapeDtypeStruct((T, dq_w), BF16),
                 jax.ShapeDtypeStruct((T, dk_w), BF16),
                 jax.ShapeDtypeStruct((T // seq, seq // tk, dv_w, tk), BF16),
                 jax.ShapeDtypeStruct((T, rv_w), BF16)]
    out_specs = [pl.BlockSpec((tm, dq_w), row),
                 pl.BlockSpec((tm, dk_w), row),
                 pl.BlockSpec((None, tm // tk, dv_w, tk), lambda t: (t // nps, t % nps, 0, 0)),
                 pl.BlockSpec((tm, rv_w), row)]
    return pl.pallas_call(
        functools.partial(_inproj_body, widths=widths, tiles_per_seq=nps, chunk=chunk),
        grid=(T // tm,),
        in_specs=[pl.BlockSpec((tm, D), row),
                  pl.BlockSpec((1, D), const),
                  pl.BlockSpec((D, P), const, pipeline_mode=pl.Buffered(1))]
                 + [full(a) for a in tabs] + [full(a) for a in ret_consts] + [full(gr)],
        out_specs=out_specs,
        out_shape=out_shape,
        scratch_shapes=[pltpu.VMEM((tm, rq_w), BF16), pltpu.VMEM((tm, rk_w), BF16),
                        pltpu.VMEM((tm, rv_w), BF16), pltpu.VMEM((tm, rg_w), F32),
                        pltpu.VMEM((rq_w // LANES, LANES, LANES), F32),
                        pltpu.VMEM((dv_w, D), BF16)],
        compiler_params=pltpu.CompilerParams(
            dimension_semantics=("arbitrary",), vmem_limit_bytes=48 * 1024 * 1024),
        name="inproj",
    )(x2d, g, w, *tabs, *ret_consts, gr)


def _attn_body(lq1_ref, lk1_ref, lq2_ref, lk2_ref, g_ref, q_ref, k_ref, vt_ref, o_ref,
               q2_ref, m_ref, acc_ref,
               sa_ref, sb_ref, mxa_ref, mxb_ref, *, tq, gw, lam_init):
    i = pl.program_id(2)
    last_block = pl.num_programs(2) - 1
    tk = tq // 2
    mq = 2 * tq
    groups = [slice(g * gw, (g + 1) * gw) for g in range(mq // gw)]
    all_groups = tuple(range(len(groups)))
    lane = lax.broadcasted_iota(jnp.int32, (tq, LANES), 1)

    def load_queries(block):
        q = q_ref[pl.ds(pl.multiple_of(block * tq, tq), tq), :]
        zero = jnp.zeros_like(q)
        q2_ref[0:tq, :] = jnp.where(lane < DIFF_HEAD_DIM, q, zero)
        q2_ref[tq:mq, :] = jnp.where(lane >= DIFF_HEAD_DIM, q, zero)

    m_ref[...] = jnp.full(m_ref.shape, -jnp.inf, F32)
    acc_ref[...] = jnp.zeros(acc_ref.shape, F32)
    ones = jnp.ones((BF16_ROWS, tk), BF16)
    key_off = lax.broadcasted_iota(jnp.int32, (tk, gw), 0)
    qry_off = lax.broadcasted_iota(jnp.int32, (tk, gw), 1)

    def visibility(r, g):
        q0 = (g * gw) % tq
        k0 = r * tk
        if k0 + tk - 1 <= q0:
            return "all", None
        if k0 > q0 + gw - 1:
            return "none", None
        return "some", k0 - q0

    def produce(j, g, s_ref, mx_ref):
        kc = k_ref[pl.ds(pl.multiple_of(j * tk, tk), tk), :]
        s = lax.dot_general(kc, q2_ref[groups[g], :], NT_DIMS, preferred_element_type=F32)
        s_ref[:, groups[g]] = s
        mx_ref[0:1, groups[g]] = jnp.max(s, axis=0, keepdims=True)

    def consume(va, g, s_ref, mx_ref, shift):
        cs = groups[g]
        m_old = m_ref[0:1, cs]
        if shift is None:
            s = s_ref[:, cs]
            m_new = jnp.maximum(m_old, mx_ref[0:1, cs])
        else:
            s = jnp.where(key_off + shift <= qry_off, s_ref[:, cs], -jnp.inf)
            m_new = jnp.maximum(m_old, jnp.max(s, axis=0, keepdims=True))
        alpha = jnp.exp2(m_old - m_new)
        p = jnp.exp2(s - m_new).astype(BF16)
        m_ref[0:1, cs] = m_new
        acc_ref[:, cs] = alpha * acc_ref[:, cs] + _dot(va, p)

    def step(j, cur, nxt, diag=None):
        va = jnp.concatenate([vt_ref[j], ones], axis=0)
        for g in all_groups:
            if diag is None or (diag == 0 and visibility(1, g)[0] != "none"):
                produce(j + 1, g, *nxt)
            kind, shift = ("all", None) if diag is None else visibility(diag, g)
            if kind != "none":
                consume(va, g, *cur, shift=shift)

    buf_a = (sa_ref, mxa_ref)
    buf_b = (sb_ref, mxb_ref)

    @pl.when(i == 0)
    def _():
        load_queries(0)
        for g in all_groups:
            produce(0, g, *buf_a)

    def pair(t):
        step(2 * t, buf_a, buf_b)
        step(2 * t + 1, buf_b, buf_a)

    def two_pairs_body(u, carry):
        pair(2 * u)
        pair(2 * u + 1)
        return carry

    lax.fori_loop(0, i // 2, two_pairs_body, 0)

    @pl.when(i % 2 == 1)
    def _():
        pair(i - 1)

    step(2 * i, buf_a, buf_b, diag=0)

    def finish():
        lam = (jnp.exp(jnp.sum(lq1_ref[...] * lk1_ref[...], axis=-1, keepdims=True))
               - jnp.exp(jnp.sum(lq2_ref[...] * lk2_ref[...], axis=-1, keepdims=True))
               + lam_init)
        num = acc_ref[0:DIFF_V_DIM, :]
        den = acc_ref[DIFF_V_DIM:DIFF_V_DIM + 1, :]
        o_t = num[:, :tq] / den[:, :tq] - lam * (num[:, tq:] / den[:, tq:])
        o = o_t.T
        o_ref[...] = (_rms(o, g_ref[...], HEAD_NORM_EPS) * (1.0 - lam_init)).astype(o_ref.dtype)

    @pl.when(i < last_block)
    def _():
        step(2 * i + 1, buf_b, buf_a, diag=1)
        load_queries(i + 1)
        for g in all_groups:
            produce(0, g, *buf_a)
        finish()

    @pl.when(i == last_block)
    def _():
        step(2 * i + 1, buf_b, buf_a, diag=1)
        finish()


def _diffattn(lam_vecs, g, q, k, vt, tq, gw, lam_init):
    B, S, W = q.shape
    tk = vt.shape[3]
    assert tq == 2 * tk and tq % gw == 0
    mq = 2 * tq
    small = lambda a: pl.BlockSpec(a.shape, lambda b, h, i: (0, 0))
    return pl.pallas_call(
        functools.partial(_attn_body, tq=tq, gw=gw, lam_init=lam_init),
        grid=(B, W // DIFF_V_DIM, S // tq),
        in_specs=[small(a) for a in lam_vecs] + [
                  small(g),
                  pl.BlockSpec((None, S, LANES), lambda b, h, i: (b, 0, h)),
                  pl.BlockSpec((None, S, LANES), lambda b, h, i: (b, 0, h)),
                  pl.BlockSpec((None, S // tk, DIFF_V_DIM, tk), lambda b, h, i: (b, 0, h, 0))],
        out_specs=pl.BlockSpec((None, tq, LANES), lambda b, h, i: (b, i, h)),
        out_shape=jax.ShapeDtypeStruct((B, S, W), BF16),
        scratch_shapes=[pltpu.VMEM((mq, LANES), BF16),
                        pltpu.VMEM((8, mq), F32),
                        pltpu.VMEM((DIFF_V_DIM + BF16_ROWS, mq), F32),
                        pltpu.VMEM((tk, mq), F32), pltpu.VMEM((tk, mq), F32),
                        pltpu.VMEM((8, mq), F32), pltpu.VMEM((8, mq), F32)],
        compiler_params=pltpu.CompilerParams(
            dimension_semantics=("arbitrary", "arbitrary", "arbitrary"),
            vmem_limit_bytes=48 * 1024 * 1024),
        name="diffattn",
    )(*lam_vecs, g, q, k, vt)


def _retention_consts(n_heads, chunk):
    log_g = np.log1p(-(2.0 ** (-5.0 - np.arange(n_heads, dtype=np.float64))))
    n = np.arange(chunk, dtype=np.float64)
    rel = n[:, None] - n[None, :]
    dmat = np.where(rel >= 0, np.exp(log_g[:, None, None] * np.maximum(rel, 0.0)), 0.0)
    qdec = np.exp(log_g[:, None] * (n + 1.0)[None, :])
    kdec = np.exp(log_g[:, None] * (chunk - 1.0 - n)[None, :])
    cdec = np.exp(log_g * chunk)
    bl = lambda a: np.ascontiguousarray(np.broadcast_to(a[:, :, None], a.shape + (LANES,)))
    cdec_rows = np.repeat(cdec, RET_KEY_DIM).reshape(n_heads // 2, LANES)
    return tuple(jnp.asarray(a, F32) for a in (dmat, bl(qdec), bl(kdec), bl(cdec_rows)))


def _outffn_body(x_ref, a_ref, r_ref, wo_ref, g2_ref, wg_ref, wu_ref, wd_ref, gf_ref, o_ref):
    mix = jnp.concatenate([a_ref[...], r_ref[...]], axis=1)
    x1 = x_ref[...] + _dot(mix, wo_ref[...])
    h = _rms(x1, g2_ref[...], NORM_EPS).astype(BF16)
    gate = _dot(h, wg_ref[...])
    up = _dot(h, wu_ref[...])
    act = (gate * jax.nn.sigmoid(gate) * up).astype(BF16)
    x2 = x1 + _dot(act, wd_ref[...])
    o_ref[...] = _rms(x2, gf_ref[...], NORM_EPS)


def _outffn(x2d, a, r, wo, g2, wg, wu, wd, gf, tm):
    T, D = x2d.shape
    row = lambda t: (t, 0)
    const = lambda t: (0, 0)
    resident = lambda arr: pl.BlockSpec(arr.shape, const, pipeline_mode=pl.Buffered(1))
    return pl.pallas_call(
        _outffn_body,
        grid=(T // tm,),
        in_specs=[pl.BlockSpec((tm, D), row),
                  pl.BlockSpec((tm, a.shape[1]), row),
                  pl.BlockSpec((tm, r.shape[1]), row),
                  resident(wo), pl.BlockSpec(g2.shape, const),
                  resident(wg), resident(wu), resident(wd),
                  pl.BlockSpec(gf.shape, const)],
        out_specs=pl.BlockSpec((tm, D), row),
        out_shape=jax.ShapeDtypeStruct((T, D), F32),
        compiler_params=pltpu.CompilerParams(
            dimension_semantics=("arbitrary",), vmem_limit_bytes=56 * 1024 * 1024),
        name="outffn",
    )(x2d, a, r, wo, g2, wg, wu, wd, gf)


def _rotation_tables(seq, tm):
    d = DIFF_HEAD_DIM
    inv_h = ROPE_THETA ** (-np.arange(0, d, 2, dtype=np.float64) / d)
    lanes_h = np.tile(np.concatenate([inv_h, inv_h]), LANES // d)
    dr = RET_KEY_DIM
    inv_r = 1.0 / (RET_THETA ** np.linspace(0.0, 1.0, dr // 2, dtype=np.float64))
    lanes_r = np.tile(np.repeat(inv_r, 2), LANES // dr)

    def cos_sin(pos):
        out = []
        for freq in (lanes_h, lanes_r):
            ang = pos[:, None] * freq[None, :]
            out += [np.cos(ang), np.sin(ang)]
        return jnp.asarray(np.stack(out), F32)

    starts = np.arange(seq // tm, dtype=np.float64) * tm
    return cos_sin(starts), cos_sin(np.arange(tm, dtype=np.float64))


def kernel(x, attn_norm_g, w_in, lambda_q1, lambda_k1, lambda_q2, lambda_k2, diff_subln_g,
           ret_norm_g, w_out, ffn_norm_g, w_gate, w_up, w_down, final_norm_g):
    B, S, D = x.shape
    assert w_in.shape[0] == 1, "single-layer block only"
    l = 0
    diff_w = D // 2
    ret_w = D - diff_w
    ret_heads = ret_w // RET_V_DIM
    widths = (diff_w, diff_w, diff_w, ret_heads * RET_KEY_DIM, ret_heads * RET_KEY_DIM, ret_w, ret_w)
    tm_in = 512
    tabs = _rotation_tables(S, tm_in)
    ret_chunk = 128
    ret_consts = _retention_consts(ret_heads, ret_chunk)
    xs = x.reshape(B * S, D)
    lam_init = 0.8 - 0.6 * math.exp(-0.3 * l)
    q, k, vt, r = _inproj(
        xs, attn_norm_g[l][None, :], w_in[l].astype(BF16), tabs, ret_consts, ret_norm_g[l][None, :],
        S, widths, tm=tm_in, tk=512, chunk=ret_chunk)
    tok3 = lambda t: t.reshape(B, S, t.shape[-1])
    lam_vecs = [v[l:l + 1].astype(F32) for v in (lambda_q1, lambda_k1, lambda_q2, lambda_k2)]
    a = _diffattn(lam_vecs, diff_subln_g[l][None, :], tok3(q), tok3(k), vt, tq=1024, gw=256,
                  lam_init=lam_init)
    out = _outffn(xs, a.reshape(B * S, diff_w), r, w_out[l].astype(BF16),
                  ffn_norm_g[l][None, :], w_gate[l].astype(BF16), w_up[l].astype(BF16),
                  w_down[l].astype(BF16), final_norm_g[None, :], tm=512)
    return out.reshape(B, S, D)
```

```python
import functools
import math

import jax
import jax.numpy as jnp
import numpy as np
from jax import lax
from jax.experimental import pallas as pl
from jax.experimental.pallas import tpu as pltpu

F32 = jnp.float32
BF16 = jnp.bfloat16

DIFF_HEAD_DIM = 64
DIFF_V_DIM = 128
RET_KEY_DIM = 64
RET_V_DIM = 128
ROPE_THETA = 10000.0
RET_THETA = 10000.0
NORM_EPS = 1e-6
HEAD_NORM_EPS = 1e-5
LANES = 128
BF16_ROWS = 16
LOG2E = math.log2(math.e)

NT_DIMS = (((1,), (1,)), ((), ()))
TN_DIMS = (((0,), (0,)), ((), ()))


def _dot(a, b):
    return jnp.dot(a, b, preferred_element_type=F32)


def _rms(x, g, eps):
    ms = jnp.mean(x * x, axis=-1, keepdims=True)
    return x * lax.rsqrt(ms + eps) * g


def _retention_tile(rq_ref, rk_ref, rv_ref, rg_ref, o_ref, state_ref,
                    dmat_ref, qdec_ref, kdec_ref, cdec_ref, g_ref, chunk):
    n_chunks = rq_ref.shape[0] // chunk
    n_pairs = rq_ref.shape[1] // LANES
    lane = lax.broadcasted_iota(jnp.int32, (chunk, LANES), 1)
    row = lax.broadcasted_iota(jnp.int32, (LANES, LANES), 0)
    rows = [slice(c * chunk, (c + 1) * chunk) for c in range(n_chunks)]
    head_lanes = [lane < RET_KEY_DIM, lane >= RET_KEY_DIM]

    def q_masked(c, p, hh):
        qp = rq_ref[rows[c], p * LANES:(p + 1) * LANES]
        return jnp.where(head_lanes[hh], qp, jnp.zeros_like(qp))

    scores, updates = {}, {}
    for c in range(n_chunks):
        for p in range(n_pairs):
            kp = rk_ref[rows[c], p * LANES:(p + 1) * LANES]
            for hh in range(2):
                h = 2 * p + hh
                vh = rv_ref[rows[c], h * RET_V_DIM:(h + 1) * RET_V_DIM]
                s = lax.dot_general(q_masked(c, p, hh), kp, NT_DIMS, preferred_element_type=F32)
                scores[c, h] = (s * dmat_ref[h]).astype(BF16)
                vd = (vh.astype(F32) * kdec_ref[h]).astype(BF16)
                updates[c, h] = lax.dot_general(kp, vd, TN_DIMS, preferred_element_type=F32)

    states = {}
    for p in range(n_pairs):
        st = state_ref[p]
        for c in range(n_chunks):
            states[c, p] = st.astype(BF16)
            st = cdec_ref[p] * st + jnp.where(row < RET_KEY_DIM, updates[c, 2 * p], updates[c, 2 * p + 1])
        state_ref[p] = st

    yield

    for c in range(n_chunks):
        for p in range(n_pairs):
            for hh in range(2):
                h = 2 * p + hh
                hs = slice(h * RET_V_DIM, (h + 1) * RET_V_DIM)
                cross = _dot(q_masked(c, p, hh), states[c, p]) * qdec_ref[h]
                o = _dot(scores[c, h], rv_ref[rows[c], hs]) + cross
                mu = jnp.mean(o, axis=-1, keepdims=True)
                cen = o - mu
                var = jnp.mean(cen * cen, axis=-1, keepdims=True)
                y = cen * lax.rsqrt(var + HEAD_NORM_EPS) * g_ref[:, hs]
                gate = rg_ref[rows[c], hs]
                o_ref[rows[c], hs] = (gate * jax.nn.sigmoid(gate) * y).astype(o_ref.dtype)


def _inproj_body(x_ref, g_ref, w_ref, base_ref, off_ref,
                 dmat_ref, qdec_ref, kdec_ref, cdec_ref, gr_ref,
                 q_ref, k_ref, vt_ref, r_ref,
                 rq_ref, rk_ref, rv_ref, rg_ref, state_ref, wvt_ref,
                 *, widths, tiles_per_seq, chunk):
    dq_w, dk_w, dv_w, rq_w, rk_w, rv_w, rg_w = widths
    tile_in_seq = pl.program_id(0) % tiles_per_seq

    @pl.when(pl.program_id(0) == 0)
    def _():
        wv = w_ref[:, dq_w + dk_w:dq_w + dk_w + dv_w]
        wvt_ref[...] = wv.astype(F32).T.astype(wvt_ref.dtype)

    @pl.when(tile_in_seq == 0)
    def _():
        state_ref[...] = jnp.zeros(state_ref.shape, F32)

    h = _rms(x_ref[...], g_ref[...], NORM_EPS).astype(BF16)
    lane = lax.broadcasted_iota(jnp.int32, off_ref.shape[1:], 1)
    first_half = (lane & (DIFF_HEAD_DIM // 2)) == 0
    even = (lane & 1) == 0

    def angle_tables(kind, negate_where):
        cb = base_ref[2 * kind, pl.ds(tile_in_seq, 1), :]
        sb = base_ref[2 * kind + 1, pl.ds(tile_in_seq, 1), :]
        co, so = off_ref[2 * kind], off_ref[2 * kind + 1]
        sin = sb * co + cb * so
        return cb * co - sb * so, jnp.where(negate_where, -sin, sin)

    ch, sh = angle_tables(0, first_half)
    ci, si = angle_tables(1, even)

    def rope(xc):
        sw = jnp.where(first_half, pltpu.roll(xc, LANES - 32, 1), pltpu.roll(xc, 32, 1))
        return xc * ch + sw * sh

    def pair_rot(xc):
        sw = jnp.where(even, pltpu.roll(xc, LANES - 1, 1), pltpu.roll(xc, 1, 1))
        return xc * ci + sw * si

    def emit(out_ref, c0, width, fn):
        p = _dot(h, w_ref[:, c0:c0 + width])
        for j in range(width // LANES):
            sl = slice(j * LANES, (j + 1) * LANES)
            out_ref[:, sl] = fn(p[:, sl]).astype(out_ref.dtype)

    c_q, c_k = 0, dq_w
    c_rq = dq_w + dk_w + dv_w
    c_rk, c_rv, c_rg = c_rq + rq_w, c_rq + rq_w + rk_w, c_rq + rq_w + rk_w + rv_w
    emit(rq_ref, c_rq, rq_w, pair_rot)
    emit(rk_ref, c_rk, rk_w, lambda t: pair_rot(t) * (RET_KEY_DIM ** -0.5))
    emit(rv_ref, c_rv, rv_w, lambda t: t)
    emit(rg_ref, c_rg, rg_w, lambda t: t)
    retention = _retention_tile(rq_ref, rk_ref, rv_ref, rg_ref, r_ref, state_ref,
                                dmat_ref, qdec_ref, kdec_ref, cdec_ref, gr_ref, chunk)
    next(retention)
    emit(q_ref, c_q, dq_w, lambda t: rope(t) * (DIFF_HEAD_DIM ** -0.5 * LOG2E))
    next(retention, None)
    emit(k_ref, c_k, dk_w, rope)
    vt = lax.dot_general(wvt_ref[...], h, NT_DIMS, preferred_element_type=F32)
    tkc = vt_ref.shape[-1]
    for j in range(vt_ref.shape[0]):
        vt_ref[j] = vt[:, j * tkc:(j + 1) * tkc].astype(vt_ref.dtype)


def _inproj(x2d, g, w, tabs, ret_consts, gr, seq, widths, tm, tk, chunk):
    T, D = x2d.shape
    P = w.shape[1]
    nps = seq // tm
    dq_w, dk_w, dv_w, rq_w, rk_w, rv_w, rg_w = widths
    row = lambda t: (t, 0)
    const = lambda t: (0, 0)
    full = lambda a: pl.BlockSpec(a.shape, lambda t: (0,) * a.ndim)
    out_shape = [jax.ShapeDtypeStruct((T, dq_w), BF16),
                 jax.ShapeDtypeStruct((T, dk_w), BF16),
                 jax.ShapeDtypeStruct((T // seq, seq // tk, dv_w, tk), BF16),
                 jax.ShapeDtypeStruct((T, rv_w), BF16)]
    out_specs = [pl.BlockSpec((tm, dq_w), row),
                 pl.BlockSpec((tm, dk_w), row),
                 pl.BlockSpec((None, tm // tk, dv_w, tk), lambda t: (t // nps, t % nps, 0, 0)),
                 pl.BlockSpec((tm, rv_w), row)]
    return pl.pallas_call(
        functools.partial(_inproj_body, widths=widths, tiles_per_seq=nps, chunk=chunk),
        grid=(T // tm,),
        in_specs=[pl.BlockSpec((tm, D), row),
                  pl.BlockSpec((1, D), const),
                  pl.BlockSpec((D, P), const, pipeline_mode=pl.Buffered(1))]
                 + [full(a) for a in tabs] + [full(a) for a in ret_consts] + [full(gr)],
        out_specs=out_specs,
        out_shape=out_shape,
        scratch_shapes=[pltpu.VMEM((tm, rq_w), BF16), pltpu.VMEM((tm, rk_w), BF16),
                        pltpu.VMEM((tm, rv_w), BF16), pltpu.VMEM((tm, rg_w), F32),
                        pltpu.VMEM((rq_w // LANES, LANES, LANES), F32),
                        pltpu.VMEM((dv_w, D), BF16)],
        compiler_params=pltpu.CompilerParams(
            dimension_semantics=("arbitrary",), vmem_limit_bytes=48 * 1024 * 1024),
        name="inproj",
    )(x2d, g, w, *tabs, *ret_consts, gr)


def _attn_body(*refs, tq, gw, lam_init, cast_blocks):
    n_cast = len(cast_blocks)
    lq1_ref, lk1_ref, lq2_ref, lk2_ref, g_ref, q_ref, k_ref, vt_ref = refs[:8]
    w_refs = refs[8:8 + n_cast]
    o_ref = refs[8 + n_cast]
    wb_refs = refs[9 + n_cast:9 + 2 * n_cast]
    q2_ref, m_ref, acc_ref, sa_ref, sb_ref, mxa_ref, mxb_ref = refs[9 + 2 * n_cast:]

    flat_step = ((pl.program_id(0) * pl.num_programs(1) + pl.program_id(1)) * pl.num_programs(2)
                 + pl.program_id(2))
    for w_ref, wb_ref, n_blocks in zip(w_refs, wb_refs, cast_blocks):
        @pl.when(flat_step < n_blocks)
        def _(w_ref=w_ref, wb_ref=wb_ref):
            wb_ref[...] = w_ref[...].astype(wb_ref.dtype)

    i = pl.program_id(2)
    last_block = pl.num_programs(2) - 1
    tk = tq // 2
    mq = 2 * tq
    groups = [slice(g * gw, (g + 1) * gw) for g in range(mq // gw)]
    all_groups = tuple(range(len(groups)))
    lane = lax.broadcasted_iota(jnp.int32, (tq, LANES), 1)

    def load_queries(block):
        q = q_ref[pl.ds(pl.multiple_of(block * tq, tq), tq), :]
        zero = jnp.zeros_like(q)
        q2_ref[0:tq, :] = jnp.where(lane < DIFF_HEAD_DIM, q, zero)
        q2_ref[tq:mq, :] = jnp.where(lane >= DIFF_HEAD_DIM, q, zero)

    m_ref[...] = jnp.full(m_ref.shape, -jnp.inf, F32)
    acc_ref[...] = jnp.zeros(acc_ref.shape, F32)
    ones = jnp.ones((BF16_ROWS, tk), BF16)
    key_off = lax.broadcasted_iota(jnp.int32, (tk, gw), 0)
    qry_off = lax.broadcasted_iota(jnp.int32, (tk, gw), 1)

    def visibility(r, g):
        q0 = (g * gw) % tq
        k0 = r * tk
        if k0 + tk - 1 <= q0:
            return "all", None
        if k0 > q0 + gw - 1:
            return "none", None
        return "some", k0 - q0

    def produce(j, g, s_ref, mx_ref):
        kc = k_ref[pl.ds(pl.multiple_of(j * tk, tk), tk), :]
        s = lax.dot_general(kc, q2_ref[groups[g], :], NT_DIMS, preferred_element_type=F32)
        s_ref[:, groups[g]] = s
        mx_ref[0:1, groups[g]] = jnp.max(s, axis=0, keepdims=True)

    def consume(va, g, s_ref, mx_ref, shift):
        cs = groups[g]
        m_old = m_ref[0:1, cs]
        if shift is None:
            s = s_ref[:, cs]
            m_new = jnp.maximum(m_old, mx_ref[0:1, cs])
        else:
            s = jnp.where(key_off + shift <= qry_off, s_ref[:, cs], -jnp.inf)
            m_new = jnp.maximum(m_old, jnp.max(s, axis=0, keepdims=True))
        alpha = jnp.exp2(m_old - m_new)
        p = jnp.exp2(s - m_new).astype(BF16)
        m_ref[0:1, cs] = m_new
        acc_ref[:, cs] = alpha * acc_ref[:, cs] + _dot(va, p)

    def step(j, cur, nxt, diag=None):
        va = jnp.concatenate([vt_ref[j], ones], axis=0)
        for g in all_groups:
            if diag is None or (diag == 0 and visibility(1, g)[0] != "none"):
                produce(j + 1, g, *nxt)
            kind, shift = ("all", None) if diag is None else visibility(diag, g)
            if kind != "none":
                consume(va, g, *cur, shift=shift)

    buf_a = (sa_ref, mxa_ref)
    buf_b = (sb_ref, mxb_ref)

    @pl.when(i == 0)
    def _():
        load_queries(0)
        for g in all_groups:
            produce(0, g, *buf_a)

    def pair(t):
        step(2 * t, buf_a, buf_b)
        step(2 * t + 1, buf_b, buf_a)

    def two_pairs_body(u, carry):
        pair(2 * u)
        pair(2 * u + 1)
        return carry

    lax.fori_loop(0, i // 2, two_pairs_body, 0)

    @pl.when(i % 2 == 1)
    def _():
        pair(i - 1)

    step(2 * i, buf_a, buf_b, diag=0)

    def finish():
        lam = (jnp.exp(jnp.sum(lq1_ref[...] * lk1_ref[...], axis=-1, keepdims=True))
               - jnp.exp(jnp.sum(lq2_ref[...] * lk2_ref[...], axis=-1, keepdims=True))
               + lam_init)
        num = acc_ref[0:DIFF_V_DIM, :]
        den = acc_ref[DIFF_V_DIM:DIFF_V_DIM + 1, :]
        o_t = num[:, :tq] / den[:, :tq] - lam * (num[:, tq:] / den[:, tq:])
        o = o_t.T
        o_ref[...] = (_rms(o, g_ref[...], HEAD_NORM_EPS) * (1.0 - lam_init)).astype(o_ref.dtype)

    @pl.when(i < last_block)
    def _():
        step(2 * i + 1, buf_b, buf_a, diag=1)
        load_queries(i + 1)
        for g in all_groups:
            produce(0, g, *buf_a)
        finish()

    @pl.when(i == last_block)
    def _():
        step(2 * i + 1, buf_b, buf_a, diag=1)
        finish()


def _cast_row_block(rows, steps):
    rb = BF16_ROWS
    while rows % rb or rows // rb > steps:
        rb += BF16_ROWS
    return rb


def _diffattn(lam_vecs, g, q, k, vt, cast_weights, tq, gw, lam_init):
    B, S, W = q.shape
    tk = vt.shape[3]
    assert tq == 2 * tk and tq % gw == 0
    mq = 2 * tq
    n_heads, n_qblocks = W // DIFF_V_DIM, S // tq
    steps = B * n_heads * n_qblocks
    small = lambda a: pl.BlockSpec(a.shape, lambda b, h, i: (0, 0))
    row_blocks = [_cast_row_block(w.shape[0], steps) for w in cast_weights]
    cast_blocks = tuple(w.shape[0] // rb for w, rb in zip(cast_weights, row_blocks))

    def cast_spec(w, rb, nb):
        return pl.BlockSpec(
            (rb, w.shape[1]),
            lambda b, h, i: (jnp.minimum((b * n_heads + h) * n_qblocks + i, nb - 1), 0))

    cast_specs = [cast_spec(w, rb, nb) for w, rb, nb in zip(cast_weights, row_blocks, cast_blocks)]
    outs = pl.pallas_call(
        functools.partial(_attn_body, tq=tq, gw=gw, lam_init=lam_init, cast_blocks=cast_blocks),
        grid=(B, n_heads, n_qblocks),
        in_specs=[small(a) for a in lam_vecs] + [
                  small(g),
                  pl.BlockSpec((None, S, LANES), lambda b, h, i: (b, 0, h)),
                  pl.BlockSpec((None, S, LANES), lambda b, h, i: (b, 0, h)),
                  pl.BlockSpec((None, S // tk, DIFF_V_DIM, tk), lambda b, h, i: (b, 0, h, 0))]
                 + cast_specs,
        out_specs=[pl.BlockSpec((None, tq, LANES), lambda b, h, i: (b, i, h))] + cast_specs,
        out_shape=[jax.ShapeDtypeStruct((B, S, W), BF16)]
                  + [jax.ShapeDtypeStruct(w.shape, BF16) for w in cast_weights],
        scratch_shapes=[pltpu.VMEM((mq, LANES), BF16),
                        pltpu.VMEM((8, mq), F32),
                        pltpu.VMEM((DIFF_V_DIM + BF16_ROWS, mq), F32),
                        pltpu.VMEM((tk, mq), F32), pltpu.VMEM((tk, mq), F32),
                        pltpu.VMEM((8, mq), F32), pltpu.VMEM((8, mq), F32)],
        compiler_params=pltpu.CompilerParams(
            dimension_semantics=("arbitrary", "arbitrary", "arbitrary"),
            vmem_limit_bytes=48 * 1024 * 1024),
        name="diffattn",
    )(*lam_vecs, g, q, k, vt, *cast_weights)
    return outs[0], outs[1:]


def _retention_consts(n_heads, chunk):
    log_g = np.log1p(-(2.0 ** (-5.0 - np.arange(n_heads, dtype=np.float64))))
    n = np.arange(chunk, dtype=np.float64)
    rel = n[:, None] - n[None, :]
    dmat = np.where(rel >= 0, np.exp(log_g[:, None, None] * np.maximum(rel, 0.0)), 0.0)
    qdec = np.exp(log_g[:, None] * (n + 1.0)[None, :])
    kdec = np.exp(log_g[:, None] * (chunk - 1.0 - n)[None, :])
    cdec = np.exp(log_g * chunk)
    bl = lambda a: np.ascontiguousarray(np.broadcast_to(a[:, :, None], a.shape + (LANES,)))
    cdec_rows = np.repeat(cdec, RET_KEY_DIM).reshape(n_heads // 2, LANES)
    return tuple(jnp.asarray(a, F32) for a in (dmat, bl(qdec), bl(kdec), bl(cdec_rows)))


def _outffn_body(x_ref, a_ref, r_ref, wo_ref, g2_ref, wg_ref, wu_ref, wd_ref, gf_ref, o_ref):
    mix = jnp.concatenate([a_ref[...], r_ref[...]], axis=1)
    x1 = x_ref[...] + _dot(mix, wo_ref[...])
    h = _rms(x1, g2_ref[...], NORM_EPS).astype(BF16)
    gate = _dot(h, wg_ref[...])
    up = _dot(h, wu_ref[...])
    act = (gate * jax.nn.sigmoid(gate) * up).astype(BF16)
    x2 = x1 + _dot(act, wd_ref[...])
    o_ref[...] = _rms(x2, gf_ref[...], NORM_EPS)


def _outffn(x2d, a, r, wo, g2, wg, wu, wd, gf, tm):
    T, D = x2d.shape
    row = lambda t: (t, 0)
    const = lambda t: (0, 0)
    resident = lambda arr: pl.BlockSpec(arr.shape, const, pipeline_mode=pl.Buffered(1))
    return pl.pallas_call(
        _outffn_body,
        grid=(T // tm,),
        in_specs=[pl.BlockSpec((tm, D), row),
                  pl.BlockSpec((tm, a.shape[1]), row),
                  pl.BlockSpec((tm, r.shape[1]), row),
                  resident(wo), pl.BlockSpec(g2.shape, const),
                  resident(wg), resident(wu), resident(wd),
                  pl.BlockSpec(gf.shape, const)],
        out_specs=pl.BlockSpec((tm, D), row),
        out_shape=jax.ShapeDtypeStruct((T, D), F32),
        compiler_params=pltpu.CompilerParams(
            dimension_semantics=("arbitrary",), vmem_limit_bytes=56 * 1024 * 1024),
        name="outffn",
    )(x2d, a, r, wo, g2, wg, wu, wd, gf)


def _rotation_tables(seq, tm):
    d = DIFF_HEAD_DIM
    inv_h = ROPE_THETA ** (-np.arange(0, d, 2, dtype=np.float64) / d)
    lanes_h = np.tile(np.concatenate([inv_h, inv_h]), LANES // d)
    dr = RET_KEY_DIM
    inv_r = 1.0 / (RET_THETA ** np.linspace(0.0, 1.0, dr // 2, dtype=np.float64))
    lanes_r = np.tile(np.repeat(inv_r, 2), LANES // dr)

    def cos_sin(pos):
        out = []
        for freq in (lanes_h, lanes_r):
            ang = pos[:, None] * freq[None, :]
            out += [np.cos(ang), np.sin(ang)]
        return jnp.asarray(np.stack(out), F32)

    starts = np.arange(seq // tm, dtype=np.float64) * tm
    return cos_sin(starts), cos_sin(np.arange(tm, dtype=np.float64))


def kernel(x, attn_norm_g, w_in, lambda_q1, lambda_k1, lambda_q2, lambda_k2, diff_subln_g,
           ret_norm_g, w_out, ffn_norm_g, w_gate, w_up, w_down, final_norm_g):
    B, S, D = x.shape
    assert w_in.shape[0] == 1, "single-layer block only"
    l = 0
    diff_w = D // 2
    ret_w = D - diff_w
    ret_heads = ret_w // RET_V_DIM
    widths = (diff_w, diff_w, diff_w, ret_heads * RET_KEY_DIM, ret_heads * RET_KEY_DIM, ret_w, ret_w)
    tm_in = 512
    tabs = _rotation_tables(S, tm_in)
    ret_chunk = 128
    ret_consts = _retention_consts(ret_heads, ret_chunk)
    xs = x.reshape(B * S, D)
    lam_init = 0.8 - 0.6 * math.exp(-0.3 * l)
    q, k, vt, r = _inproj(
        xs, attn_norm_g[l][None, :], w_in[l].astype(BF16), tabs, ret_consts, ret_norm_g[l][None, :],
        S, widths, tm=tm_in, tk=512, chunk=ret_chunk)
    tok3 = lambda t: t.reshape(B, S, t.shape[-1])
    lam_vecs = [v[l:l + 1].astype(F32) for v in (lambda_q1, lambda_k1, lambda_q2, lambda_k2)]
    a, (wo_b, wg_b, wu_b, wd_b) = _diffattn(
        lam_vecs, diff_subln_g[l][None, :], tok3(q), tok3(k), vt,
        [w_out[l], w_gate[l], w_up[l], w_down[l]], tq=1024, gw=256, lam_init=lam_init)
    out = _outffn(xs, a.reshape(B * S, diff_w), r, wo_b, ffn_norm_g[l][None, :], wg_b, wu_b, wd_b,
                  final_norm_g[None, :], tm=512)
    return out.reshape(B, S, D)
```

```python
import functools
import math

import jax
import jax.numpy as jnp
import numpy as np
from jax import lax
from jax.experimental import pallas as pl
from jax.experimental.pallas import tpu as pltpu

F32 = jnp.float32
BF16 = jnp.bfloat16

DIFF_HEAD_DIM = 64
DIFF_V_DIM = 128
RET_KEY_DIM = 64
RET_V_DIM = 128
ROPE_THETA = 10000.0
RET_THETA = 10000.0
NORM_EPS = 1e-6
HEAD_NORM_EPS = 1e-5
LANES = 128
BF16_ROWS = 16
LOG2E = math.log2(math.e)

NT_DIMS = (((1,), (1,)), ((), ()))
TN_DIMS = (((0,), (0,)), ((), ()))


def _dot(a, b):
    return jnp.dot(a, b, preferred_element_type=F32)


def _rms(x, g, eps):
    ms = jnp.mean(x * x, axis=-1, keepdims=True)
    return x * lax.rsqrt(ms + eps) * g


def _cast_plan(weights, steps, step_of):
    specs, shapes, n_blocks = [], [], []
    for w in weights:
        rows, cols = w.shape
        rb = BF16_ROWS
        while rows % rb or rows // rb > steps:
            rb += BF16_ROWS
        nb = rows // rb
        specs.append(pl.BlockSpec(
            (rb, cols), lambda *idx, nb=nb: (jnp.minimum(step_of(*idx), nb - 1), 0)))
        shapes.append(jax.ShapeDtypeStruct(w.shape, BF16))
        n_blocks.append(nb)
    return specs, shapes, tuple(n_blocks)


def _cast_step(step, w_refs, wb_refs, n_blocks):
    for w_ref, wb_ref, nb in zip(w_refs, wb_refs, n_blocks):
        @pl.when(step < nb)
        def _(w_ref=w_ref, wb_ref=wb_ref):
            wb_ref[...] = w_ref[...].astype(wb_ref.dtype)


def _retention_tile(rq_ref, rk_ref, rv_ref, rg_ref, o_ref, state_ref,
                    dmat_ref, qdec_ref, kdec_ref, cdec_ref, g_ref, chunk):
    n_chunks = rq_ref.shape[0] // chunk
    n_pairs = rq_ref.shape[1] // LANES
    lane = lax.broadcasted_iota(jnp.int32, (chunk, LANES), 1)
    row = lax.broadcasted_iota(jnp.int32, (LANES, LANES), 0)
    rows = [slice(c * chunk, (c + 1) * chunk) for c in range(n_chunks)]
    head_lanes = [lane < RET_KEY_DIM, lane >= RET_KEY_DIM]

    def q_masked(c, p, hh):
        qp = rq_ref[rows[c], p * LANES:(p + 1) * LANES]
        return jnp.where(head_lanes[hh], qp, jnp.zeros_like(qp))

    scores, updates = {}, {}
    for c in range(n_chunks):
        for p in range(n_pairs):
            kp = rk_ref[rows[c], p * LANES:(p + 1) * LANES]
            for hh in range(2):
                h = 2 * p + hh
                vh = rv_ref[rows[c], h * RET_V_DIM:(h + 1) * RET_V_DIM]
                s = lax.dot_general(q_masked(c, p, hh), kp, NT_DIMS, preferred_element_type=F32)
                scores[c, h] = (s * dmat_ref[h]).astype(BF16)
                vd = (vh.astype(F32) * kdec_ref[h]).astype(BF16)
                updates[c, h] = lax.dot_general(kp, vd, TN_DIMS, preferred_element_type=F32)

    states = {}
    for p in range(n_pairs):
        st = state_ref[p]
        for c in range(n_chunks):
            states[c, p] = st.astype(BF16)
            st = cdec_ref[p] * st + jnp.where(row < RET_KEY_DIM, updates[c, 2 * p], updates[c, 2 * p + 1])
        state_ref[p] = st

    yield

    for c in range(n_chunks):
        for p in range(n_pairs):
            for hh in range(2):
                h = 2 * p + hh
                hs = slice(h * RET_V_DIM, (h + 1) * RET_V_DIM)
                cross = _dot(q_masked(c, p, hh), states[c, p]) * qdec_ref[h]
                o = _dot(scores[c, h], rv_ref[rows[c], hs]) + cross
                mu = jnp.mean(o, axis=-1, keepdims=True)
                cen = o - mu
                var = jnp.mean(cen * cen, axis=-1, keepdims=True)
                y = cen * lax.rsqrt(var + HEAD_NORM_EPS) * g_ref[:, hs]
                gate = rg_ref[rows[c], hs]
                o_ref[rows[c], hs] = (gate * jax.nn.sigmoid(gate) * y).astype(o_ref.dtype)


def _inproj_body(x_ref, g_ref, w_ref, base_ref, off_ref,
                 dmat_ref, qdec_ref, kdec_ref, cdec_ref, gr_ref, *refs,
                 widths, tiles_per_seq, chunk, cast_blocks):
    n_cast = len(cast_blocks)
    cast_in, refs = refs[:n_cast], refs[n_cast:]
    (q_ref, k_ref, vt_ref, r_ref), refs = refs[:4], refs[4:]
    cast_out, refs = refs[:n_cast], refs[n_cast:]
    rq_ref, rk_ref, rv_ref, rg_ref, state_ref, wvt_ref = refs
    dq_w, dk_w, dv_w, rq_w, rk_w, rv_w, rg_w = widths
    tile_in_seq = pl.program_id(0) % tiles_per_seq
    _cast_step(pl.program_id(0), cast_in, cast_out, cast_blocks)

    @pl.when(pl.program_id(0) == 0)
    def _():
        wv = w_ref[:, dq_w + dk_w:dq_w + dk_w + dv_w]
        wvt_ref[...] = wv.astype(F32).T.astype(wvt_ref.dtype)

    @pl.when(tile_in_seq == 0)
    def _():
        state_ref[...] = jnp.zeros(state_ref.shape, F32)

    h = _rms(x_ref[...], g_ref[...], NORM_EPS).astype(BF16)
    lane = lax.broadcasted_iota(jnp.int32, off_ref.shape[1:], 1)
    first_half = (lane & (DIFF_HEAD_DIM // 2)) == 0
    even = (lane & 1) == 0

    def angle_tables(kind, negate_where):
        cb = base_ref[2 * kind, pl.ds(tile_in_seq, 1), :]
        sb = base_ref[2 * kind + 1, pl.ds(tile_in_seq, 1), :]
        co, so = off_ref[2 * kind], off_ref[2 * kind + 1]
        sin = sb * co + cb * so
        return cb * co - sb * so, jnp.where(negate_where, -sin, sin)

    ch, sh = angle_tables(0, first_half)
    ci, si = angle_tables(1, even)

    def rope(xc):
        sw = jnp.where(first_half, pltpu.roll(xc, LANES - 32, 1), pltpu.roll(xc, 32, 1))
        return xc * ch + sw * sh

    def pair_rot(xc):
        sw = jnp.where(even, pltpu.roll(xc, LANES - 1, 1), pltpu.roll(xc, 1, 1))
        return xc * ci + sw * si

    def emit(out_ref, c0, width, fn):
        p = _dot(h, w_ref[:, c0:c0 + width])
        for j in range(width // LANES):
            sl = slice(j * LANES, (j + 1) * LANES)
            out_ref[:, sl] = fn(p[:, sl]).astype(out_ref.dtype)

    c_q, c_k = 0, dq_w
    c_rq = dq_w + dk_w + dv_w
    c_rk, c_rv, c_rg = c_rq + rq_w, c_rq + rq_w + rk_w, c_rq + rq_w + rk_w + rv_w
    emit(rq_ref, c_rq, rq_w, pair_rot)
    emit(rk_ref, c_rk, rk_w, lambda t: pair_rot(t) * (RET_KEY_DIM ** -0.5))
    emit(rv_ref, c_rv, rv_w, lambda t: t)
    emit(rg_ref, c_rg, rg_w, lambda t: t)
    retention = _retention_tile(rq_ref, rk_ref, rv_ref, rg_ref, r_ref, state_ref,
                                dmat_ref, qdec_ref, kdec_ref, cdec_ref, gr_ref, chunk)
    next(retention)
    emit(q_ref, c_q, dq_w, lambda t: rope(t) * (DIFF_HEAD_DIM ** -0.5 * LOG2E))
    next(retention, None)
    emit(k_ref, c_k, dk_w, rope)
    vt = lax.dot_general(wvt_ref[...], h, NT_DIMS, preferred_element_type=F32)
    tkc = vt_ref.shape[-1]
    for j in range(vt_ref.shape[0]):
        vt_ref[j] = vt[:, j * tkc:(j + 1) * tkc].astype(vt_ref.dtype)


def _inproj(x2d, g, w, tabs, ret_consts, gr, cast_weights, seq, widths, tm, tk, chunk):
    T, D = x2d.shape
    cast_specs, cast_shapes, cast_blocks = _cast_plan(cast_weights, T // tm, lambda t: t)
    P = w.shape[1]
    nps = seq // tm
    dq_w, dk_w, dv_w, rq_w, rk_w, rv_w, rg_w = widths
    row = lambda t: (t, 0)
    const = lambda t: (0, 0)
    full = lambda a: pl.BlockSpec(a.shape, lambda t: (0,) * a.ndim)
    out_shape = [jax.ShapeDtypeStruct((T, dq_w), BF16),
                 jax.ShapeDtypeStruct((T, dk_w), BF16),
                 jax.ShapeDtypeStruct((T // seq, seq // tk, dv_w, tk), BF16),
                 jax.ShapeDtypeStruct((T, rv_w), BF16)]
    out_specs = [pl.BlockSpec((tm, dq_w), row),
                 pl.BlockSpec((tm, dk_w), row),
                 pl.BlockSpec((None, tm // tk, dv_w, tk), lambda t: (t // nps, t % nps, 0, 0)),
                 pl.BlockSpec((tm, rv_w), row)]
    outs = pl.pallas_call(
        functools.partial(_inproj_body, widths=widths, tiles_per_seq=nps, chunk=chunk,
                          cast_blocks=cast_blocks),
        grid=(T // tm,),
        in_specs=[pl.BlockSpec((tm, D), row),
                  pl.BlockSpec((1, D), const),
                  pl.BlockSpec((D, P), const, pipeline_mode=pl.Buffered(1))]
                 + [full(a) for a in tabs] + [full(a) for a in ret_consts] + [full(gr)]
                 + cast_specs,
        out_specs=out_specs + cast_specs,
        out_shape=out_shape + cast_shapes,
        scratch_shapes=[pltpu.VMEM((tm, rq_w), BF16), pltpu.VMEM((tm, rk_w), BF16),
                        pltpu.VMEM((tm, rv_w), BF16), pltpu.VMEM((tm, rg_w), F32),
                        pltpu.VMEM((rq_w // LANES, LANES, LANES), F32),
                        pltpu.VMEM((dv_w, D), BF16)],
        compiler_params=pltpu.CompilerParams(
            dimension_semantics=("arbitrary",), vmem_limit_bytes=48 * 1024 * 1024),
        name="inproj",
    )(x2d, g, w, *tabs, *ret_consts, gr, *cast_weights)
    return outs[:4], outs[4:]


def _attn_body(*refs, tq, gw, lam_init, cast_blocks):
    n_cast = len(cast_blocks)
    lq1_ref, lk1_ref, lq2_ref, lk2_ref, g_ref, q_ref, k_ref, vt_ref = refs[:8]
    w_refs = refs[8:8 + n_cast]
    o_ref = refs[8 + n_cast]
    wb_refs = refs[9 + n_cast:9 + 2 * n_cast]
    (q2_ref, m_ref, acc_ref, sa_ref, sb_ref, mxa_ref, mxb_ref,
     kc_ref) = refs[9 + 2 * n_cast:]

    flat_step = ((pl.program_id(0) * pl.num_programs(1) + pl.program_id(1)) * pl.num_programs(2)
                 + pl.program_id(2))
    _cast_step(flat_step, w_refs, wb_refs, cast_blocks)

    i = pl.program_id(2)
    last_block = pl.num_programs(2) - 1
    tk = tq // 2
    mq = 2 * tq
    groups = [slice(g * gw, (g + 1) * gw) for g in range(mq // gw)]
    all_groups = tuple(range(len(groups)))
    lane = lax.broadcasted_iota(jnp.int32, (tq, LANES), 1)

    def load_queries(block):
        q = q_ref[pl.ds(pl.multiple_of(block * tq, tq), tq), :]
        zero = jnp.zeros_like(q)
        q2_ref[0:tq, :] = jnp.where(lane < DIFF_HEAD_DIM, q, zero)
        q2_ref[tq:mq, :] = jnp.where(lane >= DIFF_HEAD_DIM, q, zero)

    m_ref[...] = jnp.full(m_ref.shape, -jnp.inf, F32)
    acc_ref[...] = jnp.zeros(acc_ref.shape, F32)
    ones = jnp.ones((BF16_ROWS, tk), BF16)
    key_off = lax.broadcasted_iota(jnp.int32, (tk, gw), 0)
    qry_off = lax.broadcasted_iota(jnp.int32, (tk, gw), 1)

    def visibility(r, g):
        q0 = (g * gw) % tq
        k0 = r * tk
        if k0 + tk - 1 <= q0:
            return "all", None
        if k0 > q0 + gw - 1:
            return "none", None
        return "some", k0 - q0

    def stage_keys(j):
        kc_ref[...] = k_ref[pl.ds(pl.multiple_of(j * tk, tk), tk), :]

    def produce(g, s_ref, mx_ref):
        s = lax.dot_general(kc_ref[...], q2_ref[groups[g], :], NT_DIMS,
                            preferred_element_type=F32)
        s_ref[:, groups[g]] = s
        mx_ref[0:1, groups[g]] = jnp.max(s, axis=0, keepdims=True)

    def consume(va, g, s_ref, mx_ref, shift):
        cs = groups[g]
        m_old = m_ref[0:1, cs]
        if shift is None:
            s = s_ref[:, cs]
            m_new = jnp.maximum(m_old, mx_ref[0:1, cs])
        else:
            s = jnp.where(key_off + shift <= qry_off, s_ref[:, cs], -jnp.inf)
            m_new = jnp.maximum(m_old, jnp.max(s, axis=0, keepdims=True))
        alpha = jnp.exp2(m_old - m_new)
        p = jnp.exp2(s - m_new).astype(BF16)
        m_ref[0:1, cs] = m_new
        acc_ref[:, cs] = alpha * acc_ref[:, cs] + _dot(va, p)

    def step(j, cur, nxt, diag=None):
        va = jnp.concatenate([vt_ref[j], ones], axis=0)
        if diag != 1:
            stage_keys(j + 1)
        for g in all_groups:
            if diag is None or (diag == 0 and visibility(1, g)[0] != "none"):
                produce(g, *nxt)
            kind, shift = ("all", None) if diag is None else visibility(diag, g)
            if kind != "none":
                consume(va, g, *cur, shift=shift)

    buf_a = (sa_ref, mxa_ref)
    buf_b = (sb_ref, mxb_ref)

    @pl.when(i == 0)
    def _():
        load_queries(0)
        stage_keys(0)
        for g in all_groups:
            produce(g, *buf_a)

    def pair(t):
        step(2 * t, buf_a, buf_b)
        step(2 * t + 1, buf_b, buf_a)

    def two_pairs_body(u, carry):
        pair(2 * u)
        pair(2 * u + 1)
        return carry

    lax.fori_loop(0, i // 2, two_pairs_body, 0)

    @pl.when(i % 2 == 1)
    def _():
        pair(i - 1)

    step(2 * i, buf_a, buf_b, diag=0)

    def finish():
        lam = (jnp.exp(jnp.sum(lq1_ref[...] * lk1_ref[...], axis=-1, keepdims=True))
               - jnp.exp(jnp.sum(lq2_ref[...] * lk2_ref[...], axis=-1, keepdims=True))
               + lam_init)
        num = acc_ref[0:DIFF_V_DIM, :]
        den = acc_ref[DIFF_V_DIM:DIFF_V_DIM + 1, :]
        o_t = num[:, :tq] / den[:, :tq] - lam * (num[:, tq:] / den[:, tq:])
        o = o_t.T
        o_ref[...] = (_rms(o, g_ref[...], HEAD_NORM_EPS) * (1.0 - lam_init)).astype(o_ref.dtype)

    @pl.when(i < last_block)
    def _():
        step(2 * i + 1, buf_b, buf_a, diag=1)
        load_queries(i + 1)
        stage_keys(0)
        for g in all_groups:
            produce(g, *buf_a)
        finish()

    @pl.when(i == last_block)
    def _():
        step(2 * i + 1, buf_b, buf_a, diag=1)
        finish()


def _diffattn(lam_vecs, g, q, k, vt, cast_weights, tq, gw, lam_init):
    B, S, W = q.shape
    tk = vt.shape[3]
    assert tq == 2 * tk and tq % gw == 0
    mq = 2 * tq
    n_heads, n_qblocks = W // DIFF_V_DIM, S // tq
    steps = B * n_heads * n_qblocks
    small = lambda a: pl.BlockSpec(a.shape, lambda b, h, i: (0, 0))
    cast_specs, cast_shapes, cast_blocks = _cast_plan(
        cast_weights, steps, lambda b, h, i: (b * n_heads + h) * n_qblocks + i)
    outs = pl.pallas_call(
        functools.partial(_attn_body, tq=tq, gw=gw, lam_init=lam_init, cast_blocks=cast_blocks),
        grid=(B, n_heads, n_qblocks),
        in_specs=[small(a) for a in lam_vecs] + [
                  small(g),
                  pl.BlockSpec((None, S, LANES), lambda b, h, i: (b, 0, h)),
                  pl.BlockSpec((None, S, LANES), lambda b, h, i: (b, 0, h)),
                  pl.BlockSpec((None, S // tk, DIFF_V_DIM, tk), lambda b, h, i: (b, 0, h, 0))]
                 + cast_specs,
        out_specs=[pl.BlockSpec((None, tq, LANES), lambda b, h, i: (b, i, h))] + cast_specs,
        out_shape=[jax.ShapeDtypeStruct((B, S, W), BF16)] + cast_shapes,
        scratch_shapes=[pltpu.VMEM((mq, LANES), BF16),
                        pltpu.VMEM((8, mq), F32),
                        pltpu.VMEM((DIFF_V_DIM + BF16_ROWS, mq), F32),
                        pltpu.VMEM((tk, mq), F32), pltpu.VMEM((tk, mq), F32),
                        pltpu.VMEM((8, mq), F32), pltpu.VMEM((8, mq), F32),
                        pltpu.VMEM((tk, LANES), BF16)],
        compiler_params=pltpu.CompilerParams(
            dimension_semantics=("arbitrary", "arbitrary", "arbitrary"),
            vmem_limit_bytes=48 * 1024 * 1024),
        name="diffattn",
    )(*lam_vecs, g, q, k, vt, *cast_weights)
    return outs[0], outs[1:]


def _retention_consts(n_heads, chunk):
    log_g = np.log1p(-(2.0 ** (-5.0 - np.arange(n_heads, dtype=np.float64))))
    n = np.arange(chunk, dtype=np.float64)
    rel = n[:, None] - n[None, :]
    dmat = np.where(rel >= 0, np.exp(log_g[:, None, None] * np.maximum(rel, 0.0)), 0.0)
    qdec = np.exp(log_g[:, None] * (n + 1.0)[None, :])
    kdec = np.exp(log_g[:, None] * (chunk - 1.0 - n)[None, :])
    cdec = np.exp(log_g * chunk)
    bl = lambda a: np.ascontiguousarray(np.broadcast_to(a[:, :, None], a.shape + (LANES,)))
    cdec_rows = np.repeat(cdec, RET_KEY_DIM).reshape(n_heads // 2, LANES)
    return tuple(jnp.asarray(a, F32) for a in (dmat, bl(qdec), bl(kdec), bl(cdec_rows)))


def _outffn_body(x_ref, a_ref, r_ref, wo_ref, g2_ref, wg_ref, wu_ref, wd_ref, gf_ref, o_ref):
    mix = jnp.concatenate([a_ref[...], r_ref[...]], axis=1)
    x1 = x_ref[...] + _dot(mix, wo_ref[...])
    h = _rms(x1, g2_ref[...], NORM_EPS).astype(BF16)
    gate = _dot(h, wg_ref[...])
    up = _dot(h, wu_ref[...])
    act = (gate * jax.nn.sigmoid(gate) * up).astype(BF16)
    x2 = x1 + _dot(act, wd_ref[...])
    o_ref[...] = _rms(x2, gf_ref[...], NORM_EPS)


def _outffn(x2d, a, r, wo, g2, wg, wu, wd, gf, tm):
    T, D = x2d.shape
    row = lambda t: (t, 0)
    const = lambda t: (0, 0)
    resident = lambda arr: pl.BlockSpec(arr.shape, const, pipeline_mode=pl.Buffered(1))
    return pl.pallas_call(
        _outffn_body,
        grid=(T // tm,),
        in_specs=[pl.BlockSpec((tm, D), row),
                  pl.BlockSpec((tm, a.shape[1]), row),
                  pl.BlockSpec((tm, r.shape[1]), row),
                  resident(wo), pl.BlockSpec(g2.shape, const),
                  resident(wg), resident(wu), resident(wd),
                  pl.BlockSpec(gf.shape, const)],
        out_specs=pl.BlockSpec((tm, D), row),
        out_shape=jax.ShapeDtypeStruct((T, D), F32),
        compiler_params=pltpu.CompilerParams(
            dimension_semantics=("arbitrary",), vmem_limit_bytes=56 * 1024 * 1024),
        name="outffn",
    )(x2d, a, r, wo, g2, wg, wu, wd, gf)


def _rotation_tables(seq, tm):
    d = DIFF_HEAD_DIM
    inv_h = ROPE_THETA ** (-np.arange(0, d, 2, dtype=np.float64) / d)
    lanes_h = np.tile(np.concatenate([inv_h, inv_h]), LANES // d)
    dr = RET_KEY_DIM
    inv_r = 1.0 / (RET_THETA ** np.linspace(0.0, 1.0, dr // 2, dtype=np.float64))
    lanes_r = np.tile(np.repeat(inv_r, 2), LANES // dr)

    def cos_sin(pos):
        out = []
        for freq in (lanes_h, lanes_r):
            ang = pos[:, None] * freq[None, :]
            out += [np.cos(ang), np.sin(ang)]
        return jnp.asarray(np.stack(out), F32)

    starts = np.arange(seq // tm, dtype=np.float64) * tm
    return cos_sin(starts), cos_sin(np.arange(tm, dtype=np.float64))


def kernel(x, attn_norm_g, w_in, lambda_q1, lambda_k1, lambda_q2, lambda_k2, diff_subln_g,
           ret_norm_g, w_out, ffn_norm_g, w_gate, w_up, w_down, final_norm_g):
    B, S, D = x.shape
    assert w_in.shape[0] == 1, "single-layer block only"
    l = 0
    diff_w = D // 2
    ret_w = D - diff_w
    ret_heads = ret_w // RET_V_DIM
    widths = (diff_w, diff_w, diff_w, ret_heads * RET_KEY_DIM, ret_heads * RET_KEY_DIM, ret_w, ret_w)
    tm_in = 512
    tabs = _rotation_tables(S, tm_in)
    ret_chunk = 128
    ret_consts = _retention_consts(ret_heads, ret_chunk)
    xs = x.reshape(B * S, D)
    lam_init = 0.8 - 0.6 * math.exp(-0.3 * l)
    later_weights = [w_out[l], w_gate[l], w_up[l], w_down[l]]
    (q, k, vt, r), (wo_b, wg_b, wu_b, wd_b) = _inproj(
        xs, attn_norm_g[l][None, :], w_in[l].astype(BF16), tabs, ret_consts, ret_norm_g[l][None, :],
        later_weights, S, widths, tm=tm_in, tk=512, chunk=ret_chunk)
    tok3 = lambda t: t.reshape(B, S, t.shape[-1])
    lam_vecs = [v[l:l + 1].astype(F32) for v in (lambda_q1, lambda_k1, lambda_q2, lambda_k2)]
    a, _ = _diffattn(lam_vecs, diff_subln_g[l][None, :], tok3(q), tok3(k), vt, [],
                     tq=1024, gw=256, lam_init=lam_init)
    out = _outffn(xs, a.reshape(B * S, diff_w), r, wo_b, ffn_norm_g[l][None, :], wg_b, wu_b, wd_b,
                  final_norm_g[None, :], tm=512)
    return out.reshape(B, S, D)
```

```python
import functools
import math

import jax
import jax.numpy as jnp
import numpy as np
from jax import lax
from jax.experimental import pallas as pl
from jax.experimental.pallas import tpu as pltpu

F32 = jnp.float32
BF16 = jnp.bfloat16

DIFF_HEAD_DIM = 64
DIFF_V_DIM = 128
RET_KEY_DIM = 64
RET_V_DIM = 128
ROPE_THETA = 10000.0
RET_THETA = 10000.0
NORM_EPS = 1e-6
HEAD_NORM_EPS = 1e-5
LANES = 128
BF16_ROWS = 16
LOG2E = math.log2(math.e)

NT_DIMS = (((1,), (1,)), ((), ()))
TN_DIMS = (((0,), (0,)), ((), ()))


def _dot(a, b):
    return jnp.dot(a, b, preferred_element_type=F32)


def _rms(x, g, eps):
    ms = jnp.mean(x * x, axis=-1, keepdims=True)
    return x * lax.rsqrt(ms + eps) * g


P_ATTN_G, P_FFN_G, P_FINAL_G, P_RET_G, P_DIFF = 0, 1, 2, 3, 4
P_ROWS = 8
P_DIFF_SLOTS = ("subln_g", "lambda_q1", "lambda_k1", "lambda_q2", "lambda_k2")


def _pack_params(d_model, attn_g, ffn_g, final_g, ret_g, diff_vectors):
    def padded(v):
        return jnp.pad(v.astype(F32), (0, d_model - v.shape[0]))
    diff_row = jnp.concatenate([jnp.pad(v.astype(F32), (0, LANES - v.shape[0])) for v in diff_vectors])
    rows = [padded(attn_g), padded(ffn_g), padded(final_g), padded(ret_g), padded(diff_row)]
    rows += [jnp.zeros((d_model,), F32)] * (P_ROWS - len(rows))
    return jnp.stack(rows)


def _param_row(p_ref, row, width=None, lane0=0):
    width = p_ref.shape[1] if width is None else width
    return p_ref[row:row + 1, lane0:lane0 + width]


def _diff_param(p_ref, name, width):
    return _param_row(p_ref, P_DIFF, width, P_DIFF_SLOTS.index(name) * LANES)


def _cast_plan(weights, steps, step_of):
    specs, shapes, n_blocks = [], [], []
    for w in weights:
        rows, cols = w.shape
        rb = BF16_ROWS
        while rows % rb or rows // rb > steps:
            rb += BF16_ROWS
        nb = rows // rb
        specs.append(pl.BlockSpec(
            (rb, cols), lambda *idx, nb=nb: (jnp.minimum(step_of(*idx), nb - 1), 0)))
        shapes.append(jax.ShapeDtypeStruct(w.shape, BF16))
        n_blocks.append(nb)
    return specs, shapes, tuple(n_blocks)


def _cast_step(step, w_refs, wb_refs, n_blocks):
    for w_ref, wb_ref, nb in zip(w_refs, wb_refs, n_blocks):
        @pl.when(step < nb)
        def _(w_ref=w_ref, wb_ref=wb_ref):
            wb_ref[...] = w_ref[...].astype(wb_ref.dtype)


def _retention_tile(rq_ref, rk_ref, rv_ref, rg_ref, o_ref, state_ref,
                    dmat_ref, qdec_ref, kdec_ref, cdec_ref, gain, chunk):
    n_chunks = rq_ref.shape[0] // chunk
    n_pairs = rq_ref.shape[1] // LANES
    lane = lax.broadcasted_iota(jnp.int32, (chunk, LANES), 1)
    row = lax.broadcasted_iota(jnp.int32, (LANES, LANES), 0)
    rows = [slice(c * chunk, (c + 1) * chunk) for c in range(n_chunks)]
    head_lanes = [lane < RET_KEY_DIM, lane >= RET_KEY_DIM]

    def q_masked(c, p, hh):
        qp = rq_ref[rows[c], p * LANES:(p + 1) * LANES]
        return jnp.where(head_lanes[hh], qp, jnp.zeros_like(qp))

    scores, updates = {}, {}
    for c in range(n_chunks):
        for p in range(n_pairs):
            kp = rk_ref[rows[c], p * LANES:(p + 1) * LANES]
            for hh in range(2):
                h = 2 * p + hh
                vh = rv_ref[rows[c], h * RET_V_DIM:(h + 1) * RET_V_DIM]
                s = lax.dot_general(q_masked(c, p, hh), kp, NT_DIMS, preferred_element_type=F32)
                scores[c, h] = (s * dmat_ref[h]).astype(BF16)
                vd = (vh.astype(F32) * kdec_ref[h]).astype(BF16)
                updates[c, h] = lax.dot_general(kp, vd, TN_DIMS, preferred_element_type=F32)

    states = {}
    for p in range(n_pairs):
        st = state_ref[p]
        for c in range(n_chunks):
            states[c, p] = st.astype(BF16)
            st = cdec_ref[p] * st + jnp.where(row < RET_KEY_DIM, updates[c, 2 * p], updates[c, 2 * p + 1])
        state_ref[p] = st

    yield

    for c in range(n_chunks):
        for p in range(n_pairs):
            for hh in range(2):
                h = 2 * p + hh
                hs = slice(h * RET_V_DIM, (h + 1) * RET_V_DIM)
                cross = _dot(q_masked(c, p, hh), states[c, p]) * qdec_ref[h]
                o = _dot(scores[c, h], rv_ref[rows[c], hs]) + cross
                mu = jnp.mean(o, axis=-1, keepdims=True)
                cen = o - mu
                var = jnp.mean(cen * cen, axis=-1, keepdims=True)
                y = cen * lax.rsqrt(var + HEAD_NORM_EPS) * gain[:, hs]
                gate = rg_ref[rows[c], hs]
                o_ref[rows[c], hs] = (gate * jax.nn.sigmoid(gate) * y).astype(o_ref.dtype)


def _inproj_body(x_ref, p_ref, w_ref, base_ref, off_ref,
                 dmat_ref, qdec_ref, kdec_ref, cdec_ref, *refs,
                 widths, tiles_per_seq, chunk, cast_blocks):
    n_cast = len(cast_blocks)
    cast_in, refs = refs[:n_cast], refs[n_cast:]
    (q_ref, k_ref, vt_ref, r_ref), refs = refs[:4], refs[4:]
    cast_out, refs = refs[:n_cast], refs[n_cast:]
    rq_ref, rk_ref, rv_ref, rg_ref, state_ref, wvt_ref = refs
    dq_w, dk_w, dv_w, rq_w, rk_w, rv_w, rg_w = widths
    tile_in_seq = pl.program_id(0) % tiles_per_seq
    _cast_step(pl.program_id(0), cast_in, cast_out, cast_blocks)

    @pl.when(pl.program_id(0) == 0)
    def _():
        wv = w_ref[:, dq_w + dk_w:dq_w + dk_w + dv_w]
        wvt_ref[...] = wv.astype(F32).T.astype(wvt_ref.dtype)

    @pl.when(tile_in_seq == 0)
    def _():
        state_ref[...] = jnp.zeros(state_ref.shape, F32)

    h = _rms(x_ref[...], _param_row(p_ref, P_ATTN_G), NORM_EPS).astype(BF16)
    lane = lax.broadcasted_iota(jnp.int32, off_ref.shape[1:], 1)
    first_half = (lane & (DIFF_HEAD_DIM // 2)) == 0
    even = (lane & 1) == 0

    def angle_tables(kind, negate_where):
        cb = base_ref[2 * kind, pl.ds(tile_in_seq, 1), :]
        sb = base_ref[2 * kind + 1, pl.ds(tile_in_seq, 1), :]
        co, so = off_ref[2 * kind], off_ref[2 * kind + 1]
        sin = sb * co + cb * so
        return cb * co - sb * so, jnp.where(negate_where, -sin, sin)

    ch, sh = angle_tables(0, first_half)
    ci, si = angle_tables(1, even)

    def rope(xc):
        sw = jnp.where(first_half, pltpu.roll(xc, LANES - 32, 1), pltpu.roll(xc, 32, 1))
        return xc * ch + sw * sh

    def pair_rot(xc):
        sw = jnp.where(even, pltpu.roll(xc, LANES - 1, 1), pltpu.roll(xc, 1, 1))
        return xc * ci + sw * si

    def emit(out_ref, c0, width, fn):
        p = _dot(h, w_ref[:, c0:c0 + width])
        for j in range(width // LANES):
            sl = slice(j * LANES, (j + 1) * LANES)
            out_ref[:, sl] = fn(p[:, sl]).astype(out_ref.dtype)

    c_q, c_k = 0, dq_w
    c_rq = dq_w + dk_w + dv_w
    c_rk, c_rv, c_rg = c_rq + rq_w, c_rq + rq_w + rk_w, c_rq + rq_w + rk_w + rv_w
    emit(rq_ref, c_rq, rq_w, pair_rot)
    emit(rk_ref, c_rk, rk_w, lambda t: pair_rot(t) * (RET_KEY_DIM ** -0.5))
    emit(rv_ref, c_rv, rv_w, lambda t: t)
    emit(rg_ref, c_rg, rg_w, lambda t: t)
    retention = _retention_tile(rq_ref, rk_ref, rv_ref, rg_ref, r_ref, state_ref,
                                dmat_ref, qdec_ref, kdec_ref, cdec_ref,
                                _param_row(p_ref, P_RET_G, rv_w), chunk)
    next(retention)
    emit(q_ref, c_q, dq_w, lambda t: rope(t) * (DIFF_HEAD_DIM ** -0.5 * LOG2E))
    next(retention, None)
    emit(k_ref, c_k, dk_w, rope)
    vt = lax.dot_general(wvt_ref[...], h, NT_DIMS, preferred_element_type=F32)
    tkc = vt_ref.shape[-1]
    for j in range(vt_ref.shape[0]):
        vt_ref[j] = vt[:, j * tkc:(j + 1) * tkc].astype(vt_ref.dtype)


def _inproj(x2d, params, w, tabs, ret_consts, cast_weights, seq, widths, tm, tk, chunk):
    T, D = x2d.shape
    cast_specs, cast_shapes, cast_blocks = _cast_plan(cast_weights, T // tm, lambda t: t)
    P = w.shape[1]
    nps = seq // tm
    dq_w, dk_w, dv_w, rq_w, rk_w, rv_w, rg_w = widths
    row = lambda t: (t, 0)
    const = lambda t: (0, 0)
    full = lambda a: pl.BlockSpec(a.shape, lambda t: (0,) * a.ndim)
    out_shape = [jax.ShapeDtypeStruct((T, dq_w), BF16),
                 jax.ShapeDtypeStruct((T, dk_w), BF16),
                 jax.ShapeDtypeStruct((T // seq, seq // tk, dv_w, tk), BF16),
                 jax.ShapeDtypeStruct((T, rv_w), BF16)]
    out_specs = [pl.BlockSpec((tm, dq_w), row),
                 pl.BlockSpec((tm, dk_w), row),
                 pl.BlockSpec((None, tm // tk, dv_w, tk), lambda t: (t // nps, t % nps, 0, 0)),
                 pl.BlockSpec((tm, rv_w), row)]
    outs = pl.pallas_call(
        functools.partial(_inproj_body, widths=widths, tiles_per_seq=nps, chunk=chunk,
                          cast_blocks=cast_blocks),
        grid=(T // tm,),
        in_specs=[pl.BlockSpec((tm, D), row),
                  pl.BlockSpec(params.shape, const),
                  pl.BlockSpec((D, P), const, pipeline_mode=pl.Buffered(1))]
                 + [full(a) for a in tabs] + [full(a) for a in ret_consts]
                 + cast_specs,
        out_specs=out_specs + cast_specs,
        out_shape=out_shape + cast_shapes,
        scratch_shapes=[pltpu.VMEM((tm, rq_w), BF16), pltpu.VMEM((tm, rk_w), BF16),
                        pltpu.VMEM((tm, rv_w), BF16), pltpu.VMEM((tm, rg_w), F32),
                        pltpu.VMEM((rq_w // LANES, LANES, LANES), F32),
                        pltpu.VMEM((dv_w, D), BF16)],
        compiler_params=pltpu.CompilerParams(
            dimension_semantics=("arbitrary",), vmem_limit_bytes=48 * 1024 * 1024),
        name="inproj",
    )(x2d, params, w, *tabs, *ret_consts, *cast_weights)
    return outs[:4], outs[4:]


def _attn_body(p_ref, q_ref, k_ref, vt_ref, o_ref,
               q2_ref, m_ref, acc_ref, sa_ref, sb_ref, mxa_ref, mxb_ref, kc_ref,
               *, tq, gw, lam_init):
    i = pl.program_id(2)
    last_block = pl.num_programs(2) - 1
    tk = tq // 2
    mq = 2 * tq
    groups = [slice(g * gw, (g + 1) * gw) for g in range(mq // gw)]
    all_groups = tuple(range(len(groups)))
    lane = lax.broadcasted_iota(jnp.int32, (tq, LANES), 1)

    def load_queries(block):
        q = q_ref[pl.ds(pl.multiple_of(block * tq, tq), tq), :]
        zero = jnp.zeros_like(q)
        q2_ref[0:tq, :] = jnp.where(lane < DIFF_HEAD_DIM, q, zero)
        q2_ref[tq:mq, :] = jnp.where(lane >= DIFF_HEAD_DIM, q, zero)

    m_ref[...] = jnp.full(m_ref.shape, -jnp.inf, F32)
    acc_ref[...] = jnp.zeros(acc_ref.shape, F32)
    ones = jnp.ones((BF16_ROWS, tk), BF16)
    key_off = lax.broadcasted_iota(jnp.int32, (tk, gw), 0)
    qry_off = lax.broadcasted_iota(jnp.int32, (tk, gw), 1)

    def visibility(r, g):
        q0 = (g * gw) % tq
        k0 = r * tk
        if k0 + tk - 1 <= q0:
            return "all", None
        if k0 > q0 + gw - 1:
            return "none", None
        return "some", k0 - q0

    def stage_keys(j):
        kc_ref[...] = k_ref[pl.ds(pl.multiple_of(j * tk, tk), tk), :]

    def produce(g, s_ref, mx_ref):
        s = lax.dot_general(kc_ref[...], q2_ref[groups[g], :], NT_DIMS,
                            preferred_element_type=F32)
        s_ref[:, groups[g]] = s
        mx_ref[0:1, groups[g]] = jnp.max(s, axis=0, keepdims=True)

    def consume(va, g, s_ref, mx_ref, shift):
        cs = groups[g]
        m_old = m_ref[0:1, cs]
        if shift is None:
            s = s_ref[:, cs]
            m_new = jnp.maximum(m_old, mx_ref[0:1, cs])
        else:
            s = jnp.where(key_off + shift <= qry_off, s_ref[:, cs], -jnp.inf)
            m_new = jnp.maximum(m_old, jnp.max(s, axis=0, keepdims=True))
        alpha = jnp.exp2(m_old - m_new)
        p = jnp.exp2(s - m_new).astype(BF16)
        m_ref[0:1, cs] = m_new
        acc_ref[:, cs] = alpha * acc_ref[:, cs] + _dot(va, p)

    def step(j, cur, nxt, diag=None):
        va = jnp.concatenate([vt_ref[j], ones], axis=0)
        if diag != 1:
            stage_keys(j + 1)
        for g in all_groups:
            if diag is None or (diag == 0 and visibility(1, g)[0] != "none"):
                produce(g, *nxt)
            kind, shift = ("all", None) if diag is None else visibility(diag, g)
            if kind != "none":
                consume(va, g, *cur, shift=shift)

    buf_a = (sa_ref, mxa_ref)
    buf_b = (sb_ref, mxb_ref)

    @pl.when(i == 0)
    def _():
        load_queries(0)
        stage_keys(0)
        for g in all_groups:
            produce(g, *buf_a)

    def pair(t):
        step(2 * t, buf_a, buf_b)
        step(2 * t + 1, buf_b, buf_a)

    def two_pairs_body(u, carry):
        pair(2 * u)
        pair(2 * u + 1)
        return carry

    lax.fori_loop(0, i // 2, two_pairs_body, 0)

    @pl.when(i % 2 == 1)
    def _():
        pair(i - 1)

    step(2 * i, buf_a, buf_b, diag=0)

    def finish():
        lam_vec = lambda name: _diff_param(p_ref, name, DIFF_HEAD_DIM)
        lam = (jnp.exp(jnp.sum(lam_vec("lambda_q1") * lam_vec("lambda_k1"), axis=-1, keepdims=True))
               - jnp.exp(jnp.sum(lam_vec("lambda_q2") * lam_vec("lambda_k2"), axis=-1, keepdims=True))
               + lam_init)
        num = acc_ref[0:DIFF_V_DIM, :]
        den = acc_ref[DIFF_V_DIM:DIFF_V_DIM + 1, :]
        o_t = num[:, :tq] / den[:, :tq] - lam * (num[:, tq:] / den[:, tq:])
        o = o_t.T
        gain = _diff_param(p_ref, "subln_g", DIFF_V_DIM)
        o_ref[...] = (_rms(o, gain, HEAD_NORM_EPS) * (1.0 - lam_init)).astype(o_ref.dtype)

    @pl.when(i < last_block)
    def _():
        step(2 * i + 1, buf_b, buf_a, diag=1)
        load_queries(i + 1)
        stage_keys(0)
        for g in all_groups:
            produce(g, *buf_a)
        finish()

    @pl.when(i == last_block)
    def _():
        step(2 * i + 1, buf_b, buf_a, diag=1)
        finish()


def _diffattn(params, q, k, vt, tq, gw, lam_init):
    B, S, W = q.shape
    tk = vt.shape[3]
    assert tq == 2 * tk and tq % gw == 0
    mq = 2 * tq
    return pl.pallas_call(
        functools.partial(_attn_body, tq=tq, gw=gw, lam_init=lam_init),
        grid=(B, W // DIFF_V_DIM, S // tq),
        in_specs=[pl.BlockSpec(params.shape, lambda b, h, i: (0, 0)),
                  pl.BlockSpec((None, S, LANES), lambda b, h, i: (b, 0, h)),
                  pl.BlockSpec((None, S, LANES), lambda b, h, i: (b, 0, h)),
                  pl.BlockSpec((None, S // tk, DIFF_V_DIM, tk), lambda b, h, i: (b, 0, h, 0))],
        out_specs=pl.BlockSpec((None, tq, LANES), lambda b, h, i: (b, i, h)),
        out_shape=jax.ShapeDtypeStruct((B, S, W), BF16),
        scratch_shapes=[pltpu.VMEM((mq, LANES), BF16),
                        pltpu.VMEM((8, mq), F32),
                        pltpu.VMEM((DIFF_V_DIM + BF16_ROWS, mq), F32),
                        pltpu.VMEM((tk, mq), F32), pltpu.VMEM((tk, mq), F32),
                        pltpu.VMEM((8, mq), F32), pltpu.VMEM((8, mq), F32),
                        pltpu.VMEM((tk, LANES), BF16)],
        compiler_params=pltpu.CompilerParams(
            dimension_semantics=("arbitrary", "arbitrary", "arbitrary"),
            vmem_limit_bytes=48 * 1024 * 1024),
        name="diffattn",
    )(params, q, k, vt)


def _retention_consts(n_heads, chunk):
    log_g = np.log1p(-(2.0 ** (-5.0 - np.arange(n_heads, dtype=np.float64))))
    n = np.arange(chunk, dtype=np.float64)
    rel = n[:, None] - n[None, :]
    dmat = np.where(rel >= 0, np.exp(log_g[:, None, None] * np.maximum(rel, 0.0)), 0.0)
    qdec = np.exp(log_g[:, None] * (n + 1.0)[None, :])
    kdec = np.exp(log_g[:, None] * (chunk - 1.0 - n)[None, :])
    cdec = np.exp(log_g * chunk)
    bl = lambda a: np.ascontiguousarray(np.broadcast_to(a[:, :, None], a.shape + (LANES,)))
    cdec_rows = np.repeat(cdec, RET_KEY_DIM).reshape(n_heads // 2, LANES)
    return tuple(jnp.asarray(a, F32) for a in (dmat, bl(qdec), bl(kdec), bl(cdec_rows)))


def _outffn_body(x_ref, a_ref, r_ref, p_ref, wo_ref, wg_ref, wu_ref, wd_ref, o_ref):
    mix = jnp.concatenate([a_ref[...], r_ref[...]], axis=1)
    x1 = x_ref[...] + _dot(mix, wo_ref[...])
    h = _rms(x1, _param_row(p_ref, P_FFN_G), NORM_EPS).astype(BF16)
    gate = _dot(h, wg_ref[...])
    up = _dot(h, wu_ref[...])
    act = (gate * jax.nn.sigmoid(gate) * up).astype(BF16)
    x2 = x1 + _dot(act, wd_ref[...])
    o_ref[...] = _rms(x2, _param_row(p_ref, P_FINAL_G), NORM_EPS)


def _outffn(x2d, a, r, params, wo, wg, wu, wd, tm):
    T, D = x2d.shape
    row = lambda t: (t, 0)
    const = lambda t: (0, 0)
    resident = lambda arr: pl.BlockSpec(arr.shape, const, pipeline_mode=pl.Buffered(1))
    return pl.pallas_call(
        _outffn_body,
        grid=(T // tm,),
        in_specs=[pl.BlockSpec((tm, D), row),
                  pl.BlockSpec((tm, a.shape[1]), row),
                  pl.BlockSpec((tm, r.shape[1]), row),
                  pl.BlockSpec(params.shape, const),
                  resident(wo), resident(wg), resident(wu), resident(wd)],
        out_specs=pl.BlockSpec((tm, D), row),
        out_shape=jax.ShapeDtypeStruct((T, D), F32),
        compiler_params=pltpu.CompilerParams(
            dimension_semantics=("arbitrary",), vmem_limit_bytes=56 * 1024 * 1024),
        name="outffn",
    )(x2d, a, r, params, wo, wg, wu, wd)


def _rotation_tables(seq, tm):
    d = DIFF_HEAD_DIM
    inv_h = ROPE_THETA ** (-np.arange(0, d, 2, dtype=np.float64) / d)
    lanes_h = np.tile(np.concatenate([inv_h, inv_h]), LANES // d)
    dr = RET_KEY_DIM
    inv_r = 1.0 / (RET_THETA ** np.linspace(0.0, 1.0, dr // 2, dtype=np.float64))
    lanes_r = np.tile(np.repeat(inv_r, 2), LANES // dr)

    def cos_sin(pos):
        out = []
        for freq in (lanes_h, lanes_r):
            ang = pos[:, None] * freq[None, :]
            out += [np.cos(ang), np.sin(ang)]
        return jnp.asarray(np.stack(out), F32)

    starts = np.arange(seq // tm, dtype=np.float64) * tm
    return cos_sin(starts), cos_sin(np.arange(tm, dtype=np.float64))


def kernel(x, attn_norm_g, w_in, lambda_q1, lambda_k1, lambda_q2, lambda_k2, diff_subln_g,
           ret_norm_g, w_out, ffn_norm_g, w_gate, w_up, w_down, final_norm_g):
    B, S, D = x.shape
    assert w_in.shape[0] == 1, "single-layer block only"
    l = 0
    diff_w = D // 2
    ret_w = D - diff_w
    ret_heads = ret_w // RET_V_DIM
    widths = (diff_w, diff_w, diff_w, ret_heads * RET_KEY_DIM, ret_heads * RET_KEY_DIM, ret_w, ret_w)
    tm_in = 512
    tabs = _rotation_tables(S, tm_in)
    ret_chunk = 128
    ret_consts = _retention_consts(ret_heads, ret_chunk)
    xs = x.reshape(B * S, D)
    lam_init = 0.8 - 0.6 * math.exp(-0.3 * l)
    params = _pack_params(D, attn_norm_g[l], ffn_norm_g[l], final_norm_g, ret_norm_g[l],
                          [diff_subln_g[l], lambda_q1[l], lambda_k1[l], lambda_q2[l], lambda_k2[l]])
    later_weights = [w_out[l], w_gate[l], w_up[l], w_down[l]]
    (q, k, vt, r), (wo_b, wg_b, wu_b, wd_b) = _inproj(
        xs, params, w_in[l].astype(BF16), tabs, ret_consts, later_weights, S, widths,
        tm=tm_in, tk=512, chunk=ret_chunk)
    tok3 = lambda t: t.reshape(B, S, t.shape[-1])
    a = _diffattn(params, tok3(q), tok3(k), vt, tq=1024, gw=256, lam_init=lam_init)
    out = _outffn(xs, a.reshape(B * S, diff_w), r, params, wo_b, wg_b, wu_b, wd_b, tm=512)
    return out.reshape(B, S, D)
```

```python
import functools
import math

import jax
import jax.numpy as jnp
import numpy as np
from jax import lax
from jax.experimental import pallas as pl
from jax.experimental.pallas import tpu as pltpu

F32 = jnp.float32
BF16 = jnp.bfloat16

DIFF_HEAD_DIM = 64
DIFF_V_DIM = 128
RET_KEY_DIM = 64
RET_V_DIM = 128
ROPE_THETA = 10000.0
RET_THETA = 10000.0
NORM_EPS = 1e-6
HEAD_NORM_EPS = 1e-5
LANES = 128
BF16_ROWS = 16
LOG2E = math.log2(math.e)

NT_DIMS = (((1,), (1,)), ((), ()))
TN_DIMS = (((0,), (0,)), ((), ()))


def _dot(a, b):
    return jnp.dot(a, b, preferred_element_type=F32)


def _rms(x, g, eps):
    ms = jnp.mean(x * x, axis=-1, keepdims=True)
    return x * lax.rsqrt(ms + eps) * g


P_ATTN_G, P_FFN_G, P_FINAL_G, P_RET_G, P_DIFF = 0, 1, 2, 3, 4
P_ROWS = 8
P_DIFF_SLOTS = ("subln_g", "lambda_q1", "lambda_k1", "lambda_q2", "lambda_k2")


def _pack_params(d_model, attn_g, ffn_g, final_g, ret_g, diff_vectors):
    def padded(v):
        return jnp.pad(v.astype(F32), (0, d_model - v.shape[0]))
    diff_row = jnp.concatenate([jnp.pad(v.astype(F32), (0, LANES - v.shape[0])) for v in diff_vectors])
    rows = [padded(attn_g), padded(ffn_g), padded(final_g), padded(ret_g), padded(diff_row)]
    rows += [jnp.zeros((d_model,), F32)] * (P_ROWS - len(rows))
    return jnp.stack(rows)


def _param_row(p_ref, row, width=None, lane0=0):
    width = p_ref.shape[1] if width is None else width
    return p_ref[row:row + 1, lane0:lane0 + width]


def _diff_param(p_ref, name, width):
    return _param_row(p_ref, P_DIFF, width, P_DIFF_SLOTS.index(name) * LANES)


def _cast_plan(weights, steps, step_of):
    specs, shapes, n_blocks = [], [], []
    for w in weights:
        rows, cols = w.shape
        rb = BF16_ROWS
        while rows % rb or rows // rb > steps:
            rb += BF16_ROWS
        nb = rows // rb
        specs.append(pl.BlockSpec(
            (rb, cols), lambda *idx, nb=nb: (jnp.minimum(step_of(*idx), nb - 1), 0)))
        shapes.append(jax.ShapeDtypeStruct(w.shape, BF16))
        n_blocks.append(nb)
    return specs, shapes, tuple(n_blocks)


def _cast_step(step, w_refs, wb_refs, n_blocks):
    for w_ref, wb_ref, nb in zip(w_refs, wb_refs, n_blocks):
        @pl.when(step < nb)
        def _(w_ref=w_ref, wb_ref=wb_ref):
            wb_ref[...] = w_ref[...].astype(wb_ref.dtype)


def _retention_tile(rq_ref, rk_ref, rv_ref, rg_ref, o_ref, state_ref,
                    dmat_ref, qdec_ref, kdec_ref, cdec_ref, gain, chunk):
    n_chunks = rq_ref.shape[0] // chunk
    n_pairs = rq_ref.shape[1] // LANES
    lane = lax.broadcasted_iota(jnp.int32, (chunk, LANES), 1)
    row = lax.broadcasted_iota(jnp.int32, (LANES, LANES), 0)
    rows = [slice(c * chunk, (c + 1) * chunk) for c in range(n_chunks)]
    head_lanes = [lane < RET_KEY_DIM, lane >= RET_KEY_DIM]

    def q_masked(c, p, hh):
        qp = rq_ref[rows[c], p * LANES:(p + 1) * LANES]
        return jnp.where(head_lanes[hh], qp, jnp.zeros_like(qp))

    scores, updates = {}, {}
    for c in range(n_chunks):
        for p in range(n_pairs):
            kp = rk_ref[rows[c], p * LANES:(p + 1) * LANES]
            for hh in range(2):
                h = 2 * p + hh
                vh = rv_ref[rows[c], h * RET_V_DIM:(h + 1) * RET_V_DIM]
                s = lax.dot_general(q_masked(c, p, hh), kp, NT_DIMS, preferred_element_type=F32)
                scores[c, h] = (s * dmat_ref[h]).astype(BF16)
                vd = (vh.astype(F32) * kdec_ref[h]).astype(BF16)
                updates[c, h] = lax.dot_general(kp, vd, TN_DIMS, preferred_element_type=F32)

    states = {}
    for p in range(n_pairs):
        st = state_ref[p]
        for c in range(n_chunks):
            states[c, p] = st.astype(BF16)
            st = cdec_ref[p] * st + jnp.where(row < RET_KEY_DIM, updates[c, 2 * p], updates[c, 2 * p + 1])
        state_ref[p] = st

    yield

    for c in range(n_chunks):
        for p in range(n_pairs):
            for hh in range(2):
                h = 2 * p + hh
                hs = slice(h * RET_V_DIM, (h + 1) * RET_V_DIM)
                cross = _dot(q_masked(c, p, hh), states[c, p]) * qdec_ref[h]
                o = _dot(scores[c, h], rv_ref[rows[c], hs]) + cross
                mu = jnp.mean(o, axis=-1, keepdims=True)
                cen = o - mu
                var = jnp.mean(cen * cen, axis=-1, keepdims=True)
                y = cen * lax.rsqrt(var + HEAD_NORM_EPS) * gain[:, hs]
                gate = rg_ref[rows[c], hs]
                o_ref[rows[c], hs] = (gate * jax.nn.sigmoid(gate) * y).astype(o_ref.dtype)


def _inproj_body(x_ref, p_ref, w_ref, base_ref, off_ref,
                 dmat_ref, qdec_ref, kdec_ref, cdec_ref, *refs,
                 widths, tiles_per_seq, chunk, cast_blocks):
    n_cast = len(cast_blocks)
    cast_in, refs = refs[:n_cast], refs[n_cast:]
    (q_ref, k_ref, vt_ref, r_ref), refs = refs[:4], refs[4:]
    cast_out, refs = refs[:n_cast], refs[n_cast:]
    rq_ref, rk_ref, rv_ref, rg_ref, state_ref, wvt_ref = refs
    dq_w, dk_w, dv_w, rq_w, rk_w, rv_w, rg_w = widths
    tile_in_seq = pl.program_id(0) % tiles_per_seq
    _cast_step(pl.program_id(0), cast_in, cast_out, cast_blocks)

    @pl.when(pl.program_id(0) == 0)
    def _():
        wv = w_ref[:, dq_w + dk_w:dq_w + dk_w + dv_w]
        wvt_ref[...] = wv.astype(F32).T.astype(wvt_ref.dtype)

    @pl.when(tile_in_seq == 0)
    def _():
        state_ref[...] = jnp.zeros(state_ref.shape, F32)

    h = _rms(x_ref[...], _param_row(p_ref, P_ATTN_G), NORM_EPS).astype(BF16)
    lane = lax.broadcasted_iota(jnp.int32, off_ref.shape[1:], 1)
    first_half = (lane & (DIFF_HEAD_DIM // 2)) == 0
    even = (lane & 1) == 0

    def angle_tables(kind, negate_where):
        cb = base_ref[2 * kind, pl.ds(tile_in_seq, 1), :]
        sb = base_ref[2 * kind + 1, pl.ds(tile_in_seq, 1), :]
        co, so = off_ref[2 * kind], off_ref[2 * kind + 1]
        sin = sb * co + cb * so
        return cb * co - sb * so, jnp.where(negate_where, -sin, sin)

    ch, sh = angle_tables(0, first_half)
    ci, si = angle_tables(1, even)

    def rope(xc):
        sw = jnp.where(first_half, pltpu.roll(xc, LANES - 32, 1), pltpu.roll(xc, 32, 1))
        return xc * ch + sw * sh

    def pair_rot(xc):
        sw = jnp.where(even, pltpu.roll(xc, LANES - 1, 1), pltpu.roll(xc, 1, 1))
        return xc * ci + sw * si

    def emit(out_ref, c0, width, fn):
        p = _dot(h, w_ref[:, c0:c0 + width])
        for j in range(width // LANES):
            sl = slice(j * LANES, (j + 1) * LANES)
            out_ref[:, sl] = fn(p[:, sl]).astype(out_ref.dtype)

    c_q, c_k = 0, dq_w
    c_rq = dq_w + dk_w + dv_w
    c_rk, c_rv, c_rg = c_rq + rq_w, c_rq + rq_w + rk_w, c_rq + rq_w + rk_w + rv_w
    emit(rq_ref, c_rq, rq_w, pair_rot)
    emit(rk_ref, c_rk, rk_w, lambda t: pair_rot(t) * (RET_KEY_DIM ** -0.5))
    emit(rv_ref, c_rv, rv_w, lambda t: t)
    emit(rg_ref, c_rg, rg_w, lambda t: t)
    retention = _retention_tile(rq_ref, rk_ref, rv_ref, rg_ref, r_ref, state_ref,
                                dmat_ref, qdec_ref, kdec_ref, cdec_ref,
                                _param_row(p_ref, P_RET_G, rv_w), chunk)
    next(retention)
    emit(q_ref, c_q, dq_w, lambda t: rope(t) * (DIFF_HEAD_DIM ** -0.5 * LOG2E))
    next(retention, None)
    emit(k_ref, c_k, dk_w, rope)
    vt = lax.dot_general(wvt_ref[...], h, NT_DIMS, preferred_element_type=F32)
    tkc = vt_ref.shape[-1]
    for j in range(vt_ref.shape[0]):
        vt_ref[j] = vt[:, j * tkc:(j + 1) * tkc].astype(vt_ref.dtype)


def _inproj(x2d, params, w, tabs, ret_consts, cast_weights, seq, widths, tm, tk, chunk):
    T, D = x2d.shape
    cast_specs, cast_shapes, cast_blocks = _cast_plan(cast_weights, T // tm, lambda t: t)
    P = w.shape[1]
    nps = seq // tm
    dq_w, dk_w, dv_w, rq_w, rk_w, rv_w, rg_w = widths
    row = lambda t: (t, 0)
    const = lambda t: (0, 0)
    full = lambda a: pl.BlockSpec(a.shape, lambda t: (0,) * a.ndim)
    out_shape = [jax.ShapeDtypeStruct((T, dq_w), BF16),
                 jax.ShapeDtypeStruct((T, dk_w), BF16),
                 jax.ShapeDtypeStruct((T // seq, seq // tk, dv_w, tk), BF16),
                 jax.ShapeDtypeStruct((T, rv_w), BF16)]
    out_specs = [pl.BlockSpec((tm, dq_w), row),
                 pl.BlockSpec((tm, dk_w), row),
                 pl.BlockSpec((None, tm // tk, dv_w, tk), lambda t: (t // nps, t % nps, 0, 0)),
                 pl.BlockSpec((tm, rv_w), row)]
    outs = pl.pallas_call(
        functools.partial(_inproj_body, widths=widths, tiles_per_seq=nps, chunk=chunk,
                          cast_blocks=cast_blocks),
        grid=(T // tm,),
        in_specs=[pl.BlockSpec((tm, D), row),
                  pl.BlockSpec(params.shape, const),
                  pl.BlockSpec((D, P), const, pipeline_mode=pl.Buffered(1))]
                 + [full(a) for a in tabs] + [full(a) for a in ret_consts]
                 + cast_specs,
        out_specs=out_specs + cast_specs,
        out_shape=out_shape + cast_shapes,
        scratch_shapes=[pltpu.VMEM((tm, rq_w), BF16), pltpu.VMEM((tm, rk_w), BF16),
                        pltpu.VMEM((tm, rv_w), BF16), pltpu.VMEM((tm, rg_w), F32),
                        pltpu.VMEM((rq_w // LANES, LANES, LANES), F32),
                        pltpu.VMEM((dv_w, D), BF16)],
        compiler_params=pltpu.CompilerParams(
            dimension_semantics=("arbitrary",), vmem_limit_bytes=48 * 1024 * 1024),
        name="inproj",
    )(x2d, params, w, *tabs, *ret_consts, *cast_weights)
    return outs[:4], outs[4:]


def _attn_body(p_ref, q_ref, k_ref, vt_ref, o_ref,
               q2_ref, m_ref, acc_ref, sa_ref, sb_ref, mxa_ref, mxb_ref, kc_ref,
               *, tq, gw, lam_init):
    i = pl.program_id(2)
    last_block = pl.num_programs(2) - 1
    tk = tq // 2
    mq = 2 * tq
    groups = [slice(g * gw, (g + 1) * gw) for g in range(mq // gw)]
    all_groups = tuple(range(len(groups)))
    lane = lax.broadcasted_iota(jnp.int32, (tq, LANES), 1)

    def load_queries(block):
        q = q_ref[pl.ds(pl.multiple_of(block * tq, tq), tq), :]
        zero = jnp.zeros_like(q)
        q2_ref[0:tq, :] = jnp.where(lane < DIFF_HEAD_DIM, q, zero)
        q2_ref[tq:mq, :] = jnp.where(lane >= DIFF_HEAD_DIM, q, zero)

    m_ref[...] = jnp.full(m_ref.shape, -jnp.inf, F32)
    acc_ref[...] = jnp.zeros(acc_ref.shape, F32)

    def visibility(r, g):
        q0 = (g * gw) % tq
        k0 = r * tk
        if k0 + tk - 1 <= q0:
            return "all", None
        if k0 > q0 + gw - 1:
            return "none", None
        return "some", k0 - q0

    def stage_keys(j):
        kc_ref[...] = k_ref[pl.ds(pl.multiple_of(j * tk, tk), tk), :]

    def visible_rows(shift):
        return tk if shift is None else min(tk, gw - shift)

    def produce(g, s_ref, mx_ref, shift=None):
        nr = visible_rows(shift)
        s = lax.dot_general(kc_ref[0:nr, :], q2_ref[groups[g], :], NT_DIMS,
                            preferred_element_type=F32)
        s_ref[0:nr, groups[g]] = s
        if shift is None:
            mx_ref[0:1, groups[g]] = jnp.max(s, axis=0, keepdims=True)

    def values(j, nr):
        return jnp.concatenate([vt_ref[j, :, 0:nr], jnp.ones((BF16_ROWS, nr), BF16)], axis=0)

    def consume(j, va, g, s_ref, mx_ref, shift):
        cs = groups[g]
        nr = visible_rows(shift)
        m_old = m_ref[0:1, cs]
        if shift is None:
            s = s_ref[:, cs]
            m_new = jnp.maximum(m_old, mx_ref[0:1, cs])
        else:
            visible = (lax.broadcasted_iota(jnp.int32, (nr, gw), 0) + shift
                       <= lax.broadcasted_iota(jnp.int32, (nr, gw), 1))
            s = jnp.where(visible, s_ref[0:nr, cs], -jnp.inf)
            m_new = jnp.maximum(m_old, jnp.max(s, axis=0, keepdims=True))
            va = va if nr == tk else values(j, nr)
        alpha = jnp.exp2(m_old - m_new)
        p = jnp.exp2(s - m_new).astype(BF16)
        m_ref[0:1, cs] = m_new
        acc_ref[:, cs] = alpha * acc_ref[:, cs] + _dot(va, p)

    def step(j, cur, nxt, diag=None):
        va = values(j, tk)
        if diag != 1:
            stage_keys(j + 1)
        for g in all_groups:
            if diag is None:
                produce(g, *nxt)
            elif diag == 0 and visibility(1, g)[0] != "none":
                produce(g, *nxt, shift=visibility(1, g)[1])
            kind, shift = ("all", None) if diag is None else visibility(diag, g)
            if kind != "none":
                consume(j, va, g, *cur, shift=shift)

    buf_a = (sa_ref, mxa_ref)
    buf_b = (sb_ref, mxb_ref)

    @pl.when(i == 0)
    def _():
        load_queries(0)
        stage_keys(0)
        for g in all_groups:
            produce(g, *buf_a)

    def pair(t):
        step(2 * t, buf_a, buf_b)
        step(2 * t + 1, buf_b, buf_a)

    def two_pairs_body(u, carry):
        pair(2 * u)
        pair(2 * u + 1)
        return carry

    lax.fori_loop(0, i // 2, two_pairs_body, 0)

    @pl.when(i % 2 == 1)
    def _():
        pair(i - 1)

    step(2 * i, buf_a, buf_b, diag=0)

    def finish():
        lam_vec = lambda name: _diff_param(p_ref, name, DIFF_HEAD_DIM)
        lam = (jnp.exp(jnp.sum(lam_vec("lambda_q1") * lam_vec("lambda_k1"), axis=-1, keepdims=True))
               - jnp.exp(jnp.sum(lam_vec("lambda_q2") * lam_vec("lambda_k2"), axis=-1, keepdims=True))
               + lam_init)
        num = acc_ref[0:DIFF_V_DIM, :]
        den = acc_ref[DIFF_V_DIM:DIFF_V_DIM + 1, :]
        o_t = num[:, :tq] / den[:, :tq] - lam * (num[:, tq:] / den[:, tq:])
        o = o_t.T
        gain = _diff_param(p_ref, "subln_g", DIFF_V_DIM)
        o_ref[...] = (_rms(o, gain, HEAD_NORM_EPS) * (1.0 - lam_init)).astype(o_ref.dtype)

    @pl.when(i < last_block)
    def _():
        step(2 * i + 1, buf_b, buf_a, diag=1)
        load_queries(i + 1)
        stage_keys(0)
        for g in all_groups:
            produce(g, *buf_a)
        finish()

    @pl.when(i == last_block)
    def _():
        step(2 * i + 1, buf_b, buf_a, diag=1)
        finish()


def _diffattn(params, q, k, vt, tq, gw, lam_init):
    B, S, W = q.shape
    tk = vt.shape[3]
    assert tq == 2 * tk and tq % gw == 0
    mq = 2 * tq
    return pl.pallas_call(
        functools.partial(_attn_body, tq=tq, gw=gw, lam_init=lam_init),
        grid=(B, W // DIFF_V_DIM, S // tq),
        in_specs=[pl.BlockSpec(params.shape, lambda b, h, i: (0, 0)),
                  pl.BlockSpec((None, S, LANES), lambda b, h, i: (b, 0, h)),
                  pl.BlockSpec((None, S, LANES), lambda b, h, i: (b, 0, h)),
                  pl.BlockSpec((None, S // tk, DIFF_V_DIM, tk), lambda b, h, i: (b, 0, h, 0))],
        out_specs=pl.BlockSpec((None, tq, LANES), lambda b, h, i: (b, i, h)),
        out_shape=jax.ShapeDtypeStruct((B, S, W), BF16),
        scratch_shapes=[pltpu.VMEM((mq, LANES), BF16),
                        pltpu.VMEM((8, mq), F32),
                        pltpu.VMEM((DIFF_V_DIM + BF16_ROWS, mq), F32),
                        pltpu.VMEM((tk, mq), F32), pltpu.VMEM((tk, mq), F32),
                        pltpu.VMEM((8, mq), F32), pltpu.VMEM((8, mq), F32),
                        pltpu.VMEM((tk, LANES), BF16)],
        compiler_params=pltpu.CompilerParams(
            dimension_semantics=("arbitrary", "arbitrary", "arbitrary"),
            vmem_limit_bytes=48 * 1024 * 1024),
        name="diffattn",
    )(params, q, k, vt)


def _retention_consts(n_heads, chunk):
    log_g = np.log1p(-(2.0 ** (-5.0 - np.arange(n_heads, dtype=np.float64))))
    n = np.arange(chunk, dtype=np.float64)
    rel = n[:, None] - n[None, :]
    dmat = np.where(rel >= 0, np.exp(log_g[:, None, None] * np.maximum(rel, 0.0)), 0.0)
    qdec = np.exp(log_g[:, None] * (n + 1.0)[None, :])
    kdec = np.exp(log_g[:, None] * (chunk - 1.0 - n)[None, :])
    cdec = np.exp(log_g * chunk)
    bl = lambda a: np.ascontiguousarray(np.broadcast_to(a[:, :, None], a.shape + (LANES,)))
    cdec_rows = np.repeat(cdec, RET_KEY_DIM).reshape(n_heads // 2, LANES)
    return tuple(jnp.asarray(a, F32) for a in (dmat, bl(qdec), bl(kdec), bl(cdec_rows)))


def _outffn_body(x_ref, a_ref, r_ref, p_ref, wo_ref, wg_ref, wu_ref, wd_ref, o_ref):
    mix = jnp.concatenate([a_ref[...], r_ref[...]], axis=1)
    x1 = x_ref[...] + _dot(mix, wo_ref[...])
    h = _rms(x1, _param_row(p_ref, P_FFN_G), NORM_EPS).astype(BF16)
    gate = _dot(h, wg_ref[...])
    up = _dot(h, wu_ref[...])
    act = (gate * jax.nn.sigmoid(gate) * up).astype(BF16)
    x2 = x1 + _dot(act, wd_ref[...])
    o_ref[...] = _rms(x2, _param_row(p_ref, P_FINAL_G), NORM_EPS)


def _outffn(x2d, a, r, params, wo, wg, wu, wd, tm):
    T, D = x2d.shape
    row = lambda t: (t, 0)
    const = lambda t: (0, 0)
    resident = lambda arr: pl.BlockSpec(arr.shape, const, pipeline_mode=pl.Buffered(1))
    return pl.pallas_call(
        _outffn_body,
        grid=(T // tm,),
        in_specs=[pl.BlockSpec((tm, D), row),
                  pl.BlockSpec((tm, a.shape[1]), row),
                  pl.BlockSpec((tm, r.shape[1]), row),
                  pl.BlockSpec(params.shape, const),
                  resident(wo), resident(wg), resident(wu), resident(wd)],
        out_specs=pl.BlockSpec((tm, D), row),
        out_shape=jax.ShapeDtypeStruct((T, D), F32),
        compiler_params=pltpu.CompilerParams(
            dimension_semantics=("arbitrary",), vmem_limit_bytes=56 * 1024 * 1024),
        name="outffn",
    )(x2d, a, r, params, wo, wg, wu, wd)


def _rotation_tables(seq, tm):
    d = DIFF_HEAD_DIM
    inv_h = ROPE_THETA ** (-np.arange(0, d, 2, dtype=np.float64) / d)
    lanes_h = np.tile(np.concatenate([inv_h, inv_h]), LANES // d)
    dr = RET_KEY_DIM
    inv_r = 1.0 / (RET_THETA ** np.linspace(0.0, 1.0, dr // 2, dtype=np.float64))
    lanes_r = np.tile(np.repeat(inv_r, 2), LANES // dr)

    def cos_sin(pos):
        out = []
        for freq in (lanes_h, lanes_r):
            ang = pos[:, None] * freq[None, :]
            out += [np.cos(ang), np.sin(ang)]
        return jnp.asarray(np.stack(out), F32)

    starts = np.arange(seq // tm, dtype=np.float64) * tm
    return cos_sin(starts), cos_sin(np.arange(tm, dtype=np.float64))


def kernel(x, attn_norm_g, w_in, lambda_q1, lambda_k1, lambda_q2, lambda_k2, diff_subln_g,
           ret_norm_g, w_out, ffn_norm_g, w_gate, w_up, w_down, final_norm_g):
    B, S, D = x.shape
    assert w_in.shape[0] == 1, "single-layer block only"
    l = 0
    diff_w = D // 2
    ret_w = D - diff_w
    ret_heads = ret_w // RET_V_DIM
    widths = (diff_w, diff_w, diff_w, ret_heads * RET_KEY_DIM, ret_heads * RET_KEY_DIM, ret_w, ret_w)
    tm_in = 512
    tabs = _rotation_tables(S, tm_in)
    ret_chunk = 128
    ret_consts = _retention_consts(ret_heads, ret_chunk)
    xs = x.reshape(B * S, D)
    lam_init = 0.8 - 0.6 * math.exp(-0.3 * l)
    params = _pack_params(D, attn_norm_g[l], ffn_norm_g[l], final_norm_g, ret_norm_g[l],
                          [diff_subln_g[l], lambda_q1[l], lambda_k1[l], lambda_q2[l], lambda_k2[l]])
    later_weights = [w_out[l], w_gate[l], w_up[l], w_down[l]]
    (q, k, vt, r), (wo_b, wg_b, wu_b, wd_b) = _inproj(
        xs, params, w_in[l].astype(BF16), tabs, ret_consts, later_weights, S, widths,
        tm=tm_in, tk=512, chunk=ret_chunk)
    tok3 = lambda t: t.reshape(B, S, t.shape[-1])
    a = _diffattn(params, tok3(q), tok3(k), vt, tq=1024, gw=256, lam_init=lam_init)
    out = _outffn(xs, a.reshape(B * S, diff_w), r, params, wo_b, wg_b, wu_b, wd_b, tm=512)
    return out.reshape(B, S, D)
```

```python
import functools
import math

import jax
import jax.numpy as jnp
import numpy as np
from jax import lax
from jax.experimental import pallas as pl
from jax.experimental.pallas import tpu as pltpu

F32 = jnp.float32
BF16 = jnp.bfloat16

DIFF_HEAD_DIM = 64
DIFF_V_DIM = 128
RET_KEY_DIM = 64
RET_V_DIM = 128
ROPE_THETA = 10000.0
RET_THETA = 10000.0
NORM_EPS = 1e-6
HEAD_NORM_EPS = 1e-5
LANES = 128
BF16_ROWS = 16
LOG2E = math.log2(math.e)

NT_DIMS = (((1,), (1,)), ((), ()))
TN_DIMS = (((0,), (0,)), ((), ()))


def _dot(a, b):
    return jnp.dot(a, b, preferred_element_type=F32)


def _rms(x, g, eps):
    ms = jnp.mean(x * x, axis=-1, keepdims=True)
    return x * lax.rsqrt(ms + eps) * g


(P_ATTN_G, P_FFN_G, P_FINAL_G, P_RET_G, P_SUBLN_G,
 P_LAMBDA_Q1, P_LAMBDA_K1, P_LAMBDA_Q2, P_LAMBDA_K2) = range(9)
P_ROWS = 16


def _pack_params(d_model, vectors):
    pieces = []
    for v in vectors:
        pieces += [v.astype(F32), jnp.zeros((d_model - v.shape[0],), F32)]
    pieces.append(jnp.zeros(((P_ROWS - len(vectors)) * d_model,), F32))
    return jnp.concatenate(pieces).reshape(P_ROWS, d_model)


def _param_row(p_ref, row, width=None):
    width = p_ref.shape[1] if width is None else width
    return p_ref[row:row + 1, 0:width]


def _cast_plan(weights, steps, step_of):
    specs, shapes, n_blocks = [], [], []
    for w in weights:
        rows, cols = w.shape
        rb = BF16_ROWS
        while rows % rb or rows // rb > steps:
            rb += BF16_ROWS
        nb = rows // rb
        specs.append(pl.BlockSpec(
            (rb, cols), lambda *idx, nb=nb: (jnp.minimum(step_of(*idx), nb - 1), 0)))
        shapes.append(jax.ShapeDtypeStruct(w.shape, BF16))
        n_blocks.append(nb)
    return specs, shapes, tuple(n_blocks)


def _cast_step(step, w_refs, wb_refs, n_blocks):
    for w_ref, wb_ref, nb in zip(w_refs, wb_refs, n_blocks):
        @pl.when(step < nb)
        def _(w_ref=w_ref, wb_ref=wb_ref):
            wb_ref[...] = w_ref[...].astype(wb_ref.dtype)


def _retention_tile(rq_ref, rk_ref, rv_ref, rg_ref, o_ref, state_ref,
                    dmat_ref, qdec_ref, kdec_ref, cdec_ref, gain, chunk):
    n_chunks = rq_ref.shape[0] // chunk
    n_pairs = rq_ref.shape[1] // LANES
    lane = lax.broadcasted_iota(jnp.int32, (chunk, LANES), 1)
    row = lax.broadcasted_iota(jnp.int32, (LANES, LANES), 0)
    rows = [slice(c * chunk, (c + 1) * chunk) for c in range(n_chunks)]
    head_lanes = [lane < RET_KEY_DIM, lane >= RET_KEY_DIM]

    def q_masked(c, p, hh):
        qp = rq_ref[rows[c], p * LANES:(p + 1) * LANES]
        return jnp.where(head_lanes[hh], qp, jnp.zeros_like(qp))

    scores, updates = {}, {}
    for c in range(n_chunks):
        for p in range(n_pairs):
            kp = rk_ref[rows[c], p * LANES:(p + 1) * LANES]
            for hh in range(2):
                h = 2 * p + hh
                vh = rv_ref[rows[c], h * RET_V_DIM:(h + 1) * RET_V_DIM]
                s = lax.dot_general(q_masked(c, p, hh), kp, NT_DIMS, preferred_element_type=F32)
                scores[c, h] = (s * dmat_ref[h]).astype(BF16)
                vd = (vh.astype(F32) * kdec_ref[h]).astype(BF16)
                updates[c, h] = lax.dot_general(kp, vd, TN_DIMS, preferred_element_type=F32)

    states = {}
    for p in range(n_pairs):
        st = state_ref[p]
        for c in range(n_chunks):
            states[c, p] = st.astype(BF16)
            st = cdec_ref[p] * st + jnp.where(row < RET_KEY_DIM, updates[c, 2 * p], updates[c, 2 * p + 1])
        state_ref[p] = st

    yield

    for c in range(n_chunks):
        for p in range(n_pairs):
            for hh in range(2):
                h = 2 * p + hh
                hs = slice(h * RET_V_DIM, (h + 1) * RET_V_DIM)
                cross = _dot(q_masked(c, p, hh), states[c, p]) * qdec_ref[h]
                o = _dot(scores[c, h], rv_ref[rows[c], hs]) + cross
                mu = jnp.mean(o, axis=-1, keepdims=True)
                cen = o - mu
                var = jnp.mean(cen * cen, axis=-1, keepdims=True)
                y = cen * lax.rsqrt(var + HEAD_NORM_EPS) * gain[:, hs]
                gate = rg_ref[rows[c], hs]
                o_ref[rows[c], hs] = (gate * jax.nn.sigmoid(gate) * y).astype(o_ref.dtype)


def _inproj_body(x_ref, p_ref, w_ref, base_ref, off_ref,
                 dmat_ref, qdec_ref, kdec_ref, cdec_ref, *refs,
                 widths, tiles_per_seq, chunk, cast_blocks):
    n_cast = len(cast_blocks)
    cast_in, refs = refs[:n_cast], refs[n_cast:]
    (q_ref, k_ref, vt_ref, r_ref), refs = refs[:4], refs[4:]
    cast_out, refs = refs[:n_cast], refs[n_cast:]
    rq_ref, rk_ref, rv_ref, rg_ref, state_ref, wb_ref, wvt_ref = refs
    dq_w, dk_w, dv_w, rq_w, rk_w, rv_w, rg_w = widths
    tile_in_seq = pl.program_id(0) % tiles_per_seq
    _cast_step(pl.program_id(0), cast_in, cast_out, cast_blocks)

    @pl.when(pl.program_id(0) == 0)
    def _():
        c_v = dq_w + dk_w
        for c0 in range(0, w_ref.shape[1], dv_w):
            if c0 == c_v:
                wvt_ref[...] = w_ref[:, c0:c0 + dv_w].T.astype(wvt_ref.dtype)
            else:
                wb_ref[:, c0:c0 + dv_w] = w_ref[:, c0:c0 + dv_w].astype(wb_ref.dtype)

    @pl.when(tile_in_seq == 0)
    def _():
        state_ref[...] = jnp.zeros(state_ref.shape, F32)

    h = _rms(x_ref[...], _param_row(p_ref, P_ATTN_G), NORM_EPS).astype(BF16)
    lane = lax.broadcasted_iota(jnp.int32, off_ref.shape[1:], 1)
    first_half = (lane & (DIFF_HEAD_DIM // 2)) == 0
    even = (lane & 1) == 0

    def angle_tables(kind, negate_where):
        cb = base_ref[2 * kind, pl.ds(tile_in_seq, 1), :]
        sb = base_ref[2 * kind + 1, pl.ds(tile_in_seq, 1), :]
        co, so = off_ref[2 * kind], off_ref[2 * kind + 1]
        sin = sb * co + cb * so
        return cb * co - sb * so, jnp.where(negate_where, -sin, sin)

    ch, sh = angle_tables(0, first_half)
    ci, si = angle_tables(1, even)

    def rope(xc):
        sw = jnp.where(first_half, pltpu.roll(xc, LANES - 32, 1), pltpu.roll(xc, 32, 1))
        return xc * ch + sw * sh

    def pair_rot(xc):
        sw = jnp.where(even, pltpu.roll(xc, LANES - 1, 1), pltpu.roll(xc, 1, 1))
        return xc * ci + sw * si

    def emit(out_ref, c0, width, fn):
        p = _dot(h, wb_ref[:, c0:c0 + width])
        for j in range(width // LANES):
            sl = slice(j * LANES, (j + 1) * LANES)
            out_ref[:, sl] = fn(p[:, sl]).astype(out_ref.dtype)

    c_q, c_k = 0, dq_w
    c_rq = dq_w + dk_w + dv_w
    c_rk, c_rv, c_rg = c_rq + rq_w, c_rq + rq_w + rk_w, c_rq + rq_w + rk_w + rv_w
    emit(rq_ref, c_rq, rq_w, pair_rot)
    emit(rk_ref, c_rk, rk_w, lambda t: pair_rot(t) * (RET_KEY_DIM ** -0.5))
    emit(rv_ref, c_rv, rv_w, lambda t: t)
    emit(rg_ref, c_rg, rg_w, lambda t: t)
    retention = _retention_tile(rq_ref, rk_ref, rv_ref, rg_ref, r_ref, state_ref,
                                dmat_ref, qdec_ref, kdec_ref, cdec_ref,
                                _param_row(p_ref, P_RET_G, rv_w), chunk)
    next(retention)
    emit(q_ref, c_q, dq_w, lambda t: rope(t) * (DIFF_HEAD_DIM ** -0.5 * LOG2E))
    next(retention, None)
    emit(k_ref, c_k, dk_w, rope)
    vt = lax.dot_general(wvt_ref[...], h, NT_DIMS, preferred_element_type=F32)
    tkc = vt_ref.shape[-1]
    for j in range(vt_ref.shape[0]):
        vt_ref[j] = vt[:, j * tkc:(j + 1) * tkc].astype(vt_ref.dtype)


def _inproj(x2d, params, w, tabs, ret_consts, cast_weights, seq, widths, tm, tk, chunk):
    T, D = x2d.shape
    cast_specs, cast_shapes, cast_blocks = _cast_plan(cast_weights, T // tm, lambda t: t)
    P = w.shape[1]
    nps = seq // tm
    dq_w, dk_w, dv_w, rq_w, rk_w, rv_w, rg_w = widths
    row = lambda t: (t, 0)
    const = lambda t: (0, 0)
    full = lambda a: pl.BlockSpec(a.shape, lambda t: (0,) * a.ndim)
    out_shape = [jax.ShapeDtypeStruct((T, dq_w), BF16),
                 jax.ShapeDtypeStruct((T, dk_w), BF16),
                 jax.ShapeDtypeStruct((T // seq, seq // tk, dv_w, tk), BF16),
                 jax.ShapeDtypeStruct((T, rv_w), BF16)]
    out_specs = [pl.BlockSpec((tm, dq_w), row),
                 pl.BlockSpec((tm, dk_w), row),
                 pl.BlockSpec((None, tm // tk, dv_w, tk), lambda t: (t // nps, t % nps, 0, 0)),
                 pl.BlockSpec((tm, rv_w), row)]
    outs = pl.pallas_call(
        functools.partial(_inproj_body, widths=widths, tiles_per_seq=nps, chunk=chunk,
                          cast_blocks=cast_blocks),
        grid=(T // tm,),
        in_specs=[pl.BlockSpec((tm, D), row),
                  pl.BlockSpec(params.shape, const),
                  pl.BlockSpec((D, P), const, pipeline_mode=pl.Buffered(1))]
                 + [full(a) for a in tabs] + [full(a) for a in ret_consts]
                 + cast_specs,
        out_specs=out_specs + cast_specs,
        out_shape=out_shape + cast_shapes,
        scratch_shapes=[pltpu.VMEM((tm, rq_w), BF16), pltpu.VMEM((tm, rk_w), BF16),
                        pltpu.VMEM((tm, rv_w), BF16), pltpu.VMEM((tm, rg_w), F32),
                        pltpu.VMEM((rq_w // LANES, LANES, LANES), F32),
                        pltpu.VMEM((D, P), BF16), pltpu.VMEM((dv_w, D), BF16)],
        compiler_params=pltpu.CompilerParams(
            dimension_semantics=("arbitrary",), vmem_limit_bytes=56 * 1024 * 1024),
        name="inproj",
    )(x2d, params, w, *tabs, *ret_consts, *cast_weights)
    return outs[:4], outs[4:]


def _attn_body(p_ref, q_ref, k_ref, vt_ref, o_ref,
               q2_ref, m_ref, acc_ref, sa_ref, sb_ref, mxa_ref, mxb_ref, kc_ref,
               *, tq, gw, lam_init):
    i = pl.program_id(2)
    last_block = pl.num_programs(2) - 1
    tk = tq // 2
    mq = 2 * tq
    groups = [slice(g * gw, (g + 1) * gw) for g in range(mq // gw)]
    all_groups = tuple(range(len(groups)))
    lane = lax.broadcasted_iota(jnp.int32, (tq, LANES), 1)

    def load_queries(block):
        q = q_ref[pl.ds(pl.multiple_of(block * tq, tq), tq), :]
        zero = jnp.zeros_like(q)
        q2_ref[0:tq, :] = jnp.where(lane < DIFF_HEAD_DIM, q, zero)
        q2_ref[tq:mq, :] = jnp.where(lane >= DIFF_HEAD_DIM, q, zero)

    m_ref[...] = jnp.full(m_ref.shape, -jnp.inf, F32)
    acc_ref[...] = jnp.zeros(acc_ref.shape, F32)

    def visibility(r, g):
        q0 = (g * gw) % tq
        k0 = r * tk
        if k0 + tk - 1 <= q0:
            return "all", None
        if k0 > q0 + gw - 1:
            return "none", None
        return "some", k0 - q0

    def stage_keys(j):
        kc_ref[...] = k_ref[pl.ds(pl.multiple_of(j * tk, tk), tk), :]

    def visible_rows(shift):
        return tk if shift is None else min(tk, gw - shift)

    def produce(g, s_ref, mx_ref, shift=None):
        nr = visible_rows(shift)
        s = lax.dot_general(kc_ref[0:nr, :], q2_ref[groups[g], :], NT_DIMS,
                            preferred_element_type=F32)
        s_ref[0:nr, groups[g]] = s
        if shift is None:
            mx_ref[0:1, groups[g]] = jnp.max(s, axis=0, keepdims=True)

    def values(j, nr):
        return jnp.concatenate([vt_ref[j, :, 0:nr], jnp.ones((BF16_ROWS, nr), BF16)], axis=0)

    def consume(j, va, g, s_ref, mx_ref, shift):
        cs = groups[g]
        nr = visible_rows(shift)
        m_old = m_ref[0:1, cs]
        if shift is None:
            s = s_ref[:, cs]
            m_new = jnp.maximum(m_old, mx_ref[0:1, cs])
        else:
            visible = (lax.broadcasted_iota(jnp.int32, (nr, gw), 0) + shift
                       <= lax.broadcasted_iota(jnp.int32, (nr, gw), 1))
            s = jnp.where(visible, s_ref[0:nr, cs], -jnp.inf)
            m_new = jnp.maximum(m_old, jnp.max(s, axis=0, keepdims=True))
            va = va if nr == tk else values(j, nr)
        alpha = jnp.exp2(m_old - m_new)
        p = jnp.exp2(s - m_new).astype(BF16)
        m_ref[0:1, cs] = m_new
        acc_ref[:, cs] = alpha * acc_ref[:, cs] + _dot(va, p)

    def step(j, cur, nxt, diag=None):
        va = values(j, tk)
        if diag != 1:
            stage_keys(j + 1)
        for g in all_groups:
            if diag is None:
                produce(g, *nxt)
            elif diag == 0 and visibility(1, g)[0] != "none":
                produce(g, *nxt, shift=visibility(1, g)[1])
            kind, shift = ("all", None) if diag is None else visibility(diag, g)
            if kind != "none":
                consume(j, va, g, *cur, shift=shift)

    buf_a = (sa_ref, mxa_ref)
    buf_b = (sb_ref, mxb_ref)

    @pl.when(i == 0)
    def _():
        load_queries(0)
        stage_keys(0)
        for g in all_groups:
            produce(g, *buf_a)

    def pair(t):
        step(2 * t, buf_a, buf_b)
        step(2 * t + 1, buf_b, buf_a)

    def two_pairs_body(u, carry):
        pair(2 * u)
        pair(2 * u + 1)
        return carry

    lax.fori_loop(0, i // 2, two_pairs_body, 0)

    @pl.when(i % 2 == 1)
    def _():
        pair(i - 1)

    step(2 * i, buf_a, buf_b, diag=0)

    def finish():
        lam_vec = lambda row: _param_row(p_ref, row, DIFF_HEAD_DIM)
        lam = (jnp.exp(jnp.sum(lam_vec(P_LAMBDA_Q1) * lam_vec(P_LAMBDA_K1), axis=-1, keepdims=True))
               - jnp.exp(jnp.sum(lam_vec(P_LAMBDA_Q2) * lam_vec(P_LAMBDA_K2), axis=-1, keepdims=True))
               + lam_init)
        num = acc_ref[0:DIFF_V_DIM, :]
        den = acc_ref[DIFF_V_DIM:DIFF_V_DIM + 1, :]
        o_t = num[:, :tq] / den[:, :tq] - lam * (num[:, tq:] / den[:, tq:])
        o = o_t.T
        gain = _param_row(p_ref, P_SUBLN_G, DIFF_V_DIM)
        o_ref[...] = (_rms(o, gain, HEAD_NORM_EPS) * (1.0 - lam_init)).astype(o_ref.dtype)

    @pl.when(i < last_block)
    def _():
        step(2 * i + 1, buf_b, buf_a, diag=1)
        load_queries(i + 1)
        stage_keys(0)
        for g in all_groups:
            produce(g, *buf_a)
        finish()

    @pl.when(i == last_block)
    def _():
        step(2 * i + 1, buf_b, buf_a, diag=1)
        finish()


def _diffattn(params, q, k, vt, tq, gw, lam_init):
    B, S, W = q.shape
    tk = vt.shape[3]
    assert tq == 2 * tk and tq % gw == 0
    mq = 2 * tq
    return pl.pallas_call(
        functools.partial(_attn_body, tq=tq, gw=gw, lam_init=lam_init),
        grid=(B, W // DIFF_V_DIM, S // tq),
        in_specs=[pl.BlockSpec(params.shape, lambda b, h, i: (0, 0)),
                  pl.BlockSpec((None, S, LANES), lambda b, h, i: (b, 0, h)),
                  pl.BlockSpec((None, S, LANES), lambda b, h, i: (b, 0, h)),
                  pl.BlockSpec((None, S // tk, DIFF_V_DIM, tk), lambda b, h, i: (b, 0, h, 0))],
        out_specs=pl.BlockSpec((None, tq, LANES), lambda b, h, i: (b, i, h)),
        out_shape=jax.ShapeDtypeStruct((B, S, W), BF16),
        scratch_shapes=[pltpu.VMEM((mq, LANES), BF16),
                        pltpu.VMEM((8, mq), F32),
                        pltpu.VMEM((DIFF_V_DIM + BF16_ROWS, mq), F32),
                        pltpu.VMEM((tk, mq), F32), pltpu.VMEM((tk, mq), F32),
                        pltpu.VMEM((8, mq), F32), pltpu.VMEM((8, mq), F32),
                        pltpu.VMEM((tk, LANES), BF16)],
        compiler_params=pltpu.CompilerParams(
            dimension_semantics=("arbitrary", "arbitrary", "arbitrary"),
            vmem_limit_bytes=48 * 1024 * 1024),
        name="diffattn",
    )(params, q, k, vt)


def _retention_consts(n_heads, chunk):
    log_g = np.log1p(-(2.0 ** (-5.0 - np.arange(n_heads, dtype=np.float64))))
    n = np.arange(chunk, dtype=np.float64)
    rel = n[:, None] - n[None, :]
    dmat = np.where(rel >= 0, np.exp(log_g[:, None, None] * np.maximum(rel, 0.0)), 0.0)
    qdec = np.exp(log_g[:, None] * (n + 1.0)[None, :])
    kdec = np.exp(log_g[:, None] * (chunk - 1.0 - n)[None, :])
    cdec = np.exp(log_g * chunk)
    bl = lambda a: np.ascontiguousarray(np.broadcast_to(a[:, :, None], a.shape + (LANES,)))
    cdec_rows = np.repeat(cdec, RET_KEY_DIM).reshape(n_heads // 2, LANES)
    return tuple(jnp.asarray(a, F32) for a in (dmat, bl(qdec), bl(kdec), bl(cdec_rows)))


def _outffn_body(x_ref, a_ref, r_ref, p_ref, wo_ref, wg_ref, wu_ref, wd_ref, o_ref):
    mix = jnp.concatenate([a_ref[...], r_ref[...]], axis=1)
    x1 = x_ref[...] + _dot(mix, wo_ref[...])
    h = _rms(x1, _param_row(p_ref, P_FFN_G), NORM_EPS).astype(BF16)
    gate = _dot(h, wg_ref[...])
    up = _dot(h, wu_ref[...])
    act = (gate * jax.nn.sigmoid(gate) * up).astype(BF16)
    x2 = x1 + _dot(act, wd_ref[...])
    o_ref[...] = _rms(x2, _param_row(p_ref, P_FINAL_G), NORM_EPS)


def _outffn(x2d, a, r, params, wo, wg, wu, wd, tm):
    T, D = x2d.shape
    row = lambda t: (t, 0)
    const = lambda t: (0, 0)
    resident = lambda arr: pl.BlockSpec(arr.shape, const, pipeline_mode=pl.Buffered(1))
    return pl.pallas_call(
        _outffn_body,
        grid=(T // tm,),
        in_specs=[pl.BlockSpec((tm, D), row),
                  pl.BlockSpec((tm, a.shape[1]), row),
                  pl.BlockSpec((tm, r.shape[1]), row),
                  pl.BlockSpec(params.shape, const),
                  resident(wo), resident(wg), resident(wu), resident(wd)],
        out_specs=pl.BlockSpec((tm, D), row),
        out_shape=jax.ShapeDtypeStruct((T, D), F32),
        compiler_params=pltpu.CompilerParams(
            dimension_semantics=("arbitrary",), vmem_limit_bytes=56 * 1024 * 1024),
        name="outffn",
    )(x2d, a, r, params, wo, wg, wu, wd)


def _rotation_tables(seq, tm):
    d = DIFF_HEAD_DIM
    inv_h = ROPE_THETA ** (-np.arange(0, d, 2, dtype=np.float64) / d)
    lanes_h = np.tile(np.concatenate([inv_h, inv_h]), LANES // d)
    dr = RET_KEY_DIM
    inv_r = 1.0 / (RET_THETA ** np.linspace(0.0, 1.0, dr // 2, dtype=np.float64))
    lanes_r = np.tile(np.repeat(inv_r, 2), LANES // dr)

    def cos_sin(pos):
        out = []
        for freq in (lanes_h, lanes_r):
            ang = pos[:, None] * freq[None, :]
            out += [np.cos(ang), np.sin(ang)]
        return jnp.asarray(np.stack(out), F32)

    starts = np.arange(seq // tm, dtype=np.float64) * tm
    return cos_sin(starts), cos_sin(np.arange(tm, dtype=np.float64))


def kernel(x, attn_norm_g, w_in, lambda_q1, lambda_k1, lambda_q2, lambda_k2, diff_subln_g,
           ret_norm_g, w_out, ffn_norm_g, w_gate, w_up, w_down, final_norm_g):
    B, S, D = x.shape
    assert w_in.shape[0] == 1, "single-layer block only"
    l = 0
    diff_w = D // 2
    ret_w = D - diff_w
    ret_heads = ret_w // RET_V_DIM
    widths = (diff_w, diff_w, diff_w, ret_heads * RET_KEY_DIM, ret_heads * RET_KEY_DIM, ret_w, ret_w)
    tm_in = 512
    tabs = _rotation_tables(S, tm_in)
    ret_chunk = 128
    ret_consts = _retention_consts(ret_heads, ret_chunk)
    xs = x.reshape(B * S, D)
    lam_init = 0.8 - 0.6 * math.exp(-0.3 * l)
    params = _pack_params(D, [attn_norm_g[l], ffn_norm_g[l], final_norm_g, ret_norm_g[l],
                              diff_subln_g[l], lambda_q1[l], lambda_k1[l], lambda_q2[l], lambda_k2[l]])
    later_weights = [w_out[l], w_gate[l], w_up[l], w_down[l]]
    (q, k, vt, r), (wo_b, wg_b, wu_b, wd_b) = _inproj(
        xs, params, w_in[l], tabs, ret_consts, later_weights, S, widths,
        tm=tm_in, tk=512, chunk=ret_chunk)
    tok3 = lambda t: t.reshape(B, S, t.shape[-1])
    a = _diffattn(params, tok3(q), tok3(k), vt, tq=1024, gw=256, lam_init=lam_init)
    out = _outffn(xs, a.reshape(B * S, diff_w), r, params, wo_b, wg_b, wu_b, wd_b, tm=512)
    return out.reshape(B, S, D)
```

```python
import functools
import math

import jax
import jax.numpy as jnp
import numpy as np
from jax import lax
from jax.experimental import pallas as pl
from jax.experimental.pallas import tpu as pltpu

F32 = jnp.float32
BF16 = jnp.bfloat16

DIFF_HEAD_DIM = 64
DIFF_V_DIM = 128
RET_KEY_DIM = 64
RET_V_DIM = 128
ROPE_THETA = 10000.0
RET_THETA = 10000.0
NORM_EPS = 1e-6
HEAD_NORM_EPS = 1e-5
LANES = 128
BF16_ROWS = 16
LOG2E = math.log2(math.e)

NT_DIMS = (((1,), (1,)), ((), ()))
TN_DIMS = (((0,), (0,)), ((), ()))


def _dot(a, b):
    return jnp.dot(a, b, preferred_element_type=F32)


def _rms(x, g, eps):
    ms = jnp.mean(x * x, axis=-1, keepdims=True)
    return x * lax.rsqrt(ms + eps) * g


(P_ATTN_G, P_FFN_G, P_FINAL_G, P_RET_G, P_SUBLN_G,
 P_LAMBDA_Q1, P_LAMBDA_K1, P_LAMBDA_Q2, P_LAMBDA_K2) = range(9)
P_ROWS = 16


def _pack_params(d_model, vectors):
    pieces = []
    for v in vectors:
        pieces += [v.astype(F32), jnp.zeros((d_model - v.shape[0],), F32)]
    pieces.append(jnp.zeros(((P_ROWS - len(vectors)) * d_model,), F32))
    return jnp.concatenate(pieces).reshape(P_ROWS, d_model)


def _param_row(p_ref, row, width=None):
    width = p_ref.shape[1] if width is None else width
    return p_ref[row:row + 1, 0:width]


def _cast_plan(weights, steps, step_of):
    specs, shapes, n_blocks = [], [], []
    for w in weights:
        rows, cols = w.shape
        rb = BF16_ROWS
        while rows % rb or rows // rb > steps:
            rb += BF16_ROWS
        nb = rows // rb
        specs.append(pl.BlockSpec(
            (rb, cols), lambda *idx, nb=nb: (jnp.minimum(step_of(*idx), nb - 1), 0)))
        shapes.append(jax.ShapeDtypeStruct(w.shape, BF16))
        n_blocks.append(nb)
    return specs, shapes, tuple(n_blocks)


def _cast_step(step, w_refs, wb_refs, n_blocks):
    for w_ref, wb_ref, nb in zip(w_refs, wb_refs, n_blocks):
        @pl.when(step < nb)
        def _(w_ref=w_ref, wb_ref=wb_ref):
            wb_ref[...] = w_ref[...].astype(wb_ref.dtype)


def _retention_tile(rq_ref, rk_ref, rv_ref, rg_ref, o_ref, state_ref,
                    dmat_ref, qdec_ref, kdec_ref, cdec_ref, gain, chunk):
    n_chunks = rq_ref.shape[0] // chunk
    n_pairs = rq_ref.shape[1] // LANES
    lane = lax.broadcasted_iota(jnp.int32, (chunk, LANES), 1)
    row = lax.broadcasted_iota(jnp.int32, (LANES, LANES), 0)
    rows = [slice(c * chunk, (c + 1) * chunk) for c in range(n_chunks)]
    head_lanes = [lane < RET_KEY_DIM, lane >= RET_KEY_DIM]

    def q_masked(c, p, hh):
        qp = rq_ref[rows[c], p * LANES:(p + 1) * LANES]
        return jnp.where(head_lanes[hh], qp, jnp.zeros_like(qp))

    scores, updates = {}, {}
    for c in range(n_chunks):
        for p in range(n_pairs):
            kp = rk_ref[rows[c], p * LANES:(p + 1) * LANES]
            for hh in range(2):
                h = 2 * p + hh
                vh = rv_ref[rows[c], h * RET_V_DIM:(h + 1) * RET_V_DIM]
                s = lax.dot_general(q_masked(c, p, hh), kp, NT_DIMS, preferred_element_type=F32)
                scores[c, h] = (s * dmat_ref[h]).astype(BF16)
                vd = (vh.astype(F32) * kdec_ref[h]).astype(BF16)
                updates[c, h] = lax.dot_general(kp, vd, TN_DIMS, preferred_element_type=F32)

    states = {}
    for p in range(n_pairs):
        st = state_ref[p]
        for c in range(n_chunks):
            states[c, p] = st.astype(BF16)
            st = cdec_ref[p] * st + jnp.where(row < RET_KEY_DIM, updates[c, 2 * p], updates[c, 2 * p + 1])
        state_ref[p] = st

    yield

    for c in range(n_chunks):
        for p in range(n_pairs):
            for hh in range(2):
                h = 2 * p + hh
                hs = slice(h * RET_V_DIM, (h + 1) * RET_V_DIM)
                cross = _dot(q_masked(c, p, hh), states[c, p]) * qdec_ref[h]
                o = _dot(scores[c, h], rv_ref[rows[c], hs]) + cross
                mu = jnp.mean(o, axis=-1, keepdims=True)
                cen = o - mu
                var = jnp.mean(cen * cen, axis=-1, keepdims=True)
                y = cen * lax.rsqrt(var + HEAD_NORM_EPS) * gain[:, hs]
                gate = rg_ref[rows[c], hs]
                o_ref[rows[c], hs] = (gate * jax.nn.sigmoid(gate) * y).astype(o_ref.dtype)


def _inproj_body(x_ref, p_ref, w_ref, base_ref, off_ref,
                 dmat_ref, qdec_ref, kdec_ref, cdec_ref, *refs,
                 widths, tiles_per_seq, chunk, cast_blocks):
    n_cast = len(cast_blocks)
    cast_in, refs = refs[:n_cast], refs[n_cast:]
    (q_ref, k_ref, vt_ref, r_ref), refs = refs[:4], refs[4:]
    cast_out, refs = refs[:n_cast], refs[n_cast:]
    rq_ref, rk_ref, rv_ref, rg_ref, state_ref, wb_ref, wvt_ref = refs
    dq_w, dk_w, dv_w, rq_w, rk_w, rv_w, rg_w = widths
    tile_in_seq = pl.program_id(0) % tiles_per_seq
    _cast_step(pl.program_id(0), cast_in, cast_out, cast_blocks)

    @pl.when(pl.program_id(0) == 0)
    def _():
        c_v = dq_w + dk_w
        for c0 in range(0, w_ref.shape[1], dv_w):
            if c0 == c_v:
                wvt_ref[...] = w_ref[:, c0:c0 + dv_w].T.astype(wvt_ref.dtype)
            else:
                wb_ref[:, c0:c0 + dv_w] = w_ref[:, c0:c0 + dv_w].astype(wb_ref.dtype)

    @pl.when(tile_in_seq == 0)
    def _():
        state_ref[...] = jnp.zeros(state_ref.shape, F32)

    h = _rms(x_ref[...], _param_row(p_ref, P_ATTN_G), NORM_EPS).astype(BF16)
    lane = lax.broadcasted_iota(jnp.int32, off_ref.shape[1:], 1)
    first_half = (lane & (DIFF_HEAD_DIM // 2)) == 0
    even = (lane & 1) == 0

    def angle_tables(kind, negate_where):
        cb = base_ref[2 * kind, pl.ds(tile_in_seq, 1), :]
        sb = base_ref[2 * kind + 1, pl.ds(tile_in_seq, 1), :]
        co, so = off_ref[2 * kind], off_ref[2 * kind + 1]
        sin = sb * co + cb * so
        return cb * co - sb * so, jnp.where(negate_where, -sin, sin)

    ch, sh = angle_tables(0, first_half)
    ci, si = angle_tables(1, even)

    def rope(xc):
        sw = jnp.where(first_half, pltpu.roll(xc, LANES - 32, 1), pltpu.roll(xc, 32, 1))
        return xc * ch + sw * sh

    def pair_rot(xc):
        sw = jnp.where(even, pltpu.roll(xc, LANES - 1, 1), pltpu.roll(xc, 1, 1))
        return xc * ci + sw * si

    def emit(out_ref, c0, width, fn):
        p = _dot(h, wb_ref[:, c0:c0 + width])
        for j in range(width // LANES):
            sl = slice(j * LANES, (j + 1) * LANES)
            out_ref[:, sl] = fn(p[:, sl]).astype(out_ref.dtype)

    c_q, c_k = 0, dq_w
    c_rq = dq_w + dk_w + dv_w
    c_rk, c_rv, c_rg = c_rq + rq_w, c_rq + rq_w + rk_w, c_rq + rq_w + rk_w + rv_w
    emit(rq_ref, c_rq, rq_w, pair_rot)
    emit(rk_ref, c_rk, rk_w, lambda t: pair_rot(t) * (RET_KEY_DIM ** -0.5))
    emit(rv_ref, c_rv, rv_w, lambda t: t)
    emit(rg_ref, c_rg, rg_w, lambda t: t)
    retention = _retention_tile(rq_ref, rk_ref, rv_ref, rg_ref, r_ref, state_ref,
                                dmat_ref, qdec_ref, kdec_ref, cdec_ref,
                                _param_row(p_ref, P_RET_G, rv_w), chunk)
    next(retention)
    emit(q_ref, c_q, dq_w, lambda t: rope(t) * (DIFF_HEAD_DIM ** -0.5 * LOG2E))
    next(retention, None)
    emit(k_ref, c_k, dk_w, rope)
    vt = lax.dot_general(wvt_ref[...], h, NT_DIMS, preferred_element_type=F32)
    tkc = vt_ref.shape[-1]
    for j in range(vt_ref.shape[0]):
        vt_ref[j] = vt[:, j * tkc:(j + 1) * tkc].astype(vt_ref.dtype)


def _inproj(x2d, params, w, tabs, ret_consts, cast_weights, seq, widths, tm, tk, chunk):
    T, D = x2d.shape
    cast_specs, cast_shapes, cast_blocks = _cast_plan(cast_weights, T // tm, lambda t: t)
    P = w.shape[1]
    nps = seq // tm
    dq_w, dk_w, dv_w, rq_w, rk_w, rv_w, rg_w = widths
    row = lambda t: (t, 0)
    const = lambda t: (0, 0)
    full = lambda a: pl.BlockSpec(a.shape, lambda t: (0,) * a.ndim)
    out_shape = [jax.ShapeDtypeStruct((T, dq_w), BF16),
                 jax.ShapeDtypeStruct((T, dk_w), BF16),
                 jax.ShapeDtypeStruct((T // seq, seq // tk, dv_w, tk), BF16),
                 jax.ShapeDtypeStruct((T, rv_w), BF16)]
    out_specs = [pl.BlockSpec((tm, dq_w), row),
                 pl.BlockSpec((tm, dk_w), row),
                 pl.BlockSpec((None, tm // tk, dv_w, tk), lambda t: (t // nps, t % nps, 0, 0)),
                 pl.BlockSpec((tm, rv_w), row)]
    outs = pl.pallas_call(
        functools.partial(_inproj_body, widths=widths, tiles_per_seq=nps, chunk=chunk,
                          cast_blocks=cast_blocks),
        grid=(T // tm,),
        in_specs=[pl.BlockSpec((tm, D), row),
                  pl.BlockSpec(params.shape, const),
                  pl.BlockSpec((D, P), const, pipeline_mode=pl.Buffered(1))]
                 + [full(a) for a in tabs] + [full(a) for a in ret_consts]
                 + cast_specs,
        out_specs=out_specs + cast_specs,
        out_shape=out_shape + cast_shapes,
        scratch_shapes=[pltpu.VMEM((tm, rq_w), BF16), pltpu.VMEM((tm, rk_w), BF16),
                        pltpu.VMEM((tm, rv_w), BF16), pltpu.VMEM((tm, rg_w), F32),
                        pltpu.VMEM((rq_w // LANES, LANES, LANES), F32),
                        pltpu.VMEM((D, P), BF16), pltpu.VMEM((dv_w, D), BF16)],
        compiler_params=pltpu.CompilerParams(
            dimension_semantics=("arbitrary",), vmem_limit_bytes=56 * 1024 * 1024),
        name="inproj",
    )(x2d, params, w, *tabs, *ret_consts, *cast_weights)
    return outs[:4], outs[4:]


def _attn_body(p_ref, q_ref, k_ref, vt_ref, o_ref,
               q2_ref, m_ref, acc_ref, sa_ref, sb_ref, mxa_ref, mxb_ref, kc_ref,
               *, tq, gw, lam_init):
    i = pl.program_id(2)
    last_block = pl.num_programs(2) - 1
    tk = tq // 2
    mq = 2 * tq
    groups = [slice(g * gw, (g + 1) * gw) for g in range(mq // gw)]
    all_groups = tuple(range(len(groups)))
    lane = lax.broadcasted_iota(jnp.int32, (tq, LANES), 1)

    def load_queries(block):
        q = q_ref[pl.ds(pl.multiple_of(block * tq, tq), tq), :]
        zero = jnp.zeros_like(q)
        q2_ref[0:tq, :] = jnp.where(lane < DIFF_HEAD_DIM, q, zero)
        q2_ref[tq:mq, :] = jnp.where(lane >= DIFF_HEAD_DIM, q, zero)

    m_ref[...] = jnp.full(m_ref.shape, -jnp.inf, F32)
    acc_ref[...] = jnp.zeros(acc_ref.shape, F32)

    def visibility(r, g):
        q0 = (g * gw) % tq
        k0 = r * tk
        if k0 + tk - 1 <= q0:
            return "all", None
        if k0 > q0 + gw - 1:
            return "none", None
        return "some", k0 - q0

    def stage_keys(j):
        kc_ref[...] = k_ref[pl.ds(pl.multiple_of(j * tk, tk), tk), :]

    def visible_rows(shift):
        return tk if shift is None else min(tk, gw - shift)

    def produce(g, s_ref, mx_ref, shift=None):
        nr = visible_rows(shift)
        s = lax.dot_general(kc_ref[0:nr, :], q2_ref[groups[g], :], NT_DIMS,
                            preferred_element_type=F32)
        s_ref[0:nr, groups[g]] = s
        if shift is None:
            mx_ref[0:1, groups[g]] = jnp.max(s, axis=0, keepdims=True)

    def values(j, nr):
        return jnp.concatenate([vt_ref[j, :, 0:nr], jnp.ones((BF16_ROWS, nr), BF16)], axis=0)

    def consume(j, va, g, s_ref, mx_ref, shift):
        cs = groups[g]
        nr = visible_rows(shift)
        m_old = m_ref[0:1, cs]
        if shift is None:
            s = s_ref[:, cs]
            m_new = jnp.maximum(m_old, mx_ref[0:1, cs])
        else:
            visible = (lax.broadcasted_iota(jnp.int32, (nr, gw), 0) + shift
                       <= lax.broadcasted_iota(jnp.int32, (nr, gw), 1))
            s = jnp.where(visible, s_ref[0:nr, cs], -jnp.inf)
            m_new = jnp.maximum(m_old, jnp.max(s, axis=0, keepdims=True))
            va = va if nr == tk else values(j, nr)
        alpha = jnp.exp2(m_old - m_new)
        p = jnp.exp2(s - m_new).astype(BF16)
        m_ref[0:1, cs] = m_new
        acc_ref[:, cs] = alpha * acc_ref[:, cs] + _dot(va, p)

    def step(j, cur, nxt, diag=None):
        va = values(j, tk)
        if diag != 1:
            stage_keys(j + 1)
        for g in all_groups:
            if diag is None:
                produce(g, *nxt)
            elif diag == 0 and visibility(1, g)[0] != "none":
                produce(g, *nxt, shift=visibility(1, g)[1])
            kind, shift = ("all", None) if diag is None else visibility(diag, g)
            if kind != "none":
                consume(j, va, g, *cur, shift=shift)

    buf_a = (sa_ref, mxa_ref)
    buf_b = (sb_ref, mxb_ref)

    @pl.when(i == 0)
    def _():
        load_queries(0)
        stage_keys(0)
        for g in all_groups:
            produce(g, *buf_a)

    def pair(t):
        step(2 * t, buf_a, buf_b)
        step(2 * t + 1, buf_b, buf_a)

    def four_pairs_body(u, carry):
        for t in range(4):
            pair(4 * u + t)
        return carry

    lax.fori_loop(0, i // 4, four_pairs_body, 0)
    done = (i // 4) * 4

    @pl.when(i & 2 != 0)
    def _():
        pair(done)
        pair(done + 1)

    @pl.when(i & 1 != 0)
    def _():
        pair(i - 1)

    step(2 * i, buf_a, buf_b, diag=0)

    def finish():
        lam_vec = lambda row: _param_row(p_ref, row, DIFF_HEAD_DIM)
        lam = (jnp.exp(jnp.sum(lam_vec(P_LAMBDA_Q1) * lam_vec(P_LAMBDA_K1), axis=-1, keepdims=True))
               - jnp.exp(jnp.sum(lam_vec(P_LAMBDA_Q2) * lam_vec(P_LAMBDA_K2), axis=-1, keepdims=True))
               + lam_init)
        num = acc_ref[0:DIFF_V_DIM, :]
        den = acc_ref[DIFF_V_DIM:DIFF_V_DIM + 1, :]
        o_t = num[:, :tq] / den[:, :tq] - lam * (num[:, tq:] / den[:, tq:])
        o = o_t.T
        gain = _param_row(p_ref, P_SUBLN_G, DIFF_V_DIM)
        o_ref[...] = (_rms(o, gain, HEAD_NORM_EPS) * (1.0 - lam_init)).astype(o_ref.dtype)

    @pl.when(i < last_block)
    def _():
        step(2 * i + 1, buf_b, buf_a, diag=1)
        load_queries(i + 1)
        stage_keys(0)
        for g in all_groups:
            produce(g, *buf_a)
        finish()

    @pl.when(i == last_block)
    def _():
        step(2 * i + 1, buf_b, buf_a, diag=1)
        finish()


def _diffattn(params, q, k, vt, tq, gw, lam_init):
    B, S, W = q.shape
    tk = vt.shape[3]
    assert tq == 2 * tk and tq % gw == 0
    mq = 2 * tq
    return pl.pallas_call(
        functools.partial(_attn_body, tq=tq, gw=gw, lam_init=lam_init),
        grid=(B, W // DIFF_V_DIM, S // tq),
        in_specs=[pl.BlockSpec(params.shape, lambda b, h, i: (0, 0)),
                  pl.BlockSpec((None, S, LANES), lambda b, h, i: (b, 0, h)),
                  pl.BlockSpec((None, S, LANES), lambda b, h, i: (b, 0, h)),
                  pl.BlockSpec((None, S // tk, DIFF_V_DIM, tk), lambda b, h, i: (b, 0, h, 0))],
        out_specs=pl.BlockSpec((None, tq, LANES), lambda b, h, i: (b, i, h)),
        out_shape=jax.ShapeDtypeStruct((B, S, W), BF16),
        scratch_shapes=[pltpu.VMEM((mq, LANES), BF16),
                        pltpu.VMEM((8, mq), F32),
                        pltpu.VMEM((DIFF_V_DIM + BF16_ROWS, mq), F32),
                        pltpu.VMEM((tk, mq), F32), pltpu.VMEM((tk, mq), F32),
                        pltpu.VMEM((8, mq), F32), pltpu.VMEM((8, mq), F32),
                        pltpu.VMEM((tk, LANES), BF16)],
        compiler_params=pltpu.CompilerParams(
            dimension_semantics=("arbitrary", "arbitrary", "arbitrary"),
            vmem_limit_bytes=48 * 1024 * 1024),
        name="diffattn",
    )(params, q, k, vt)


def _retention_consts(n_heads, chunk):
    log_g = np.log1p(-(2.0 ** (-5.0 - np.arange(n_heads, dtype=np.float64))))
    n = np.arange(chunk, dtype=np.float64)
    rel = n[:, None] - n[None, :]
    dmat = np.where(rel >= 0, np.exp(log_g[:, None, None] * np.maximum(rel, 0.0)), 0.0)
    qdec = np.exp(log_g[:, None] * (n + 1.0)[None, :])
    kdec = np.exp(log_g[:, None] * (chunk - 1.0 - n)[None, :])
    cdec = np.exp(log_g * chunk)
    bl = lambda a: np.ascontiguousarray(np.broadcast_to(a[:, :, None], a.shape + (LANES,)))
    cdec_rows = np.repeat(cdec, RET_KEY_DIM).reshape(n_heads // 2, LANES)
    return tuple(jnp.asarray(a, F32) for a in (dmat, bl(qdec), bl(kdec), bl(cdec_rows)))


def _outffn_body(x_ref, a_ref, r_ref, p_ref, wo_ref, wg_ref, wu_ref, wd_ref, o_ref):
    mix = jnp.concatenate([a_ref[...], r_ref[...]], axis=1)
    x1 = x_ref[...] + _dot(mix, wo_ref[...])
    h = _rms(x1, _param_row(p_ref, P_FFN_G), NORM_EPS).astype(BF16)
    gate = _dot(h, wg_ref[...])
    up = _dot(h, wu_ref[...])
    act = (gate * jax.nn.sigmoid(gate) * up).astype(BF16)
    x2 = x1 + _dot(act, wd_ref[...])
    o_ref[...] = _rms(x2, _param_row(p_ref, P_FINAL_G), NORM_EPS)


def _outffn(x2d, a, r, params, wo, wg, wu, wd, tm):
    T, D = x2d.shape
    row = lambda t: (t, 0)
    const = lambda t: (0, 0)
    resident = lambda arr: pl.BlockSpec(arr.shape, const, pipeline_mode=pl.Buffered(1))
    return pl.pallas_call(
        _outffn_body,
        grid=(T // tm,),
        in_specs=[pl.BlockSpec((tm, D), row),
                  pl.BlockSpec((tm, a.shape[1]), row),
                  pl.BlockSpec((tm, r.shape[1]), row),
                  pl.BlockSpec(params.shape, const),
                  resident(wo), resident(wg), resident(wu), resident(wd)],
        out_specs=pl.BlockSpec((tm, D), row),
        out_shape=jax.ShapeDtypeStruct((T, D), F32),
        compiler_params=pltpu.CompilerParams(
            dimension_semantics=("arbitrary",), vmem_limit_bytes=56 * 1024 * 1024),
        name="outffn",
    )(x2d, a, r, params, wo, wg, wu, wd)


def _rotation_tables(seq, tm):
    d = DIFF_HEAD_DIM
    inv_h = ROPE_THETA ** (-np.arange(0, d, 2, dtype=np.float64) / d)
    lanes_h = np.tile(np.concatenate([inv_h, inv_h]), LANES // d)
    dr = RET_KEY_DIM
    inv_r = 1.0 / (RET_THETA ** np.linspace(0.0, 1.0, dr // 2, dtype=np.float64))
    lanes_r = np.tile(np.repeat(inv_r, 2), LANES // dr)

    def cos_sin(pos):
        out = []
        for freq in (lanes_h, lanes_r):
            ang = pos[:, None] * freq[None, :]
            out += [np.cos(ang), np.sin(ang)]
        return jnp.asarray(np.stack(out), F32)

    starts = np.arange(seq // tm, dtype=np.float64) * tm
    return cos_sin(starts), cos_sin(np.arange(tm, dtype=np.float64))


def kernel(x, attn_norm_g, w_in, lambda_q1, lambda_k1, lambda_q2, lambda_k2, diff_subln_g,
           ret_norm_g, w_out, ffn_norm_g, w_gate, w_up, w_down, final_norm_g):
    B, S, D = x.shape
    assert w_in.shape[0] == 1, "single-layer block only"
    l = 0
    diff_w = D // 2
    ret_w = D - diff_w
    ret_heads = ret_w // RET_V_DIM
    widths = (diff_w, diff_w, diff_w, ret_heads * RET_KEY_DIM, ret_heads * RET_KEY_DIM, ret_w, ret_w)
    tm_in = 512
    tabs = _rotation_tables(S, tm_in)
    ret_chunk = 128
    ret_consts = _retention_consts(ret_heads, ret_chunk)
    xs = x.reshape(B * S, D)
    lam_init = 0.8 - 0.6 * math.exp(-0.3 * l)
    params = _pack_params(D, [attn_norm_g[l], ffn_norm_g[l], final_norm_g, ret_norm_g[l],
                              diff_subln_g[l], lambda_q1[l], lambda_k1[l], lambda_q2[l], lambda_k2[l]])
    later_weights = [w_out[l], w_gate[l], w_up[l], w_down[l]]
    (q, k, vt, r), (wo_b, wg_b, wu_b, wd_b) = _inproj(
        xs, params, w_in[l], tabs, ret_consts, later_weights, S, widths,
        tm=tm_in, tk=512, chunk=ret_chunk)
    tok3 = lambda t: t.reshape(B, S, t.shape[-1])
    a = _diffattn(params, tok3(q), tok3(k), vt, tq=1024, gw=256, lam_init=lam_init)
    out = _outffn(xs, a.reshape(B * S, diff_w), r, params, wo_b, wg_b, wu_b, wd_b, tm=512)
    return out.reshape(B, S, D)
```

```python
import functools
import math

import jax
import jax.numpy as jnp
import numpy as np
from jax import lax
from jax.experimental import pallas as pl
from jax.experimental.pallas import tpu as pltpu

F32 = jnp.float32
BF16 = jnp.bfloat16

DIFF_HEAD_DIM = 64
DIFF_V_DIM = 128
RET_KEY_DIM = 64
RET_V_DIM = 128
ROPE_THETA = 10000.0
RET_THETA = 10000.0
NORM_EPS = 1e-6
HEAD_NORM_EPS = 1e-5
LANES = 128
BF16_ROWS = 16
LOG2E = math.log2(math.e)

NT_DIMS = (((1,), (1,)), ((), ()))
TN_DIMS = (((0,), (0,)), ((), ()))


def _dot(a, b):
    return jnp.dot(a, b, preferred_element_type=F32)


def _rms(x, g, eps):
    ms = jnp.mean(x * x, axis=-1, keepdims=True)
    return x * lax.rsqrt(ms + eps) * g


(P_ATTN_G, P_FFN_G, P_FINAL_G, P_RET_G, P_SUBLN_G,
 P_LAMBDA_Q1, P_LAMBDA_K1, P_LAMBDA_Q2, P_LAMBDA_K2) = range(9)
P_ROWS = 16


def _pack_params(d_model, vectors):
    pieces = []
    for v in vectors:
        pieces += [v.astype(F32), jnp.zeros((d_model - v.shape[0],), F32)]
    pieces.append(jnp.zeros(((P_ROWS - len(vectors)) * d_model,), F32))
    return jnp.concatenate(pieces).reshape(P_ROWS, d_model)


def _param_row(p_ref, row, width=None):
    width = p_ref.shape[1] if width is None else width
    return p_ref[row:row + 1, 0:width]


def _cast_plan(weights, steps, step_of):
    specs, shapes, n_blocks = [], [], []
    for w in weights:
        rows, cols = w.shape
        rb = BF16_ROWS
        while rows % rb or rows // rb > steps:
            rb += BF16_ROWS
        nb = rows // rb
        specs.append(pl.BlockSpec(
            (rb, cols), lambda *idx, nb=nb: (jnp.minimum(step_of(*idx), nb - 1), 0)))
        shapes.append(jax.ShapeDtypeStruct(w.shape, BF16))
        n_blocks.append(nb)
    return specs, shapes, tuple(n_blocks)


def _cast_step(step, w_refs, wb_refs, n_blocks):
    for w_ref, wb_ref, nb in zip(w_refs, wb_refs, n_blocks):
        @pl.when(step < nb)
        def _(w_ref=w_ref, wb_ref=wb_ref):
            wb_ref[...] = w_ref[...].astype(wb_ref.dtype)


def _retention_tile(rq_ref, rk_ref, rv_ref, rg_ref, o_ref, state_ref,
                    dmat_ref, qdec_ref, kdec_ref, cdec_ref, gain, chunk):
    n_chunks = rq_ref.shape[0] // chunk
    n_pairs = rq_ref.shape[1] // LANES
    lane = lax.broadcasted_iota(jnp.int32, (chunk, LANES), 1)
    row = lax.broadcasted_iota(jnp.int32, (LANES, LANES), 0)
    rows = [slice(c * chunk, (c + 1) * chunk) for c in range(n_chunks)]
    head_lanes = [lane < RET_KEY_DIM, lane >= RET_KEY_DIM]

    def q_masked(c, p, hh):
        qp = rq_ref[rows[c], p * LANES:(p + 1) * LANES]
        return jnp.where(head_lanes[hh], qp, jnp.zeros_like(qp))

    scores, updates = {}, {}
    for c in range(n_chunks):
        for p in range(n_pairs):
            kp = rk_ref[rows[c], p * LANES:(p + 1) * LANES]
            for hh in range(2):
                h = 2 * p + hh
                vh = rv_ref[rows[c], h * RET_V_DIM:(h + 1) * RET_V_DIM]
                s = lax.dot_general(q_masked(c, p, hh), kp, NT_DIMS, preferred_element_type=F32)
                scores[c, h] = (s * dmat_ref[h]).astype(BF16)
                vd = (vh.astype(F32) * kdec_ref[h]).astype(BF16)
                updates[c, h] = lax.dot_general(kp, vd, TN_DIMS, preferred_element_type=F32)

    states = {}
    for p in range(n_pairs):
        st = state_ref[p]
        for c in range(n_chunks):
            states[c, p] = st.astype(BF16)
            st = cdec_ref[p] * st + jnp.where(row < RET_KEY_DIM, updates[c, 2 * p], updates[c, 2 * p + 1])
        state_ref[p] = st

    yield

    for c in range(n_chunks):
        for p in range(n_pairs):
            for hh in range(2):
                h = 2 * p + hh
                hs = slice(h * RET_V_DIM, (h + 1) * RET_V_DIM)
                cross = _dot(q_masked(c, p, hh), states[c, p]) * qdec_ref[h]
                o = _dot(scores[c, h], rv_ref[rows[c], hs]) + cross
                mu = jnp.mean(o, axis=-1, keepdims=True)
                cen = o - mu
                var = jnp.mean(cen * cen, axis=-1, keepdims=True)
                y = cen * lax.rsqrt(var + HEAD_NORM_EPS) * gain[:, hs]
                gate = rg_ref[rows[c], hs]
                o_ref[rows[c], hs] = (gate * jax.nn.sigmoid(gate) * y).astype(o_ref.dtype)


def _inproj_body(x_ref, p_ref, w_ref, base_ref, off_ref,
                 dmat_ref, qdec_ref, kdec_ref, cdec_ref, *refs,
                 widths, tiles_per_seq, chunk, cast_blocks):
    n_cast = len(cast_blocks)
    cast_in, refs = refs[:n_cast], refs[n_cast:]
    (q_ref, k_ref, vt_ref, r_ref), refs = refs[:4], refs[4:]
    cast_out, refs = refs[:n_cast], refs[n_cast:]
    rq_ref, rk_ref, rv_ref, rg_ref, state_ref, wb_ref, wvt_ref = refs
    dq_w, dk_w, dv_w, rq_w, rk_w, rv_w, rg_w = widths
    tile_in_seq = pl.program_id(0) % tiles_per_seq
    _cast_step(pl.program_id(0), cast_in, cast_out, cast_blocks)

    @pl.when(pl.program_id(0) == 0)
    def _():
        c_v = dq_w + dk_w
        for c0 in range(0, w_ref.shape[1], dv_w):
            if c0 == c_v:
                wvt_ref[...] = w_ref[:, c0:c0 + dv_w].T.astype(wvt_ref.dtype)
            else:
                wb_ref[:, c0:c0 + dv_w] = w_ref[:, c0:c0 + dv_w].astype(wb_ref.dtype)

    @pl.when(tile_in_seq == 0)
    def _():
        state_ref[...] = jnp.zeros(state_ref.shape, F32)

    h = _rms(x_ref[...], _param_row(p_ref, P_ATTN_G), NORM_EPS).astype(BF16)
    lane = lax.broadcasted_iota(jnp.int32, off_ref.shape[1:], 1)
    first_half = (lane & (DIFF_HEAD_DIM // 2)) == 0
    even = (lane & 1) == 0

    def angle_tables(kind, negate_where):
        cb = base_ref[2 * kind, pl.ds(tile_in_seq, 1), :]
        sb = base_ref[2 * kind + 1, pl.ds(tile_in_seq, 1), :]
        co, so = off_ref[2 * kind], off_ref[2 * kind + 1]
        sin = sb * co + cb * so
        return cb * co - sb * so, jnp.where(negate_where, -sin, sin)

    ch, sh = angle_tables(0, first_half)
    ci, si = angle_tables(1, even)

    def rope(xc):
        sw = jnp.where(first_half, pltpu.roll(xc, LANES - 32, 1), pltpu.roll(xc, 32, 1))
        return xc * ch + sw * sh

    def pair_rot(xc):
        sw = jnp.where(even, pltpu.roll(xc, LANES - 1, 1), pltpu.roll(xc, 1, 1))
        return xc * ci + sw * si

    def emit(out_ref, c0, width, fn):
        p = _dot(h, wb_ref[:, c0:c0 + width])
        for j in range(width // LANES):
            sl = slice(j * LANES, (j + 1) * LANES)
            out_ref[:, sl] = fn(p[:, sl]).astype(out_ref.dtype)

    c_q, c_k = 0, dq_w
    c_rq = dq_w + dk_w + dv_w
    c_rk, c_rv, c_rg = c_rq + rq_w, c_rq + rq_w + rk_w, c_rq + rq_w + rk_w + rv_w
    emit(rq_ref, c_rq, rq_w, pair_rot)
    emit(rk_ref, c_rk, rk_w, lambda t: pair_rot(t) * (RET_KEY_DIM ** -0.5))
    emit(rv_ref, c_rv, rv_w, lambda t: t)
    emit(rg_ref, c_rg, rg_w, lambda t: t)
    retention = _retention_tile(rq_ref, rk_ref, rv_ref, rg_ref, r_ref, state_ref,
                                dmat_ref, qdec_ref, kdec_ref, cdec_ref,
                                _param_row(p_ref, P_RET_G, rv_w), chunk)
    next(retention)
    emit(q_ref, c_q, dq_w, lambda t: rope(t) * (DIFF_HEAD_DIM ** -0.5 * LOG2E))
    next(retention, None)
    emit(k_ref, c_k, dk_w, rope)
    vt = lax.dot_general(wvt_ref[...], h, NT_DIMS, preferred_element_type=F32)
    tkc = vt_ref.shape[-1]
    for j in range(vt_ref.shape[0]):
        vt_ref[j] = vt[:, j * tkc:(j + 1) * tkc].astype(vt_ref.dtype)


def _inproj(x2d, params, w, tabs, ret_consts, cast_weights, seq, widths, tm, tk, chunk):
    T, D = x2d.shape
    cast_specs, cast_shapes, cast_blocks = _cast_plan(cast_weights, T // tm, lambda t: t)
    P = w.shape[1]
    nps = seq // tm
    dq_w, dk_w, dv_w, rq_w, rk_w, rv_w, rg_w = widths
    row = lambda t: (t, 0)
    const = lambda t: (0, 0)
    full = lambda a: pl.BlockSpec(a.shape, lambda t: (0,) * a.ndim)
    out_shape = [jax.ShapeDtypeStruct((T, dq_w), BF16),
                 jax.ShapeDtypeStruct((T, dk_w), BF16),
                 jax.ShapeDtypeStruct((T // seq, seq // tk, dv_w, tk), BF16),
                 jax.ShapeDtypeStruct((T, rv_w), BF16)]
    out_specs = [pl.BlockSpec((tm, dq_w), row),
                 pl.BlockSpec((tm, dk_w), row),
                 pl.BlockSpec((None, tm // tk, dv_w, tk), lambda t: (t // nps, t % nps, 0, 0)),
                 pl.BlockSpec((tm, rv_w), row)]
    outs = pl.pallas_call(
        functools.partial(_inproj_body, widths=widths, tiles_per_seq=nps, chunk=chunk,
                          cast_blocks=cast_blocks),
        grid=(T // tm,),
        in_specs=[pl.BlockSpec((tm, D), row),
                  pl.BlockSpec(params.shape, const),
                  pl.BlockSpec((D, P), const, pipeline_mode=pl.Buffered(1))]
                 + [full(a) for a in tabs] + [full(a) for a in ret_consts]
                 + cast_specs,
        out_specs=out_specs + cast_specs,
        out_shape=out_shape + cast_shapes,
        scratch_shapes=[pltpu.VMEM((tm, rq_w), BF16), pltpu.VMEM((tm, rk_w), BF16),
                        pltpu.VMEM((tm, rv_w), BF16), pltpu.VMEM((tm, rg_w), F32),
                        pltpu.VMEM((rq_w // LANES, LANES, LANES), F32),
                        pltpu.VMEM((D, P), BF16), pltpu.VMEM((dv_w, D), BF16)],
        compiler_params=pltpu.CompilerParams(
            dimension_semantics=("arbitrary",), vmem_limit_bytes=56 * 1024 * 1024),
        name="inproj",
    )(x2d, params, w, *tabs, *ret_consts, *cast_weights)
    return outs[:4], outs[4:]


def _attn_body(p_ref, q_ref, k_ref, vt_ref, o_ref,
               q2_ref, m_ref, acc_ref, sa_ref, sb_ref, mxa_ref, mxb_ref, kc_ref,
               *, tq, gw, lam_init):
    i = pl.program_id(2)
    last_block = pl.num_programs(2) - 1
    tk = tq // 2
    mq = 2 * tq
    groups = [slice(g * gw, (g + 1) * gw) for g in range(mq // gw)]
    all_groups = tuple(range(len(groups)))
    lane = lax.broadcasted_iota(jnp.int32, (tq, LANES), 1)

    def load_queries(block):
        q = q_ref[pl.ds(pl.multiple_of(block * tq, tq), tq), :]
        zero = jnp.zeros_like(q)
        q2_ref[0:tq, :] = jnp.where(lane < DIFF_HEAD_DIM, q, zero)
        q2_ref[tq:mq, :] = jnp.where(lane >= DIFF_HEAD_DIM, q, zero)

    m_ref[...] = jnp.full(m_ref.shape, -jnp.inf, F32)
    acc_ref[...] = jnp.zeros(acc_ref.shape, F32)

    def visibility(r, g):
        q0 = (g * gw) % tq
        k0 = r * tk
        if k0 + tk - 1 <= q0:
            return "all", None
        if k0 > q0 + gw - 1:
            return "none", None
        return "some", k0 - q0

    def stage_keys(j):
        kc_ref[...] = k_ref[pl.ds(pl.multiple_of(j * tk, tk), tk), :]

    def visible_rows(shift):
        return tk if shift is None else min(tk, gw - shift)

    def produce(g, s_ref, mx_ref, shift=None):
        nr = visible_rows(shift)
        s = lax.dot_general(kc_ref[0:nr, :], q2_ref[groups[g], :], NT_DIMS,
                            preferred_element_type=F32)
        s_ref[g, 0:nr, :] = s
        if shift is None:
            mx_ref[0:1, groups[g]] = jnp.max(s, axis=0, keepdims=True)

    def values(j, nr):
        return jnp.concatenate([vt_ref[j, :, 0:nr], jnp.ones((BF16_ROWS, nr), BF16)], axis=0)

    def consume(j, va, g, s_ref, mx_ref, shift):
        cs = groups[g]
        nr = visible_rows(shift)
        m_old = m_ref[0:1, cs]
        if shift is None:
            s = s_ref[g]
            m_new = jnp.maximum(m_old, mx_ref[0:1, cs])
        else:
            visible = (lax.broadcasted_iota(jnp.int32, (nr, gw), 0) + shift
                       <= lax.broadcasted_iota(jnp.int32, (nr, gw), 1))
            s = jnp.where(visible, s_ref[g, 0:nr, :], -jnp.inf)
            m_new = jnp.maximum(m_old, jnp.max(s, axis=0, keepdims=True))
            va = va if nr == tk else values(j, nr)
        alpha = jnp.exp2(m_old - m_new)
        p = jnp.exp2(s - m_new).astype(BF16)
        m_ref[0:1, cs] = m_new
        acc_ref[g] = alpha * acc_ref[g] + _dot(va, p)

    def step(j, cur, nxt, diag=None):
        va = values(j, tk)
        if diag != 1:
            stage_keys(j + 1)
        for g in all_groups:
            if diag is None:
                produce(g, *nxt)
            elif diag == 0 and visibility(1, g)[0] != "none":
                produce(g, *nxt, shift=visibility(1, g)[1])
            kind, shift = ("all", None) if diag is None else visibility(diag, g)
            if kind != "none":
                consume(j, va, g, *cur, shift=shift)

    buf_a = (sa_ref, mxa_ref)
    buf_b = (sb_ref, mxb_ref)

    @pl.when(i == 0)
    def _():
        load_queries(0)
        stage_keys(0)
        for g in all_groups:
            produce(g, *buf_a)

    def pair(t):
        step(2 * t, buf_a, buf_b)
        step(2 * t + 1, buf_b, buf_a)

    def four_pairs_body(u, carry):
        for t in range(4):
            pair(4 * u + t)
        return carry

    lax.fori_loop(0, i // 4, four_pairs_body, 0)
    done = (i // 4) * 4
    for rest in range(4):
        @pl.when(i % 4 == rest)
        def _(rest=rest):
            for t in range(rest):
                pair(done + t)
            step(2 * i, buf_a, buf_b, diag=0)

    def finish():
        lam_vec = lambda row: _param_row(p_ref, row, DIFF_HEAD_DIM)
        lam = (jnp.exp(jnp.sum(lam_vec(P_LAMBDA_Q1) * lam_vec(P_LAMBDA_K1), axis=-1, keepdims=True))
               - jnp.exp(jnp.sum(lam_vec(P_LAMBDA_Q2) * lam_vec(P_LAMBDA_K2), axis=-1, keepdims=True))
               + lam_init)
        acc = jnp.concatenate([acc_ref[g] for g in all_groups], axis=1)
        num = acc[0:DIFF_V_DIM, :]
        den = acc[DIFF_V_DIM:DIFF_V_DIM + 1, :]
        o_t = num[:, :tq] / den[:, :tq] - lam * (num[:, tq:] / den[:, tq:])
        o = o_t.T
        gain = _param_row(p_ref, P_SUBLN_G, DIFF_V_DIM)
        o_ref[...] = (_rms(o, gain, HEAD_NORM_EPS) * (1.0 - lam_init)).astype(o_ref.dtype)

    @pl.when(i < last_block)
    def _():
        step(2 * i + 1, buf_b, buf_a, diag=1)
        load_queries(i + 1)
        stage_keys(0)
        for g in all_groups:
            produce(g, *buf_a)
        finish()

    @pl.when(i == last_block)
    def _():
        step(2 * i + 1, buf_b, buf_a, diag=1)
        finish()


def _diffattn(params, q, k, vt, tq, gw, lam_init):
    B, S, W = q.shape
    tk = vt.shape[3]
    assert tq == 2 * tk and tq % gw == 0
    mq = 2 * tq
    return pl.pallas_call(
        functools.partial(_attn_body, tq=tq, gw=gw, lam_init=lam_init),
        grid=(B, W // DIFF_V_DIM, S // tq),
        in_specs=[pl.BlockSpec(params.shape, lambda b, h, i: (0, 0)),
                  pl.BlockSpec((None, S, LANES), lambda b, h, i: (b, 0, h)),
                  pl.BlockSpec((None, S, LANES), lambda b, h, i: (b, 0, h)),
                  pl.BlockSpec((None, S // tk, DIFF_V_DIM, tk), lambda b, h, i: (b, 0, h, 0))],
        out_specs=pl.BlockSpec((None, tq, LANES), lambda b, h, i: (b, i, h)),
        out_shape=jax.ShapeDtypeStruct((B, S, W), BF16),
        scratch_shapes=[pltpu.VMEM((mq, LANES), BF16),
                        pltpu.VMEM((8, mq), F32),
                        pltpu.VMEM((mq // gw, DIFF_V_DIM + BF16_ROWS, gw), F32),
                        pltpu.VMEM((mq // gw, tk, gw), F32), pltpu.VMEM((mq // gw, tk, gw), F32),
                        pltpu.VMEM((8, mq), F32), pltpu.VMEM((8, mq), F32),
                        pltpu.VMEM((tk, LANES), BF16)],
        compiler_params=pltpu.CompilerParams(
            dimension_semantics=("arbitrary", "arbitrary", "arbitrary"),
            vmem_limit_bytes=48 * 1024 * 1024),
        name="diffattn",
    )(params, q, k, vt)


def _retention_consts(n_heads, chunk):
    log_g = np.log1p(-(2.0 ** (-5.0 - np.arange(n_heads, dtype=np.float64))))
    n = np.arange(chunk, dtype=np.float64)
    rel = n[:, None] - n[None, :]
    dmat = np.where(rel >= 0, np.exp(log_g[:, None, None] * np.maximum(rel, 0.0)), 0.0)
    qdec = np.exp(log_g[:, None] * (n + 1.0)[None, :])
    kdec = np.exp(log_g[:, None] * (chunk - 1.0 - n)[None, :])
    cdec = np.exp(log_g * chunk)
    bl = lambda a: np.ascontiguousarray(np.broadcast_to(a[:, :, None], a.shape + (LANES,)))
    cdec_rows = np.repeat(cdec, RET_KEY_DIM).reshape(n_heads // 2, LANES)
    return tuple(jnp.asarray(a, F32) for a in (dmat, bl(qdec), bl(kdec), bl(cdec_rows)))


def _outffn_body(x_ref, a_ref, r_ref, p_ref, wo_ref, wg_ref, wu_ref, wd_ref, o_ref):
    mix = jnp.concatenate([a_ref[...], r_ref[...]], axis=1)
    x1 = x_ref[...] + _dot(mix, wo_ref[...])
    h = _rms(x1, _param_row(p_ref, P_FFN_G), NORM_EPS).astype(BF16)
    gate = _dot(h, wg_ref[...])
    up = _dot(h, wu_ref[...])
    act = (gate * jax.nn.sigmoid(gate) * up).astype(BF16)
    x2 = x1 + _dot(act, wd_ref[...])
    o_ref[...] = _rms(x2, _param_row(p_ref, P_FINAL_G), NORM_EPS)


def _outffn(x2d, a, r, params, wo, wg, wu, wd, tm):
    T, D = x2d.shape
    row = lambda t: (t, 0)
    const = lambda t: (0, 0)
    resident = lambda arr: pl.BlockSpec(arr.shape, const, pipeline_mode=pl.Buffered(1))
    return pl.pallas_call(
        _outffn_body,
        grid=(T // tm,),
        in_specs=[pl.BlockSpec((tm, D), row),
                  pl.BlockSpec((tm, a.shape[1]), row),
                  pl.BlockSpec((tm, r.shape[1]), row),
                  pl.BlockSpec(params.shape, const),
                  resident(wo), resident(wg), resident(wu), resident(wd)],
        out_specs=pl.BlockSpec((tm, D), row),
        out_shape=jax.ShapeDtypeStruct((T, D), F32),
        compiler_params=pltpu.CompilerParams(
            dimension_semantics=("arbitrary",), vmem_limit_bytes=56 * 1024 * 1024),
        name="outffn",
    )(x2d, a, r, params, wo, wg, wu, wd)


def _rotation_tables(seq, tm):
    d = DIFF_HEAD_DIM
    inv_h = ROPE_THETA ** (-np.arange(0, d, 2, dtype=np.float64) / d)
    lanes_h = np.tile(np.concatenate([inv_h, inv_h]), LANES // d)
    dr = RET_KEY_DIM
    inv_r = 1.0 / (RET_THETA ** np.linspace(0.0, 1.0, dr // 2, dtype=np.float64))
    lanes_r = np.tile(np.repeat(inv_r, 2), LANES // dr)

    def cos_sin(pos):
        out = []
        for freq in (lanes_h, lanes_r):
            ang = pos[:, None] * freq[None, :]
            out += [np.cos(ang), np.sin(ang)]
        return jnp.asarray(np.stack(out), F32)

    starts = np.arange(seq // tm, dtype=np.float64) * tm
    return cos_sin(starts), cos_sin(np.arange(tm, dtype=np.float64))


def kernel(x, attn_norm_g, w_in, lambda_q1, lambda_k1, lambda_q2, lambda_k2, diff_subln_g,
           ret_norm_g, w_out, ffn_norm_g, w_gate, w_up, w_down, final_norm_g):
    B, S, D = x.shape
    assert w_in.shape[0] == 1, "single-layer block only"
    l = 0
    diff_w = D // 2
    ret_w = D - diff_w
    ret_heads = ret_w // RET_V_DIM
    widths = (diff_w, diff_w, diff_w, ret_heads * RET_KEY_DIM, ret_heads * RET_KEY_DIM, ret_w, ret_w)
    tm_in = 512
    tabs = _rotation_tables(S, tm_in)
    ret_chunk = 128
    ret_consts = _retention_consts(ret_heads, ret_chunk)
    xs = x.reshape(B * S, D)
    lam_init = 0.8 - 0.6 * math.exp(-0.3 * l)
    params = _pack_params(D, [attn_norm_g[l], ffn_norm_g[l], final_norm_g, ret_norm_g[l],
                              diff_subln_g[l], lambda_q1[l], lambda_k1[l], lambda_q2[l], lambda_k2[l]])
    later_weights = [w_out[l], w_gate[l], w_up[l], w_down[l]]
    (q, k, vt, r), (wo_b, wg_b, wu_b, wd_b) = _inproj(
        xs, params, w_in[l], tabs, ret_consts, later_weights, S, widths,
        tm=tm_in, tk=512, chunk=ret_chunk)
    tok3 = lambda t: t.reshape(B, S, t.shape[-1])
    a = _diffattn(params, tok3(q), tok3(k), vt, tq=1024, gw=256, lam_init=lam_init)
    out = _outffn(xs, a.reshape(B * S, diff_w), r, params, wo_b, wg_b, wu_b, wd_b, tm=512)
    return out.reshape(B, S, D)
```

```python
import functools
import math

import jax
import jax.numpy as jnp
import numpy as np
from jax import lax
from jax.experimental import pallas as pl
from jax.experimental.pallas import tpu as pltpu

F32 = jnp.float32
BF16 = jnp.bfloat16

DIFF_HEAD_DIM = 64
DIFF_V_DIM = 128
RET_KEY_DIM = 64
RET_V_DIM = 128
ROPE_THETA = 10000.0
RET_THETA = 10000.0
NORM_EPS = 1e-6
HEAD_NORM_EPS = 1e-5
LANES = 128
BF16_ROWS = 16
VMEM_LIMIT_BYTES = 56 * 1024 * 1024
LOG2E = math.log2(math.e)

NT_DIMS = (((1,), (1,)), ((), ()))
TN_DIMS = (((0,), (0,)), ((), ()))


def _dot(a, b):
    return jnp.dot(a, b, preferred_element_type=F32)


def _rms(x, g, eps):
    ms = jnp.mean(x * x, axis=-1, keepdims=True)
    return x * lax.rsqrt(ms + eps) * g


(P_ATTN_G, P_FFN_G, P_FINAL_G, P_RET_G, P_SUBLN_G,
 P_LAMBDA_Q1, P_LAMBDA_K1, P_LAMBDA_Q2, P_LAMBDA_K2) = range(9)
P_ROWS = 16


def _pack_params(d_model, vectors):
    pieces = []
    for v in vectors:
        pieces += [v.astype(F32), jnp.zeros((d_model - v.shape[0],), F32)]
    pieces.append(jnp.zeros(((P_ROWS - len(vectors)) * d_model,), F32))
    return jnp.concatenate(pieces).reshape(P_ROWS, d_model)


def _param_row(p_ref, row, width=None):
    width = p_ref.shape[1] if width is None else width
    return p_ref[row:row + 1, 0:width]


def _cast_plan(weights, steps):
    specs, shapes, n_blocks = [], [], []
    for w in weights:
        rows, cols = w.shape
        rb = BF16_ROWS
        while rows % rb or rows // rb > steps:
            rb += BF16_ROWS
        nb = rows // rb
        specs.append(pl.BlockSpec((rb, cols), lambda t, nb=nb: (jnp.minimum(t, nb - 1), 0)))
        shapes.append(jax.ShapeDtypeStruct(w.shape, BF16))
        n_blocks.append(nb)
    return specs, shapes, tuple(n_blocks)


def _cast_step(step, w_refs, wb_refs, n_blocks):
    for w_ref, wb_ref, nb in zip(w_refs, wb_refs, n_blocks):
        @pl.when(step < nb)
        def _(w_ref=w_ref, wb_ref=wb_ref):
            wb_ref[...] = w_ref[...].astype(wb_ref.dtype)


def _retention_tile(rq_ref, rk_ref, rv_ref, rg_ref, o_ref, state_ref,
                    dmat_ref, qdec_ref, kdec_ref, cdec_ref, gain, chunk):
    n_chunks = rq_ref.shape[0] // chunk
    n_pairs = rq_ref.shape[1] // LANES
    lane = lax.broadcasted_iota(jnp.int32, (chunk, LANES), 1)
    row = lax.broadcasted_iota(jnp.int32, (LANES, LANES), 0)
    rows = [slice(c * chunk, (c + 1) * chunk) for c in range(n_chunks)]
    head_lanes = [lane < RET_KEY_DIM, lane >= RET_KEY_DIM]

    def q_masked(c, p, hh):
        qp = rq_ref[rows[c], p * LANES:(p + 1) * LANES]
        return jnp.where(head_lanes[hh], qp, jnp.zeros_like(qp))

    scores, updates = {}, {}
    for c in range(n_chunks):
        for p in range(n_pairs):
            kp = rk_ref[rows[c], p * LANES:(p + 1) * LANES]
            for hh in range(2):
                h = 2 * p + hh
                vh = rv_ref[rows[c], h * RET_V_DIM:(h + 1) * RET_V_DIM]
                s = lax.dot_general(q_masked(c, p, hh), kp, NT_DIMS, preferred_element_type=F32)
                scores[c, h] = (s * dmat_ref[h]).astype(BF16)
                vd = (vh.astype(F32) * kdec_ref[h]).astype(BF16)
                updates[c, h] = lax.dot_general(kp, vd, TN_DIMS, preferred_element_type=F32)

    states = {}
    for p in range(n_pairs):
        st = state_ref[p]
        for c in range(n_chunks):
            states[c, p] = st.astype(BF16)
            st = cdec_ref[p] * st + jnp.where(row < RET_KEY_DIM, updates[c, 2 * p], updates[c, 2 * p + 1])
        state_ref[p] = st

    yield

    for c in range(n_chunks):
        for p in range(n_pairs):
            for hh in range(2):
                h = 2 * p + hh
                hs = slice(h * RET_V_DIM, (h + 1) * RET_V_DIM)
                cross = _dot(q_masked(c, p, hh), states[c, p]) * qdec_ref[h]
                o = _dot(scores[c, h], rv_ref[rows[c], hs]) + cross
                mu = jnp.mean(o, axis=-1, keepdims=True)
                cen = o - mu
                var = jnp.mean(cen * cen, axis=-1, keepdims=True)
                y = cen * lax.rsqrt(var + HEAD_NORM_EPS) * gain[:, hs]
                gate = rg_ref[rows[c], hs]
                o_ref[rows[c], hs] = (gate * jax.nn.sigmoid(gate) * y).astype(o_ref.dtype)


def _inproj_body(x_ref, p_ref, w_ref, base_ref, off_ref,
                 dmat_ref, qdec_ref, kdec_ref, cdec_ref, *refs,
                 widths, tiles_per_seq, chunk, cast_blocks):
    n_cast = len(cast_blocks)
    cast_in, refs = refs[:n_cast], refs[n_cast:]
    (q_ref, k_ref, vt_ref, r_ref), refs = refs[:4], refs[4:]
    cast_out, refs = refs[:n_cast], refs[n_cast:]
    rq_ref, rk_ref, rv_ref, rg_ref, state_ref, wb_ref, wvt_ref = refs
    dq_w, dk_w, dv_w, rq_w, rk_w, rv_w, rg_w = widths
    tile_in_seq = pl.program_id(0) % tiles_per_seq
    _cast_step(pl.program_id(0), cast_in, cast_out, cast_blocks)

    @pl.when(pl.program_id(0) == 0)
    def _():
        c_v = dq_w + dk_w
        for c0 in range(0, w_ref.shape[1], dv_w):
            if c0 == c_v:
                wvt_ref[...] = w_ref[:, c0:c0 + dv_w].T.astype(wvt_ref.dtype)
            else:
                wb_ref[:, c0:c0 + dv_w] = w_ref[:, c0:c0 + dv_w].astype(wb_ref.dtype)

    @pl.when(tile_in_seq == 0)
    def _():
        state_ref[...] = jnp.zeros(state_ref.shape, F32)

    h = _rms(x_ref[...], _param_row(p_ref, P_ATTN_G), NORM_EPS).astype(BF16)
    lane = lax.broadcasted_iota(jnp.int32, off_ref.shape[1:], 1)
    first_half = (lane & (DIFF_HEAD_DIM // 2)) == 0
    even = (lane & 1) == 0

    def angle_tables(kind, negate_where):
        cb = base_ref[2 * kind, pl.ds(tile_in_seq, 1), :]
        sb = base_ref[2 * kind + 1, pl.ds(tile_in_seq, 1), :]
        co, so = off_ref[2 * kind], off_ref[2 * kind + 1]
        sin = sb * co + cb * so
        return cb * co - sb * so, jnp.where(negate_where, -sin, sin)

    ch, sh = angle_tables(0, first_half)
    ci, si = angle_tables(1, even)

    def rope(xc):
        sw = jnp.where(first_half, pltpu.roll(xc, LANES - 32, 1), pltpu.roll(xc, 32, 1))
        return xc * ch + sw * sh

    def pair_rot(xc):
        sw = jnp.where(even, pltpu.roll(xc, LANES - 1, 1), pltpu.roll(xc, 1, 1))
        return xc * ci + sw * si

    def emit(out_ref, c0, width, fn):
        p = _dot(h, wb_ref[:, c0:c0 + width])
        for j in range(width // LANES):
            sl = slice(j * LANES, (j + 1) * LANES)
            out_ref[:, sl] = fn(p[:, sl]).astype(out_ref.dtype)

    c_q, c_k = 0, dq_w
    c_rq = dq_w + dk_w + dv_w
    c_rk, c_rv, c_rg = c_rq + rq_w, c_rq + rq_w + rk_w, c_rq + rq_w + rk_w + rv_w
    emit(rq_ref, c_rq, rq_w, pair_rot)
    emit(rk_ref, c_rk, rk_w, lambda t: pair_rot(t) * (RET_KEY_DIM ** -0.5))
    emit(rv_ref, c_rv, rv_w, lambda t: t)
    emit(rg_ref, c_rg, rg_w, lambda t: t)
    retention = _retention_tile(rq_ref, rk_ref, rv_ref, rg_ref, r_ref, state_ref,
                                dmat_ref, qdec_ref, kdec_ref, cdec_ref,
                                _param_row(p_ref, P_RET_G, rv_w), chunk)
    next(retention)
    emit(q_ref, c_q, dq_w, lambda t: rope(t) * (DIFF_HEAD_DIM ** -0.5 * LOG2E))
    next(retention, None)
    emit(k_ref, c_k, dk_w, rope)
    vt = lax.dot_general(wvt_ref[...], h, NT_DIMS, preferred_element_type=F32)
    tkc = vt_ref.shape[-1]
    for j in range(vt_ref.shape[0]):
        vt_ref[j] = vt[:, j * tkc:(j + 1) * tkc].astype(vt_ref.dtype)


def _inproj(x2d, params, w, tabs, ret_consts, cast_weights, seq, widths, tm, tk, chunk):
    T, D = x2d.shape
    cast_specs, cast_shapes, cast_blocks = _cast_plan(cast_weights, T // tm)
    P = w.shape[1]
    nps = seq // tm
    dq_w, dk_w, dv_w, rq_w, rk_w, rv_w, rg_w = widths
    row = lambda t: (t, 0)
    const = lambda t: (0, 0)
    full = lambda a: pl.BlockSpec(a.shape, lambda t: (0,) * a.ndim)
    out_shape = [jax.ShapeDtypeStruct((T, dq_w), BF16),
                 jax.ShapeDtypeStruct((T, dk_w), BF16),
                 jax.ShapeDtypeStruct((T // seq, seq // tk, dv_w, tk), BF16),
                 jax.ShapeDtypeStruct((T, rv_w), BF16)]
    out_specs = [pl.BlockSpec((tm, dq_w), row),
                 pl.BlockSpec((tm, dk_w), row),
                 pl.BlockSpec((None, tm // tk, dv_w, tk), lambda t: (t // nps, t % nps, 0, 0)),
                 pl.BlockSpec((tm, rv_w), row)]
    outs = pl.pallas_call(
        functools.partial(_inproj_body, widths=widths, tiles_per_seq=nps, chunk=chunk,
                          cast_blocks=cast_blocks),
        grid=(T // tm,),
        in_specs=[pl.BlockSpec((tm, D), row),
                  pl.BlockSpec(params.shape, const),
                  pl.BlockSpec((D, P), const, pipeline_mode=pl.Buffered(1))]
                 + [full(a) for a in tabs] + [full(a) for a in ret_consts]
                 + cast_specs,
        out_specs=out_specs + cast_specs,
        out_shape=out_shape + cast_shapes,
        scratch_shapes=[pltpu.VMEM((tm, rq_w), BF16), pltpu.VMEM((tm, rk_w), BF16),
                        pltpu.VMEM((tm, rv_w), BF16), pltpu.VMEM((tm, rg_w), F32),
                        pltpu.VMEM((rq_w // LANES, LANES, LANES), F32),
                        pltpu.VMEM((D, P), BF16), pltpu.VMEM((dv_w, D), BF16)],
        compiler_params=pltpu.CompilerParams(
            dimension_semantics=("arbitrary",), vmem_limit_bytes=VMEM_LIMIT_BYTES),
        name="inproj",
    )(x2d, params, w, *tabs, *ret_consts, *cast_weights)
    return outs[:4], outs[4:]


def _attn_body(p_ref, q_ref, k_ref, vt_ref, o_ref,
               q2_ref, m_ref, acc_ref, sa_ref, sb_ref, mxa_ref, mxb_ref, kc_ref,
               *, tq, gw, lam_init):
    i = pl.program_id(2)
    last_block = pl.num_programs(2) - 1
    tk = tq // 2
    mq = 2 * tq
    groups = [slice(g * gw, (g + 1) * gw) for g in range(mq // gw)]
    all_groups = tuple(range(len(groups)))
    lane = lax.broadcasted_iota(jnp.int32, (tq, LANES), 1)

    def load_queries(block):
        q = q_ref[pl.ds(pl.multiple_of(block * tq, tq), tq), :]
        zero = jnp.zeros_like(q)
        q2_ref[0:tq, :] = jnp.where(lane < DIFF_HEAD_DIM, q, zero)
        q2_ref[tq:mq, :] = jnp.where(lane >= DIFF_HEAD_DIM, q, zero)

    m_ref[...] = jnp.full(m_ref.shape, -jnp.inf, F32)
    acc_ref[...] = jnp.zeros(acc_ref.shape, F32)

    def visibility(r, g):
        q0 = (g * gw) % tq
        k0 = r * tk
        if k0 + tk - 1 <= q0:
            return "all", None
        if k0 > q0 + gw - 1:
            return "none", None
        return "some", k0 - q0

    def stage_keys(j):
        kc_ref[...] = k_ref[pl.ds(pl.multiple_of(j * tk, tk), tk), :]

    def visible_rows(shift):
        return tk if shift is None else min(tk, gw - shift)

    def produce(g, s_ref, mx_ref, shift=None):
        nr = visible_rows(shift)
        s = lax.dot_general(kc_ref[0:nr, :], q2_ref[groups[g], :], NT_DIMS,
                            preferred_element_type=F32)
        s_ref[g, 0:nr, :] = s
        if shift is None:
            mx_ref[0:1, groups[g]] = jnp.max(s, axis=0, keepdims=True)

    def values(j, nr):
        return jnp.concatenate([vt_ref[j, :, 0:nr], jnp.ones((BF16_ROWS, nr), BF16)], axis=0)

    def consume(j, va, g, s_ref, mx_ref, shift):
        cs = groups[g]
        nr = visible_rows(shift)
        m_old = m_ref[0:1, cs]
        if shift is None:
            s = s_ref[g]
            m_new = jnp.maximum(m_old, mx_ref[0:1, cs])
        else:
            visible = (lax.broadcasted_iota(jnp.int32, (nr, gw), 0) + shift
                       <= lax.broadcasted_iota(jnp.int32, (nr, gw), 1))
            s = jnp.where(visible, s_ref[g, 0:nr, :], -jnp.inf)
            m_new = jnp.maximum(m_old, jnp.max(s, axis=0, keepdims=True))
            va = va if nr == tk else values(j, nr)
        alpha = jnp.exp2(m_old - m_new)
        p = jnp.exp2(s - m_new).astype(BF16)
        m_ref[0:1, cs] = m_new
        acc_ref[g] = alpha * acc_ref[g] + _dot(va, p)

    def step(j, cur, nxt, diag=None):
        va = values(j, tk)
        if diag != 1:
            stage_keys(j + 1)
        for g in all_groups:
            if diag is None:
                produce(g, *nxt)
            elif diag == 0 and visibility(1, g)[0] != "none":
                produce(g, *nxt, shift=visibility(1, g)[1])
            kind, shift = ("all", None) if diag is None else visibility(diag, g)
            if kind != "none":
                consume(j, va, g, *cur, shift=shift)

    buf_a = (sa_ref, mxa_ref)
    buf_b = (sb_ref, mxb_ref)

    @pl.when(i == 0)
    def _():
        load_queries(0)
        stage_keys(0)
        for g in all_groups:
            produce(g, *buf_a)

    def pair(t):
        step(2 * t, buf_a, buf_b)
        step(2 * t + 1, buf_b, buf_a)

    def four_pairs_body(u, carry):
        for t in range(4):
            pair(4 * u + t)
        return carry

    lax.fori_loop(0, i // 4, four_pairs_body, 0)
    done = (i // 4) * 4
    for rest in range(4):
        @pl.when(i % 4 == rest)
        def _(rest=rest):
            for t in range(rest):
                pair(done + t)
            step(2 * i, buf_a, buf_b, diag=0)

    def finish():
        lam_vec = lambda row: _param_row(p_ref, row, DIFF_HEAD_DIM)
        lam = (jnp.exp(jnp.sum(lam_vec(P_LAMBDA_Q1) * lam_vec(P_LAMBDA_K1), axis=-1, keepdims=True))
               - jnp.exp(jnp.sum(lam_vec(P_LAMBDA_Q2) * lam_vec(P_LAMBDA_K2), axis=-1, keepdims=True))
               + lam_init)
        acc = jnp.concatenate([acc_ref[g] for g in all_groups], axis=1)
        num = acc[0:DIFF_V_DIM, :]
        den = acc[DIFF_V_DIM:DIFF_V_DIM + 1, :]
        o_t = num[:, :tq] / den[:, :tq] - lam * (num[:, tq:] / den[:, tq:])
        o = o_t.T
        gain = _param_row(p_ref, P_SUBLN_G, DIFF_V_DIM)
        o_ref[...] = (_rms(o, gain, HEAD_NORM_EPS) * (1.0 - lam_init)).astype(o_ref.dtype)

    @pl.when(i < last_block)
    def _():
        step(2 * i + 1, buf_b, buf_a, diag=1)
        load_queries(i + 1)
        stage_keys(0)
        for g in all_groups:
            produce(g, *buf_a)
        finish()

    @pl.when(i == last_block)
    def _():
        step(2 * i + 1, buf_b, buf_a, diag=1)
        finish()


def _diffattn(params, q, k, vt, tq, gw, lam_init):
    B, S, W = q.shape
    tk = vt.shape[3]
    assert tq == 2 * tk and tq % gw == 0
    mq = 2 * tq
    return pl.pallas_call(
        functools.partial(_attn_body, tq=tq, gw=gw, lam_init=lam_init),
        grid=(B, W // DIFF_V_DIM, S // tq),
        in_specs=[pl.BlockSpec(params.shape, lambda b, h, i: (0, 0)),
                  pl.BlockSpec((None, S, LANES), lambda b, h, i: (b, 0, h)),
                  pl.BlockSpec((None, S, LANES), lambda b, h, i: (b, 0, h)),
                  pl.BlockSpec((None, S // tk, DIFF_V_DIM, tk), lambda b, h, i: (b, 0, h, 0))],
        out_specs=pl.BlockSpec((None, tq, LANES), lambda b, h, i: (b, i, h)),
        out_shape=jax.ShapeDtypeStruct((B, S, W), BF16),
        scratch_shapes=[pltpu.VMEM((mq, LANES), BF16),
                        pltpu.VMEM((8, mq), F32),
                        pltpu.VMEM((mq // gw, DIFF_V_DIM + BF16_ROWS, gw), F32),
                        pltpu.VMEM((mq // gw, tk, gw), F32), pltpu.VMEM((mq // gw, tk, gw), F32),
                        pltpu.VMEM((8, mq), F32), pltpu.VMEM((8, mq), F32),
                        pltpu.VMEM((tk, LANES), BF16)],
        compiler_params=pltpu.CompilerParams(
            dimension_semantics=("arbitrary", "arbitrary", "arbitrary"),
            vmem_limit_bytes=VMEM_LIMIT_BYTES),
        name="diffattn",
    )(params, q, k, vt)


def _retention_consts(n_heads, chunk):
    log_g = np.log1p(-(2.0 ** (-5.0 - np.arange(n_heads, dtype=np.float64))))
    n = np.arange(chunk, dtype=np.float64)
    rel = n[:, None] - n[None, :]
    dmat = np.where(rel >= 0, np.exp(log_g[:, None, None] * np.maximum(rel, 0.0)), 0.0)
    qdec = np.exp(log_g[:, None] * (n + 1.0)[None, :])
    kdec = np.exp(log_g[:, None] * (chunk - 1.0 - n)[None, :])
    cdec = np.exp(log_g * chunk)
    bl = lambda a: np.ascontiguousarray(np.broadcast_to(a[:, :, None], a.shape + (LANES,)))
    cdec_rows = np.repeat(cdec, RET_KEY_DIM).reshape(n_heads // 2, LANES)
    return tuple(jnp.asarray(a, F32) for a in (dmat, bl(qdec), bl(kdec), bl(cdec_rows)))


def _outffn_body(x_ref, a_ref, r_ref, p_ref, wo_ref, wg_ref, wu_ref, wd_ref, o_ref):
    mix = jnp.concatenate([a_ref[...], r_ref[...]], axis=1)
    x1 = x_ref[...] + _dot(mix, wo_ref[...])
    h = _rms(x1, _param_row(p_ref, P_FFN_G), NORM_EPS).astype(BF16)
    gate = _dot(h, wg_ref[...])
    up = _dot(h, wu_ref[...])
    act = (gate * jax.nn.sigmoid(gate) * up).astype(BF16)
    x2 = x1 + _dot(act, wd_ref[...])
    o_ref[...] = _rms(x2, _param_row(p_ref, P_FINAL_G), NORM_EPS)


def _outffn(x2d, a, r, params, wo, wg, wu, wd, tm):
    T, D = x2d.shape
    row = lambda t: (t, 0)
    const = lambda t: (0, 0)
    resident = lambda arr: pl.BlockSpec(arr.shape, const, pipeline_mode=pl.Buffered(1))
    return pl.pallas_call(
        _outffn_body,
        grid=(T // tm,),
        in_specs=[pl.BlockSpec((tm, D), row),
                  pl.BlockSpec((tm, a.shape[1]), row),
                  pl.BlockSpec((tm, r.shape[1]), row),
                  pl.BlockSpec(params.shape, const),
                  resident(wo), resident(wg), resident(wu), resident(wd)],
        out_specs=pl.BlockSpec((tm, D), row),
        out_shape=jax.ShapeDtypeStruct((T, D), F32),
        compiler_params=pltpu.CompilerParams(
            dimension_semantics=("arbitrary",), vmem_limit_bytes=VMEM_LIMIT_BYTES),
        name="outffn",
    )(x2d, a, r, params, wo, wg, wu, wd)


def _rotation_tables(seq, tm):
    d = DIFF_HEAD_DIM
    inv_h = ROPE_THETA ** (-np.arange(0, d, 2, dtype=np.float64) / d)
    lanes_h = np.tile(np.concatenate([inv_h, inv_h]), LANES // d)
    dr = RET_KEY_DIM
    inv_r = 1.0 / (RET_THETA ** np.linspace(0.0, 1.0, dr // 2, dtype=np.float64))
    lanes_r = np.tile(np.repeat(inv_r, 2), LANES // dr)

    def cos_sin(pos):
        out = []
        for freq in (lanes_h, lanes_r):
            ang = pos[:, None] * freq[None, :]
            out += [np.cos(ang), np.sin(ang)]
        return jnp.asarray(np.stack(out), F32)

    starts = np.arange(seq // tm, dtype=np.float64) * tm
    return cos_sin(starts), cos_sin(np.arange(tm, dtype=np.float64))


def kernel(x, attn_norm_g, w_in, lambda_q1, lambda_k1, lambda_q2, lambda_k2, diff_subln_g,
           ret_norm_g, w_out, ffn_norm_g, w_gate, w_up, w_down, final_norm_g):
    B, S, D = x.shape
    assert w_in.shape[0] == 1, "single-layer block only"
    l = 0
    diff_w = D // 2
    ret_w = D - diff_w
    ret_heads = ret_w // RET_V_DIM
    widths = (diff_w, diff_w, diff_w, ret_heads * RET_KEY_DIM, ret_heads * RET_KEY_DIM, ret_w, ret_w)
    tm_in = 512
    tabs = _rotation_tables(S, tm_in)
    ret_chunk = 128
    ret_consts = _retention_consts(ret_heads, ret_chunk)
    xs = x.reshape(B * S, D)
    lam_init = 0.8 - 0.6 * math.exp(-0.3 * l)
    params = _pack_params(D, [attn_norm_g[l], ffn_norm_g[l], final_norm_g, ret_norm_g[l],
                              diff_subln_g[l], lambda_q1[l], lambda_k1[l], lambda_q2[l], lambda_k2[l]])
    later_weights = [w_out[l], w_gate[l], w_up[l], w_down[l]]
    (q, k, vt, r), (wo_b, wg_b, wu_b, wd_b) = _inproj(
        xs, params, w_in[l], tabs, ret_consts, later_weights, S, widths,
        tm=tm_in, tk=512, chunk=ret_chunk)
    tok3 = lambda t: t.reshape(B, S, t.shape[-1])
    a = _diffattn(params, tok3(q), tok3(k), vt, tq=1024, gw=256, lam_init=lam_init)
    out = _outffn(xs, a.reshape(B * S, diff_w), r, params, wo_b, wg_b, wu_b, wd_b, tm=512)
    return out.reshape(B, S, D)
```

```python
import functools
import math

import jax
import jax.numpy as jnp
import numpy as np
from jax import lax
from jax.experimental import pallas as pl
from jax.experimental.pallas import tpu as pltpu

F32 = jnp.float32
BF16 = jnp.bfloat16

DIFF_HEAD_DIM = 64
DIFF_V_DIM = 128
RET_KEY_DIM = 64
RET_V_DIM = 128
ROPE_THETA = 10000.0
RET_THETA = 10000.0
NORM_EPS = 1e-6
HEAD_NORM_EPS = 1e-5
LANES = 128
BF16_ROWS = 16
VMEM_LIMIT_BYTES = 56 * 1024 * 1024
LOG2E = math.log2(math.e)

NT_DIMS = (((1,), (1,)), ((), ()))
TN_DIMS = (((0,), (0,)), ((), ()))


def _dot(a, b):
    return jnp.dot(a, b, preferred_element_type=F32)


def _rms(x, g, eps):
    ms = jnp.mean(x * x, axis=-1, keepdims=True)
    return x * lax.rsqrt(ms + eps) * g


(P_ATTN_G, P_FFN_G, P_FINAL_G, P_RET_G, P_SUBLN_G,
 P_LAMBDA_Q1, P_LAMBDA_K1, P_LAMBDA_Q2, P_LAMBDA_K2) = range(9)
P_ROWS = 16


def _pack_params(d_model, vectors):
    pieces = []
    for v in vectors:
        pieces += [v.astype(F32), jnp.zeros((d_model - v.shape[0],), F32)]
    pieces.append(jnp.zeros(((P_ROWS - len(vectors)) * d_model,), F32))
    return jnp.concatenate(pieces).reshape(P_ROWS, d_model)


def _param_row(p_ref, row, width=None):
    width = p_ref.shape[1] if width is None else width
    return p_ref[row:row + 1, 0:width]


def _cast_plan(weights, steps):
    specs, shapes, n_blocks = [], [], []
    for w in weights:
        rows, cols = w.shape
        rb = BF16_ROWS
        while rows % rb or rows // rb > steps:
            rb += BF16_ROWS
        nb = rows // rb
        specs.append(pl.BlockSpec((rb, cols), lambda t, nb=nb: (jnp.minimum(t, nb - 1), 0)))
        shapes.append(jax.ShapeDtypeStruct(w.shape, BF16))
        n_blocks.append(nb)
    return specs, shapes, tuple(n_blocks)


def _cast_step(step, w_refs, wb_refs, n_blocks):
    for w_ref, wb_ref, nb in zip(w_refs, wb_refs, n_blocks):
        @pl.when(step < nb)
        def _(w_ref=w_ref, wb_ref=wb_ref):
            wb_ref[...] = w_ref[...].astype(wb_ref.dtype)


def _retention_tile(rq_ref, rk_ref, rv_ref, rg_ref, o_ref, state_ref,
                    dmat_ref, qdec_ref, kdec_ref, cdec_ref, gain, chunk):
    n_chunks = rq_ref.shape[0] // chunk
    n_pairs = rq_ref.shape[1] // LANES
    pair_v = 2 * RET_V_DIM
    rows = [slice(c * chunk, (c + 1) * chunk) for c in range(n_chunks)]
    key_lane = lax.broadcasted_iota(jnp.int32, (chunk, LANES), 1)
    val_lane = lax.broadcasted_iota(jnp.int32, (chunk, pair_v), 1)
    on_diagonal = ((lax.broadcasted_iota(jnp.int32, (LANES, pair_v), 0) < RET_KEY_DIM)
                   == (lax.broadcasted_iota(jnp.int32, (LANES, pair_v), 1) < RET_V_DIM))

    def stack_heads(x, lane, split):
        zero = jnp.zeros_like(x)
        return jnp.concatenate([jnp.where(lane < split, x, zero), jnp.where(lane >= split, x, zero)],
                               axis=0)

    scores, updates = {}, {}
    for c in range(n_chunks):
        for p in range(n_pairs):
            qp = rq_ref[rows[c], p * LANES:(p + 1) * LANES]
            kp = rk_ref[rows[c], p * LANES:(p + 1) * LANES]
            v2 = rv_ref[rows[c], p * pair_v:(p + 1) * pair_v]
            s = lax.dot_general(qp, stack_heads(kp, key_lane, RET_KEY_DIM), NT_DIMS,
                                preferred_element_type=F32)
            scores[c, p] = (s * dmat_ref[p]).astype(BF16)
            vd = (v2.astype(F32) * kdec_ref[p]).astype(BF16)
            updates[c, p] = lax.dot_general(kp, vd, TN_DIMS, preferred_element_type=F32)

    states = {}
    for p in range(n_pairs):
        st = state_ref[p]
        for c in range(n_chunks):
            states[c, p] = st.astype(BF16)
            st = cdec_ref[p] * st + jnp.where(on_diagonal, updates[c, p], 0.0)
        state_ref[p] = st

    yield

    for c in range(n_chunks):
        for p in range(n_pairs):
            qp = rq_ref[rows[c], p * LANES:(p + 1) * LANES]
            v2 = rv_ref[rows[c], p * pair_v:(p + 1) * pair_v]
            cross = _dot(qp, states[c, p]) * qdec_ref[p]
            o2 = _dot(scores[c, p], stack_heads(v2, val_lane, RET_V_DIM)) + cross
            for hh in range(2):
                hs = slice((2 * p + hh) * RET_V_DIM, (2 * p + hh + 1) * RET_V_DIM)
                o = o2[:, hh * RET_V_DIM:(hh + 1) * RET_V_DIM]
                mu = jnp.mean(o, axis=-1, keepdims=True)
                cen = o - mu
                var = jnp.mean(cen * cen, axis=-1, keepdims=True)
                y = cen * lax.rsqrt(var + HEAD_NORM_EPS) * gain[:, hs]
                gate = rg_ref[rows[c], hs]
                o_ref[rows[c], hs] = (gate * jax.nn.sigmoid(gate) * y).astype(o_ref.dtype)


def _inproj_body(x_ref, p_ref, w_ref, base_ref, off_ref,
                 dmat_ref, qdec_ref, kdec_ref, cdec_ref, *refs,
                 widths, tiles_per_seq, chunk, cast_blocks):
    n_cast = len(cast_blocks)
    cast_in, refs = refs[:n_cast], refs[n_cast:]
    (q_ref, k_ref, vt_ref, r_ref), refs = refs[:4], refs[4:]
    cast_out, refs = refs[:n_cast], refs[n_cast:]
    rq_ref, rk_ref, rv_ref, rg_ref, state_ref, wb_ref, wvt_ref = refs
    dq_w, dk_w, dv_w, rq_w, rk_w, rv_w, rg_w = widths
    tile_in_seq = pl.program_id(0) % tiles_per_seq
    _cast_step(pl.program_id(0), cast_in, cast_out, cast_blocks)

    @pl.when(pl.program_id(0) == 0)
    def _():
        c_v = dq_w + dk_w
        for c0 in range(0, w_ref.shape[1], dv_w):
            if c0 == c_v:
                wvt_ref[...] = w_ref[:, c0:c0 + dv_w].T.astype(wvt_ref.dtype)
            else:
                wb_ref[:, c0:c0 + dv_w] = w_ref[:, c0:c0 + dv_w].astype(wb_ref.dtype)

    @pl.when(tile_in_seq == 0)
    def _():
        state_ref[...] = jnp.zeros(state_ref.shape, F32)

    h = _rms(x_ref[...], _param_row(p_ref, P_ATTN_G), NORM_EPS).astype(BF16)
    lane = lax.broadcasted_iota(jnp.int32, off_ref.shape[1:], 1)
    first_half = (lane & (DIFF_HEAD_DIM // 2)) == 0
    even = (lane & 1) == 0

    def angle_tables(kind, negate_where):
        cb = base_ref[2 * kind, pl.ds(tile_in_seq, 1), :]
        sb = base_ref[2 * kind + 1, pl.ds(tile_in_seq, 1), :]
        co, so = off_ref[2 * kind], off_ref[2 * kind + 1]
        sin = sb * co + cb * so
        return cb * co - sb * so, jnp.where(negate_where, -sin, sin)

    ch, sh = angle_tables(0, first_half)
    ci, si = angle_tables(1, even)

    def rope(xc):
        sw = jnp.where(first_half, pltpu.roll(xc, LANES - 32, 1), pltpu.roll(xc, 32, 1))
        return xc * ch + sw * sh

    def pair_rot(xc):
        sw = jnp.where(even, pltpu.roll(xc, LANES - 1, 1), pltpu.roll(xc, 1, 1))
        return xc * ci + sw * si

    def emit(out_ref, c0, width, fn):
        p = _dot(h, wb_ref[:, c0:c0 + width])
        for j in range(width // LANES):
            sl = slice(j * LANES, (j + 1) * LANES)
            out_ref[:, sl] = fn(p[:, sl]).astype(out_ref.dtype)

    c_q, c_k = 0, dq_w
    c_rq = dq_w + dk_w + dv_w
    c_rk, c_rv, c_rg = c_rq + rq_w, c_rq + rq_w + rk_w, c_rq + rq_w + rk_w + rv_w
    emit(rq_ref, c_rq, rq_w, pair_rot)
    emit(rk_ref, c_rk, rk_w, lambda t: pair_rot(t) * (RET_KEY_DIM ** -0.5))
    emit(rv_ref, c_rv, rv_w, lambda t: t)
    emit(rg_ref, c_rg, rg_w, lambda t: t)
    retention = _retention_tile(rq_ref, rk_ref, rv_ref, rg_ref, r_ref, state_ref,
                                dmat_ref, qdec_ref, kdec_ref, cdec_ref,
                                _param_row(p_ref, P_RET_G, rv_w), chunk)
    next(retention)
    emit(q_ref, c_q, dq_w, lambda t: rope(t) * (DIFF_HEAD_DIM ** -0.5 * LOG2E))
    next(retention, None)
    emit(k_ref, c_k, dk_w, rope)
    vt = lax.dot_general(wvt_ref[...], h, NT_DIMS, preferred_element_type=F32)
    tkc = vt_ref.shape[-1]
    for j in range(vt_ref.shape[0]):
        vt_ref[j] = vt[:, j * tkc:(j + 1) * tkc].astype(vt_ref.dtype)


def _inproj(x2d, params, w, tabs, ret_consts, cast_weights, seq, widths, tm, tk, chunk):
    T, D = x2d.shape
    cast_specs, cast_shapes, cast_blocks = _cast_plan(cast_weights, T // tm)
    P = w.shape[1]
    nps = seq // tm
    dq_w, dk_w, dv_w, rq_w, rk_w, rv_w, rg_w = widths
    row = lambda t: (t, 0)
    const = lambda t: (0, 0)
    full = lambda a: pl.BlockSpec(a.shape, lambda t: (0,) * a.ndim)
    out_shape = [jax.ShapeDtypeStruct((T, dq_w), BF16),
                 jax.ShapeDtypeStruct((T, dk_w), BF16),
                 jax.ShapeDtypeStruct((T // seq, seq // tk, dv_w, tk), BF16),
                 jax.ShapeDtypeStruct((T, rv_w), BF16)]
    out_specs = [pl.BlockSpec((tm, dq_w), row),
                 pl.BlockSpec((tm, dk_w), row),
                 pl.BlockSpec((None, tm // tk, dv_w, tk), lambda t: (t // nps, t % nps, 0, 0)),
                 pl.BlockSpec((tm, rv_w), row)]
    outs = pl.pallas_call(
        functools.partial(_inproj_body, widths=widths, tiles_per_seq=nps, chunk=chunk,
                          cast_blocks=cast_blocks),
        grid=(T // tm,),
        in_specs=[pl.BlockSpec((tm, D), row),
                  pl.BlockSpec(params.shape, const),
                  pl.BlockSpec((D, P), const, pipeline_mode=pl.Buffered(1))]
                 + [full(a) for a in tabs] + [full(a) for a in ret_consts]
                 + cast_specs,
        out_specs=out_specs + cast_specs,
        out_shape=out_shape + cast_shapes,
        scratch_shapes=[pltpu.VMEM((tm, rq_w), BF16), pltpu.VMEM((tm, rk_w), BF16),
                        pltpu.VMEM((tm, rv_w), BF16), pltpu.VMEM((tm, rg_w), F32),
                        pltpu.VMEM((rq_w // LANES, LANES, 2 * RET_V_DIM), F32),
                        pltpu.VMEM((D, P), BF16), pltpu.VMEM((dv_w, D), BF16)],
        compiler_params=pltpu.CompilerParams(
            dimension_semantics=("arbitrary",), vmem_limit_bytes=VMEM_LIMIT_BYTES),
        name="inproj",
    )(x2d, params, w, *tabs, *ret_consts, *cast_weights)
    return outs[:4], outs[4:]


def _attn_body(p_ref, q_ref, k_ref, vt_ref, o_ref,
               q2_ref, m_ref, acc_ref, sa_ref, sb_ref, mxa_ref, mxb_ref, kc_ref,
               *, tq, gw, lam_init):
    i = pl.program_id(2)
    last_block = pl.num_programs(2) - 1
    tk = tq // 2
    mq = 2 * tq
    groups = [slice(g * gw, (g + 1) * gw) for g in range(mq // gw)]
    all_groups = tuple(range(len(groups)))
    lane = lax.broadcasted_iota(jnp.int32, (tq, LANES), 1)

    def load_queries(block):
        q = q_ref[pl.ds(pl.multiple_of(block * tq, tq), tq), :]
        zero = jnp.zeros_like(q)
        q2_ref[0:tq, :] = jnp.where(lane < DIFF_HEAD_DIM, q, zero)
        q2_ref[tq:mq, :] = jnp.where(lane >= DIFF_HEAD_DIM, q, zero)

    m_ref[...] = jnp.full(m_ref.shape, -jnp.inf, F32)
    acc_ref[...] = jnp.zeros(acc_ref.shape, F32)

    def visibility(r, g):
        q0 = (g * gw) % tq
        k0 = r * tk
        if k0 + tk - 1 <= q0:
            return "all", None
        if k0 > q0 + gw - 1:
            return "none", None
        return "some", k0 - q0

    def stage_keys(j):
        kc_ref[...] = k_ref[pl.ds(pl.multiple_of(j * tk, tk), tk), :]

    def visible_rows(shift):
        return tk if shift is None else min(tk, gw - shift)

    def produce(g, s_ref, mx_ref, shift=None):
        nr = visible_rows(shift)
        s = lax.dot_general(kc_ref[0:nr, :], q2_ref[groups[g], :], NT_DIMS,
                            preferred_element_type=F32)
        s_ref[g, 0:nr, :] = s
        if shift is None:
            mx_ref[0:1, groups[g]] = jnp.max(s, axis=0, keepdims=True)

    def values(j, nr):
        return jnp.concatenate([vt_ref[j, :, 0:nr], jnp.ones((BF16_ROWS, nr), BF16)], axis=0)

    def consume(j, va, g, s_ref, mx_ref, shift):
        cs = groups[g]
        nr = visible_rows(shift)
        m_old = m_ref[0:1, cs]
        if shift is None:
            s = s_ref[g]
            m_new = jnp.maximum(m_old, mx_ref[0:1, cs])
        else:
            visible = (lax.broadcasted_iota(jnp.int32, (nr, gw), 0) + shift
                       <= lax.broadcasted_iota(jnp.int32, (nr, gw), 1))
            s = jnp.where(visible, s_ref[g, 0:nr, :], -jnp.inf)
            m_new = jnp.maximum(m_old, jnp.max(s, axis=0, keepdims=True))
            va = va if nr == tk else values(j, nr)
        alpha = jnp.exp2(m_old - m_new)
        p = jnp.exp2(s - m_new).astype(BF16)
        m_ref[0:1, cs] = m_new
        acc_ref[g] = alpha * acc_ref[g] + _dot(va, p)

    def step(j, cur, nxt, diag=None):
        va = values(j, tk)
        if diag != 1:
            stage_keys(j + 1)
        for g in all_groups:
            if diag is None:
                produce(g, *nxt)
            elif diag == 0 and visibility(1, g)[0] != "none":
                produce(g, *nxt, shift=visibility(1, g)[1])
            kind, shift = ("all", None) if diag is None else visibility(diag, g)
            if kind != "none":
                consume(j, va, g, *cur, shift=shift)

    buf_a = (sa_ref, mxa_ref)
    buf_b = (sb_ref, mxb_ref)

    @pl.when(i == 0)
    def _():
        load_queries(0)
        stage_keys(0)
        for g in all_groups:
            produce(g, *buf_a)

    def pair(t):
        step(2 * t, buf_a, buf_b)
        step(2 * t + 1, buf_b, buf_a)

    def four_pairs_body(u, carry):
        for t in range(4):
            pair(4 * u + t)
        return carry

    lax.fori_loop(0, i // 4, four_pairs_body, 0)
    done = (i // 4) * 4
    for rest in range(4):
        @pl.when(i % 4 == rest)
        def _(rest=rest):
            for t in range(rest):
                pair(done + t)
            step(2 * i, buf_a, buf_b, diag=0)

    def finish():
        lam_vec = lambda row: _param_row(p_ref, row, DIFF_HEAD_DIM)
        lam = (jnp.exp(jnp.sum(lam_vec(P_LAMBDA_Q1) * lam_vec(P_LAMBDA_K1), axis=-1, keepdims=True))
               - jnp.exp(jnp.sum(lam_vec(P_LAMBDA_Q2) * lam_vec(P_LAMBDA_K2), axis=-1, keepdims=True))
               + lam_init)
        acc = jnp.concatenate([acc_ref[g] for g in all_groups], axis=1)
        num = acc[0:DIFF_V_DIM, :]
        den = acc[DIFF_V_DIM:DIFF_V_DIM + 1, :]
        o_t = num[:, :tq] / den[:, :tq] - lam * (num[:, tq:] / den[:, tq:])
        o = o_t.T
        gain = _param_row(p_ref, P_SUBLN_G, DIFF_V_DIM)
        o_ref[...] = (_rms(o, gain, HEAD_NORM_EPS) * (1.0 - lam_init)).astype(o_ref.dtype)

    @pl.when(i < last_block)
    def _():
        step(2 * i + 1, buf_b, buf_a, diag=1)
        load_queries(i + 1)
        stage_keys(0)
        for g in all_groups:
            produce(g, *buf_a)
        finish()

    @pl.when(i == last_block)
    def _():
        step(2 * i + 1, buf_b, buf_a, diag=1)
        finish()


def _diffattn(params, q, k, vt, tq, gw, lam_init):
    B, S, W = q.shape
    tk = vt.shape[3]
    assert tq == 2 * tk and tq % gw == 0
    mq = 2 * tq
    return pl.pallas_call(
        functools.partial(_attn_body, tq=tq, gw=gw, lam_init=lam_init),
        grid=(B, W // DIFF_V_DIM, S // tq),
        in_specs=[pl.BlockSpec(params.shape, lambda b, h, i: (0, 0)),
                  pl.BlockSpec((None, S, LANES), lambda b, h, i: (b, 0, h)),
                  pl.BlockSpec((None, S, LANES), lambda b, h, i: (b, 0, h)),
                  pl.BlockSpec((None, S // tk, DIFF_V_DIM, tk), lambda b, h, i: (b, 0, h, 0))],
        out_specs=pl.BlockSpec((None, tq, LANES), lambda b, h, i: (b, i, h)),
        out_shape=jax.ShapeDtypeStruct((B, S, W), BF16),
        scratch_shapes=[pltpu.VMEM((mq, LANES), BF16),
                        pltpu.VMEM((8, mq), F32),
                        pltpu.VMEM((mq // gw, DIFF_V_DIM + BF16_ROWS, gw), F32),
                        pltpu.VMEM((mq // gw, tk, gw), F32), pltpu.VMEM((mq // gw, tk, gw), F32),
                        pltpu.VMEM((8, mq), F32), pltpu.VMEM((8, mq), F32),
                        pltpu.VMEM((tk, LANES), BF16)],
        compiler_params=pltpu.CompilerParams(
            dimension_semantics=("arbitrary", "arbitrary", "arbitrary"),
            vmem_limit_bytes=VMEM_LIMIT_BYTES),
        name="diffattn",
    )(params, q, k, vt)


def _retention_consts(n_heads, chunk):
    log_g = np.log1p(-(2.0 ** (-5.0 - np.arange(n_heads, dtype=np.float64))))
    n = np.arange(chunk, dtype=np.float64)
    rel = n[:, None] - n[None, :]
    dmat = np.where(rel >= 0, np.exp(log_g[:, None, None] * np.maximum(rel, 0.0)), 0.0)
    qdec = np.exp(log_g[:, None] * (n + 1.0)[None, :])
    kdec = np.exp(log_g[:, None] * (chunk - 1.0 - n)[None, :])
    cdec = np.exp(log_g * chunk)
    pairs = n_heads // 2
    side_by_side = lambda a: a.reshape(pairs, 2, *a.shape[1:])
    dmat2 = np.concatenate(list(side_by_side(dmat).transpose(1, 0, 2, 3)), axis=-1)
    per_row = lambda a: np.repeat(side_by_side(a).transpose(0, 2, 1), RET_V_DIM, axis=-1)
    cdec2 = np.broadcast_to(np.repeat(cdec.reshape(pairs, 2), RET_KEY_DIM, axis=1)[:, :, None],
                            (pairs, 2 * RET_KEY_DIM, 2 * RET_V_DIM))
    return tuple(jnp.asarray(np.ascontiguousarray(a), F32)
                 for a in (dmat2, per_row(qdec), per_row(kdec), cdec2))


def _outffn_body(x_ref, a_ref, r_ref, p_ref, wo_ref, wg_ref, wu_ref, wd_ref, o_ref):
    mix = jnp.concatenate([a_ref[...], r_ref[...]], axis=1)
    x1 = x_ref[...] + _dot(mix, wo_ref[...])
    h = _rms(x1, _param_row(p_ref, P_FFN_G), NORM_EPS).astype(BF16)
    gate = _dot(h, wg_ref[...])
    up = _dot(h, wu_ref[...])
    act = (gate * jax.nn.sigmoid(gate) * up).astype(BF16)
    x2 = x1 + _dot(act, wd_ref[...])
    o_ref[...] = _rms(x2, _param_row(p_ref, P_FINAL_G), NORM_EPS)


def _outffn(x2d, a, r, params, wo, wg, wu, wd, tm):
    T, D = x2d.shape
    row = lambda t: (t, 0)
    const = lambda t: (0, 0)
    resident = lambda arr: pl.BlockSpec(arr.shape, const, pipeline_mode=pl.Buffered(1))
    return pl.pallas_call(
        _outffn_body,
        grid=(T // tm,),
        in_specs=[pl.BlockSpec((tm, D), row),
                  pl.BlockSpec((tm, a.shape[1]), row),
                  pl.BlockSpec((tm, r.shape[1]), row),
                  pl.BlockSpec(params.shape, const),
                  resident(wo), resident(wg), resident(wu), resident(wd)],
        out_specs=pl.BlockSpec((tm, D), row),
        out_shape=jax.ShapeDtypeStruct((T, D), F32),
        compiler_params=pltpu.CompilerParams(
            dimension_semantics=("arbitrary",), vmem_limit_bytes=VMEM_LIMIT_BYTES),
        name="outffn",
    )(x2d, a, r, params, wo, wg, wu, wd)


def _rotation_tables(seq, tm):
    d = DIFF_HEAD_DIM
    inv_h = ROPE_THETA ** (-np.arange(0, d, 2, dtype=np.float64) / d)
    lanes_h = np.tile(np.concatenate([inv_h, inv_h]), LANES // d)
    dr = RET_KEY_DIM
    inv_r = 1.0 / (RET_THETA ** np.linspace(0.0, 1.0, dr // 2, dtype=np.float64))
    lanes_r = np.tile(np.repeat(inv_r, 2), LANES // dr)

    def cos_sin(pos):
        out = []
        for freq in (lanes_h, lanes_r):
            ang = pos[:, None] * freq[None, :]
            out += [np.cos(ang), np.sin(ang)]
        return jnp.asarray(np.stack(out), F32)

    starts = np.arange(seq // tm, dtype=np.float64) * tm
    return cos_sin(starts), cos_sin(np.arange(tm, dtype=np.float64))


def kernel(x, attn_norm_g, w_in, lambda_q1, lambda_k1, lambda_q2, lambda_k2, diff_subln_g,
           ret_norm_g, w_out, ffn_norm_g, w_gate, w_up, w_down, final_norm_g):
    B, S, D = x.shape
    assert w_in.shape[0] == 1, "single-layer block only"
    l = 0
    diff_w = D // 2
    ret_w = D - diff_w
    ret_heads = ret_w // RET_V_DIM
    widths = (diff_w, diff_w, diff_w, ret_heads * RET_KEY_DIM, ret_heads * RET_KEY_DIM, ret_w, ret_w)
    tm_in = 512
    tabs = _rotation_tables(S, tm_in)
    ret_chunk = 128
    ret_consts = _retention_consts(ret_heads, ret_chunk)
    xs = x.reshape(B * S, D)
    lam_init = 0.8 - 0.6 * math.exp(-0.3 * l)
    params = _pack_params(D, [attn_norm_g[l], ffn_norm_g[l], final_norm_g, ret_norm_g[l],
                              diff_subln_g[l], lambda_q1[l], lambda_k1[l], lambda_q2[l], lambda_k2[l]])
    later_weights = [w_out[l], w_gate[l], w_up[l], w_down[l]]
    (q, k, vt, r), (wo_b, wg_b, wu_b, wd_b) = _inproj(
        xs, params, w_in[l], tabs, ret_consts, later_weights, S, widths,
        tm=tm_in, tk=512, chunk=ret_chunk)
    tok3 = lambda t: t.reshape(B, S, t.shape[-1])
    a = _diffattn(params, tok3(q), tok3(k), vt, tq=1024, gw=256, lam_init=lam_init)
    out = _outffn(xs, a.reshape(B * S, diff_w), r, params, wo_b, wg_b, wu_b, wd_b, tm=512)
    return out.reshape(B, S, D)
```

```python
import functools
import math

import jax
import jax.numpy as jnp
import numpy as np
from jax import lax
from jax.experimental import pallas as pl
from jax.experimental.pallas import tpu as pltpu

F32 = jnp.float32
BF16 = jnp.bfloat16

DIFF_HEAD_DIM = 64
DIFF_V_DIM = 128
RET_KEY_DIM = 64
RET_V_DIM = 128
ROPE_THETA = 10000.0
RET_THETA = 10000.0
NORM_EPS = 1e-6
HEAD_NORM_EPS = 1e-5
LANES = 128
BF16_ROWS = 16
VMEM_LIMIT_BYTES = 56 * 1024 * 1024
LOG2E = math.log2(math.e)
FFN_SLAB = 1024

NT_DIMS = (((1,), (1,)), ((), ()))
TN_DIMS = (((0,), (0,)), ((), ()))


def _dot(a, b):
    return jnp.dot(a, b, preferred_element_type=F32)


def _rms(x, g, eps):
    ms = jnp.mean(x * x, axis=-1, keepdims=True)
    return x * lax.rsqrt(ms + eps) * g


(P_ATTN_G, P_FFN_G, P_FINAL_G, P_RET_G, P_SUBLN_G,
 P_LAMBDA_Q1, P_LAMBDA_K1, P_LAMBDA_Q2, P_LAMBDA_K2) = range(9)
P_ROWS = 16


def _pack_params(d_model, vectors):
    pieces = []
    for v in vectors:
        pieces += [v.astype(F32), jnp.zeros((d_model - v.shape[0],), F32)]
    pieces.append(jnp.zeros(((P_ROWS - len(vectors)) * d_model,), F32))
    return jnp.concatenate(pieces).reshape(P_ROWS, d_model)


def _param_row(p_ref, row, width=None):
    width = p_ref.shape[1] if width is None else width
    return p_ref[row:row + 1, 0:width]


def _cast_plan(weights, steps):
    specs, shapes, n_blocks = [], [], []
    for w in weights:
        rows, cols = w.shape
        rb = BF16_ROWS
        while rows % rb or rows // rb > steps:
            rb += BF16_ROWS
        nb = rows // rb
        specs.append(pl.BlockSpec((rb, cols), lambda t, nb=nb: (jnp.minimum(t, nb - 1), 0)))
        shapes.append(jax.ShapeDtypeStruct(w.shape, BF16))
        n_blocks.append(nb)
    return specs, shapes, tuple(n_blocks)


def _cast_step(step, w_refs, wb_refs, n_blocks):
    for w_ref, wb_ref, nb in zip(w_refs, wb_refs, n_blocks):
        @pl.when(step < nb)
        def _(w_ref=w_ref, wb_ref=wb_ref):
            wb_ref[...] = w_ref[...].astype(wb_ref.dtype)


def _retention_tile(rq_ref, rk_ref, rv_ref, rg_ref, o_ref, state_ref,
                    dmat_ref, qdec_ref, kdec_ref, cdec_ref, gain, chunk):
    n_chunks = rq_ref.shape[0] // chunk
    n_pairs = rq_ref.shape[1] // LANES
    pair_v = 2 * RET_V_DIM
    rows = [slice(c * chunk, (c + 1) * chunk) for c in range(n_chunks)]
    key_lane = lax.broadcasted_iota(jnp.int32, (chunk, LANES), 1)
    val_lane = lax.broadcasted_iota(jnp.int32, (chunk, pair_v), 1)
    on_diagonal = ((lax.broadcasted_iota(jnp.int32, (LANES, pair_v), 0) < RET_KEY_DIM)
                   == (lax.broadcasted_iota(jnp.int32, (LANES, pair_v), 1) < RET_V_DIM))

    def stack_heads(x, lane, split):
        zero = jnp.zeros_like(x)
        return jnp.concatenate([jnp.where(lane < split, x, zero), jnp.where(lane >= split, x, zero)],
                               axis=0)

    scores, updates = {}, {}
    for c in range(n_chunks):
        for p in range(n_pairs):
            qp = rq_ref[rows[c], p * LANES:(p + 1) * LANES]
            kp = rk_ref[rows[c], p * LANES:(p + 1) * LANES]
            v2 = rv_ref[rows[c], p * pair_v:(p + 1) * pair_v]
            s = lax.dot_general(qp, stack_heads(kp, key_lane, RET_KEY_DIM), NT_DIMS,
                                preferred_element_type=F32)
            scores[c, p] = (s * dmat_ref[p]).astype(BF16)
            vd = (v2.astype(F32) * kdec_ref[p]).astype(BF16)
            updates[c, p] = lax.dot_general(kp, vd, TN_DIMS, preferred_element_type=F32)

    states = {}
    for p in range(n_pairs):
        st = state_ref[p]
        for c in range(n_chunks):
            states[c, p] = st.astype(BF16)
            st = cdec_ref[p] * st + jnp.where(on_diagonal, updates[c, p], 0.0)
        state_ref[p] = st

    yield

    for c in range(n_chunks):
        for p in range(n_pairs):
            qp = rq_ref[rows[c], p * LANES:(p + 1) * LANES]
            v2 = rv_ref[rows[c], p * pair_v:(p + 1) * pair_v]
            cross = _dot(qp, states[c, p]) * qdec_ref[p]
            o2 = _dot(scores[c, p], stack_heads(v2, val_lane, RET_V_DIM)) + cross
            for hh in range(2):
                hs = slice((2 * p + hh) * RET_V_DIM, (2 * p + hh + 1) * RET_V_DIM)
                o = o2[:, hh * RET_V_DIM:(hh + 1) * RET_V_DIM]
                mu = jnp.mean(o, axis=-1, keepdims=True)
                cen = o - mu
                var = jnp.mean(cen * cen, axis=-1, keepdims=True)
                y = cen * lax.rsqrt(var + HEAD_NORM_EPS) * gain[:, hs]
                gate = rg_ref[rows[c], hs]
                o_ref[rows[c], hs] = (gate * jax.nn.sigmoid(gate) * y).astype(o_ref.dtype)


def _inproj_body(x_ref, p_ref, w_ref, base_ref, off_ref,
                 dmat_ref, qdec_ref, kdec_ref, cdec_ref, *refs,
                 widths, tiles_per_seq, chunk, cast_blocks):
    n_cast = len(cast_blocks)
    cast_in, refs = refs[:n_cast], refs[n_cast:]
    (q_ref, k_ref, vt_ref, r_ref), refs = refs[:4], refs[4:]
    cast_out, refs = refs[:n_cast], refs[n_cast:]
    rq_ref, rk_ref, rv_ref, rg_ref, state_ref, wb_ref, wvt_ref = refs
    dq_w, dk_w, dv_w, rq_w, rk_w, rv_w, rg_w = widths
    tile_in_seq = pl.program_id(0) % tiles_per_seq
    _cast_step(pl.program_id(0), cast_in, cast_out, cast_blocks)

    @pl.when(pl.program_id(0) == 0)
    def _():
        c_v = dq_w + dk_w
        for c0 in range(0, w_ref.shape[1], dv_w):
            if c0 == c_v:
                wvt_ref[...] = w_ref[:, c0:c0 + dv_w].T.astype(wvt_ref.dtype)
            else:
                wb_ref[:, c0:c0 + dv_w] = w_ref[:, c0:c0 + dv_w].astype(wb_ref.dtype)

    @pl.when(tile_in_seq == 0)
    def _():
        state_ref[...] = jnp.zeros(state_ref.shape, F32)

    h = _rms(x_ref[...], _param_row(p_ref, P_ATTN_G), NORM_EPS).astype(BF16)
    lane = lax.broadcasted_iota(jnp.int32, off_ref.shape[1:], 1)
    first_half = (lane & (DIFF_HEAD_DIM // 2)) == 0
    even = (lane & 1) == 0

    def angle_tables(kind, negate_where):
        cb = base_ref[2 * kind, pl.ds(tile_in_seq, 1), :]
        sb = base_ref[2 * kind + 1, pl.ds(tile_in_seq, 1), :]
        co, so = off_ref[2 * kind], off_ref[2 * kind + 1]
        sin = sb * co + cb * so
        return cb * co - sb * so, jnp.where(negate_where, -sin, sin)

    ch, sh = angle_tables(0, first_half)
    ci, si = angle_tables(1, even)

    def rope(xc):
        sw = jnp.where(first_half, pltpu.roll(xc, LANES - 32, 1), pltpu.roll(xc, 32, 1))
        return xc * ch + sw * sh

    def pair_rot(xc):
        sw = jnp.where(even, pltpu.roll(xc, LANES - 1, 1), pltpu.roll(xc, 1, 1))
        return xc * ci + sw * si

    def emit(out_ref, c0, width, fn):
        p = _dot(h, wb_ref[:, c0:c0 + width])
        for j in range(width // LANES):
            sl = slice(j * LANES, (j + 1) * LANES)
            out_ref[:, sl] = fn(p[:, sl]).astype(out_ref.dtype)

    c_q, c_k = 0, dq_w
    c_rq = dq_w + dk_w + dv_w
    c_rk, c_rv, c_rg = c_rq + rq_w, c_rq + rq_w + rk_w, c_rq + rq_w + rk_w + rv_w
    emit(rq_ref, c_rq, rq_w, pair_rot)
    emit(rk_ref, c_rk, rk_w, lambda t: pair_rot(t) * (RET_KEY_DIM ** -0.5))
    emit(rv_ref, c_rv, rv_w, lambda t: t)
    emit(rg_ref, c_rg, rg_w, lambda t: t)
    retention = _retention_tile(rq_ref, rk_ref, rv_ref, rg_ref, r_ref, state_ref,
                                dmat_ref, qdec_ref, kdec_ref, cdec_ref,
                                _param_row(p_ref, P_RET_G, rv_w), chunk)
    next(retention)
    emit(q_ref, c_q, dq_w, lambda t: rope(t) * (DIFF_HEAD_DIM ** -0.5 * LOG2E))
    next(retention, None)
    emit(k_ref, c_k, dk_w, rope)
    vt = lax.dot_general(wvt_ref[...], h, NT_DIMS, preferred_element_type=F32)
    tkc = vt_ref.shape[-1]
    for j in range(vt_ref.shape[0]):
        vt_ref[j] = vt[:, j * tkc:(j + 1) * tkc].astype(vt_ref.dtype)


def _inproj(x2d, params, w, tabs, ret_consts, cast_weights, seq, widths, tm, tk, chunk):
    T, D = x2d.shape
    cast_specs, cast_shapes, cast_blocks = _cast_plan(cast_weights, T // tm)
    P = w.shape[1]
    nps = seq // tm
    dq_w, dk_w, dv_w, rq_w, rk_w, rv_w, rg_w = widths
    row = lambda t: (t, 0)
    const = lambda t: (0, 0)
    full = lambda a: pl.BlockSpec(a.shape, lambda t: (0,) * a.ndim)
    out_shape = [jax.ShapeDtypeStruct((T, dq_w), BF16),
                 jax.ShapeDtypeStruct((T, dk_w), BF16),
                 jax.ShapeDtypeStruct((T // seq, seq // tk, dv_w, tk), BF16),
                 jax.ShapeDtypeStruct((T, rv_w), BF16)]
    out_specs = [pl.BlockSpec((tm, dq_w), row),
                 pl.BlockSpec((tm, dk_w), row),
                 pl.BlockSpec((None, tm // tk, dv_w, tk), lambda t: (t // nps, t % nps, 0, 0)),
                 pl.BlockSpec((tm, rv_w), row)]
    outs = pl.pallas_call(
        functools.partial(_inproj_body, widths=widths, tiles_per_seq=nps, chunk=chunk,
                          cast_blocks=cast_blocks),
        grid=(T // tm,),
        in_specs=[pl.BlockSpec((tm, D), row),
                  pl.BlockSpec(params.shape, const),
                  pl.BlockSpec((D, P), const, pipeline_mode=pl.Buffered(1))]
                 + [full(a) for a in tabs] + [full(a) for a in ret_consts]
                 + cast_specs,
        out_specs=out_specs + cast_specs,
        out_shape=out_shape + cast_shapes,
        scratch_shapes=[pltpu.VMEM((tm, rq_w), BF16), pltpu.VMEM((tm, rk_w), BF16),
                        pltpu.VMEM((tm, rv_w), BF16), pltpu.VMEM((tm, rg_w), F32),
                        pltpu.VMEM((rq_w // LANES, LANES, 2 * RET_V_DIM), F32),
                        pltpu.VMEM((D, P), BF16), pltpu.VMEM((dv_w, D), BF16)],
        compiler_params=pltpu.CompilerParams(
            dimension_semantics=("arbitrary",), vmem_limit_bytes=VMEM_LIMIT_BYTES),
        name="inproj",
    )(x2d, params, w, *tabs, *ret_consts, *cast_weights)
    return outs[:4], outs[4:]


def _attn_body(p_ref, q_ref, k_ref, vt_ref, o_ref,
               q2_ref, m_ref, acc_ref, sa_ref, sb_ref, mxa_ref, mxb_ref, kc_ref,
               *, tq, gw, lam_init):
    i = pl.program_id(2)
    last_block = pl.num_programs(2) - 1
    tk = tq // 2
    mq = 2 * tq
    groups = [slice(g * gw, (g + 1) * gw) for g in range(mq // gw)]
    all_groups = tuple(range(len(groups)))
    lane = lax.broadcasted_iota(jnp.int32, (tq, LANES), 1)

    def load_queries(block):
        q = q_ref[pl.ds(pl.multiple_of(block * tq, tq), tq), :]
        zero = jnp.zeros_like(q)
        q2_ref[0:tq, :] = jnp.where(lane < DIFF_HEAD_DIM, q, zero)
        q2_ref[tq:mq, :] = jnp.where(lane >= DIFF_HEAD_DIM, q, zero)

    m_ref[...] = jnp.full(m_ref.shape, -jnp.inf, F32)
    acc_ref[...] = jnp.zeros(acc_ref.shape, F32)

    def visibility(r, g):
        q0 = (g * gw) % tq
        k0 = r * tk
        if k0 + tk - 1 <= q0:
            return "all", None
        if k0 > q0 + gw - 1:
            return "none", None
        return "some", k0 - q0

    def stage_keys(j):
        kc_ref[...] = k_ref[pl.ds(pl.multiple_of(j * tk, tk), tk), :]

    def visible_rows(shift):
        return tk if shift is None else min(tk, gw - shift)

    def produce(g, s_ref, mx_ref, shift=None):
        nr = visible_rows(shift)
        s = lax.dot_general(kc_ref[0:nr, :], q2_ref[groups[g], :], NT_DIMS,
                            preferred_element_type=F32)
        s_ref[g, 0:nr, :] = s
        if shift is None:
            mx_ref[0:1, groups[g]] = jnp.max(s, axis=0, keepdims=True)

    def values(j, nr):
        return jnp.concatenate([vt_ref[j, :, 0:nr], jnp.ones((BF16_ROWS, nr), BF16)], axis=0)

    def consume(j, va, g, s_ref, mx_ref, shift):
        cs = groups[g]
        nr = visible_rows(shift)
        m_old = m_ref[0:1, cs]
        if shift is None:
            s = s_ref[g]
            m_new = jnp.maximum(m_old, mx_ref[0:1, cs])
        else:
            visible = (lax.broadcasted_iota(jnp.int32, (nr, gw), 0) + shift
                       <= lax.broadcasted_iota(jnp.int32, (nr, gw), 1))
            s = jnp.where(visible, s_ref[g, 0:nr, :], -jnp.inf)
            m_new = jnp.maximum(m_old, jnp.max(s, axis=0, keepdims=True))
            va = va if nr == tk else values(j, nr)
        alpha = jnp.exp2(m_old - m_new)
        p = jnp.exp2(s - m_new).astype(BF16)
        m_ref[0:1, cs] = m_new
        acc_ref[g] = alpha * acc_ref[g] + _dot(va, p)

    def step(j, cur, nxt, diag=None):
        va = values(j, tk)
        if diag != 1:
            stage_keys(j + 1)
        for g in all_groups:
            if diag is None:
                produce(g, *nxt)
            elif diag == 0 and visibility(1, g)[0] != "none":
                produce(g, *nxt, shift=visibility(1, g)[1])
            kind, shift = ("all", None) if diag is None else visibility(diag, g)
            if kind != "none":
                consume(j, va, g, *cur, shift=shift)

    buf_a = (sa_ref, mxa_ref)
    buf_b = (sb_ref, mxb_ref)

    @pl.when(i == 0)
    def _():
        load_queries(0)
        stage_keys(0)
        for g in all_groups:
            produce(g, *buf_a)

    def pair(t):
        step(2 * t, buf_a, buf_b)
        step(2 * t + 1, buf_b, buf_a)

    def four_pairs_body(u, carry):
        for t in range(4):
            pair(4 * u + t)
        return carry

    lax.fori_loop(0, i // 4, four_pairs_body, 0)
    done = (i // 4) * 4
    for rest in range(4):
        @pl.when(i % 4 == rest)
        def _(rest=rest):
            for t in range(rest):
                pair(done + t)
            step(2 * i, buf_a, buf_b, diag=0)

    def finish():
        lam_vec = lambda row: _param_row(p_ref, row, DIFF_HEAD_DIM)
        lam = (jnp.exp(jnp.sum(lam_vec(P_LAMBDA_Q1) * lam_vec(P_LAMBDA_K1), axis=-1, keepdims=True))
               - jnp.exp(jnp.sum(lam_vec(P_LAMBDA_Q2) * lam_vec(P_LAMBDA_K2), axis=-1, keepdims=True))
               + lam_init)
        acc = jnp.concatenate([acc_ref[g] for g in all_groups], axis=1)
        num = acc[0:DIFF_V_DIM, :]
        den = acc[DIFF_V_DIM:DIFF_V_DIM + 1, :]
        o_t = num[:, :tq] / den[:, :tq] - lam * (num[:, tq:] / den[:, tq:])
        o = o_t.T
        gain = _param_row(p_ref, P_SUBLN_G, DIFF_V_DIM)
        o_ref[...] = (_rms(o, gain, HEAD_NORM_EPS) * (1.0 - lam_init)).astype(o_ref.dtype)

    @pl.when(i < last_block)
    def _():
        step(2 * i + 1, buf_b, buf_a, diag=1)
        load_queries(i + 1)
        stage_keys(0)
        for g in all_groups:
            produce(g, *buf_a)
        finish()

    @pl.when(i == last_block)
    def _():
        step(2 * i + 1, buf_b, buf_a, diag=1)
        finish()


def _diffattn(params, q, k, vt, tq, gw, lam_init):
    B, S, W = q.shape
    tk = vt.shape[3]
    assert tq == 2 * tk and tq % gw == 0
    mq = 2 * tq
    return pl.pallas_call(
        functools.partial(_attn_body, tq=tq, gw=gw, lam_init=lam_init),
        grid=(B, W // DIFF_V_DIM, S // tq),
        in_specs=[pl.BlockSpec(params.shape, lambda b, h, i: (0, 0)),
                  pl.BlockSpec((None, S, LANES), lambda b, h, i: (b, 0, h)),
                  pl.BlockSpec((None, S, LANES), lambda b, h, i: (b, 0, h)),
                  pl.BlockSpec((None, S // tk, DIFF_V_DIM, tk), lambda b, h, i: (b, 0, h, 0))],
        out_specs=pl.BlockSpec((None, tq, LANES), lambda b, h, i: (b, i, h)),
        out_shape=jax.ShapeDtypeStruct((B, S, W), BF16),
        scratch_shapes=[pltpu.VMEM((mq, LANES), BF16),
                        pltpu.VMEM((8, mq), F32),
                        pltpu.VMEM((mq // gw, DIFF_V_DIM + BF16_ROWS, gw), F32),
                        pltpu.VMEM((mq // gw, tk, gw), F32), pltpu.VMEM((mq // gw, tk, gw), F32),
                        pltpu.VMEM((8, mq), F32), pltpu.VMEM((8, mq), F32),
                        pltpu.VMEM((tk, LANES), BF16)],
        compiler_params=pltpu.CompilerParams(
            dimension_semantics=("arbitrary", "arbitrary", "arbitrary"),
            vmem_limit_bytes=VMEM_LIMIT_BYTES),
        name="diffattn",
    )(params, q, k, vt)


def _retention_consts(n_heads, chunk):
    log_g = np.log1p(-(2.0 ** (-5.0 - np.arange(n_heads, dtype=np.float64))))
    n = np.arange(chunk, dtype=np.float64)
    rel = n[:, None] - n[None, :]
    dmat = np.where(rel >= 0, np.exp(log_g[:, None, None] * np.maximum(rel, 0.0)), 0.0)
    qdec = np.exp(log_g[:, None] * (n + 1.0)[None, :])
    kdec = np.exp(log_g[:, None] * (chunk - 1.0 - n)[None, :])
    cdec = np.exp(log_g * chunk)
    pairs = n_heads // 2
    side_by_side = lambda a: a.reshape(pairs, 2, *a.shape[1:])
    dmat2 = np.concatenate(list(side_by_side(dmat).transpose(1, 0, 2, 3)), axis=-1)
    per_row = lambda a: np.repeat(side_by_side(a).transpose(0, 2, 1), RET_V_DIM, axis=-1)
    cdec2 = np.broadcast_to(np.repeat(cdec.reshape(pairs, 2), RET_KEY_DIM, axis=1)[:, :, None],
                            (pairs, 2 * RET_KEY_DIM, 2 * RET_V_DIM))
    return tuple(jnp.asarray(np.ascontiguousarray(a), F32)
                 for a in (dmat2, per_row(qdec), per_row(kdec), cdec2))


def _outffn_body(x_ref, a_ref, r_ref, p_ref, wo_ref, wg_ref, wu_ref, wd_ref, o_ref):
    mix = jnp.concatenate([a_ref[...], r_ref[...]], axis=1)
    x1 = x_ref[...] + _dot(mix, wo_ref[...])
    h = _rms(x1, _param_row(p_ref, P_FFN_G), NORM_EPS).astype(BF16)
    d_ff = wg_ref.shape[1]
    x2 = x1
    for c0 in range(0, d_ff, FFN_SLAB):
        c1 = min(c0 + FFN_SLAB, d_ff)
        gate = _dot(h, wg_ref[:, c0:c1])
        up = _dot(h, wu_ref[:, c0:c1])
        act = (gate * jax.nn.sigmoid(gate) * up).astype(BF16)
        x2 = x2 + _dot(act, wd_ref[c0:c1, :])
    o_ref[...] = _rms(x2, _param_row(p_ref, P_FINAL_G), NORM_EPS)


def _outffn(x2d, a, r, params, wo, wg, wu, wd, tm):
    T, D = x2d.shape
    row = lambda t: (t, 0)
    const = lambda t: (0, 0)
    resident = lambda arr: pl.BlockSpec(arr.shape, const, pipeline_mode=pl.Buffered(1))
    return pl.pallas_call(
        _outffn_body,
        grid=(T // tm,),
        in_specs=[pl.BlockSpec((tm, D), row),
                  pl.BlockSpec((tm, a.shape[1]), row),
                  pl.BlockSpec((tm, r.shape[1]), row),
                  pl.BlockSpec(params.shape, const),
                  resident(wo), resident(wg), resident(wu), resident(wd)],
        out_specs=pl.BlockSpec((tm, D), row),
        out_shape=jax.ShapeDtypeStruct((T, D), F32),
        compiler_params=pltpu.CompilerParams(
            dimension_semantics=("arbitrary",), vmem_limit_bytes=VMEM_LIMIT_BYTES),
        name="outffn",
    )(x2d, a, r, params, wo, wg, wu, wd)


def _rotation_tables(seq, tm):
    d = DIFF_HEAD_DIM
    inv_h = ROPE_THETA ** (-np.arange(0, d, 2, dtype=np.float64) / d)
    lanes_h = np.tile(np.concatenate([inv_h, inv_h]), LANES // d)
    dr = RET_KEY_DIM
    inv_r = 1.0 / (RET_THETA ** np.linspace(0.0, 1.0, dr // 2, dtype=np.float64))
    lanes_r = np.tile(np.repeat(inv_r, 2), LANES // dr)

    def cos_sin(pos):
        out = []
        for freq in (lanes_h, lanes_r):
            ang = pos[:, None] * freq[None, :]
            out += [np.cos(ang), np.sin(ang)]
        return jnp.asarray(np.stack(out), F32)

    starts = np.arange(seq // tm, dtype=np.float64) * tm
    return cos_sin(starts), cos_sin(np.arange(tm, dtype=np.float64))


def kernel(x, attn_norm_g, w_in, lambda_q1, lambda_k1, lambda_q2, lambda_k2, diff_subln_g,
           ret_norm_g, w_out, ffn_norm_g, w_gate, w_up, w_down, final_norm_g):
    B, S, D = x.shape
    assert w_in.shape[0] == 1, "single-layer block only"
    l = 0
    diff_w = D // 2
    ret_w = D - diff_w
    ret_heads = ret_w // RET_V_DIM
    widths = (diff_w, diff_w, diff_w, ret_heads * RET_KEY_DIM, ret_heads * RET_KEY_DIM, ret_w, ret_w)
    tm_in = 512
    tabs = _rotation_tables(S, tm_in)
    ret_chunk = 128
    ret_consts = _retention_consts(ret_heads, ret_chunk)
    xs = x.reshape(B * S, D)
    lam_init = 0.8 - 0.6 * math.exp(-0.3 * l)
    params = _pack_params(D, [attn_norm_g[l], ffn_norm_g[l], final_norm_g, ret_norm_g[l],
                              diff_subln_g[l], lambda_q1[l], lambda_k1[l], lambda_q2[l], lambda_k2[l]])
    later_weights = [w_out[l], w_gate[l], w_up[l], w_down[l]]
    (q, k, vt, r), (wo_b, wg_b, wu_b, wd_b) = _inproj(
        xs, params, w_in[l], tabs, ret_consts, later_weights, S, widths,
        tm=tm_in, tk=512, chunk=ret_chunk)
    tok3 = lambda t: t.reshape(B, S, t.shape[-1])
    a = _diffattn(params, tok3(q), tok3(k), vt, tq=1024, gw=256, lam_init=lam_init)
    out = _outffn(xs, a.reshape(B * S, diff_w), r, params, wo_b, wg_b, wu_b, wd_b, tm=512)
    return out.reshape(B, S, D)
```

```python
import functools
import math

import jax
import jax.numpy as jnp
import numpy as np
from jax import lax
from jax.experimental import pallas as pl
from jax.experimental.pallas import tpu as pltpu

F32 = jnp.float32
BF16 = jnp.bfloat16

DIFF_HEAD_DIM = 64
DIFF_V_DIM = 128
RET_KEY_DIM = 64
RET_V_DIM = 128
ROPE_THETA = 10000.0
RET_THETA = 10000.0
NORM_EPS = 1e-6
HEAD_NORM_EPS = 1e-5
LANES = 128
BF16_ROWS = 16
VMEM_LIMIT_BYTES = 56 * 1024 * 1024
LOG2E = math.log2(math.e)

NT_DIMS = (((1,), (1,)), ((), ()))
TN_DIMS = (((0,), (0,)), ((), ()))


def _dot(a, b):
    return jnp.dot(a, b, preferred_element_type=F32)


def _rms(x, g, eps):
    ms = jnp.mean(x * x, axis=-1, keepdims=True)
    return x * lax.rsqrt(ms + eps) * g


(P_ATTN_G, P_FFN_G, P_FINAL_G, P_RET_G, P_SUBLN_G,
 P_LAMBDA_Q1, P_LAMBDA_K1, P_LAMBDA_Q2, P_LAMBDA_K2) = range(9)
P_ROWS = 16


def _pack_params(d_model, vectors):
    pieces = []
    for v in vectors:
        pieces += [v.astype(F32), jnp.zeros((d_model - v.shape[0],), F32)]
    pieces.append(jnp.zeros(((P_ROWS - len(vectors)) * d_model,), F32))
    return jnp.concatenate(pieces).reshape(P_ROWS, d_model)


def _param_row(p_ref, row, width=None):
    width = p_ref.shape[1] if width is None else width
    return p_ref[row:row + 1, 0:width]


def _cast_plan(weights, steps):
    specs, shapes, n_blocks = [], [], []
    for w in weights:
        rows, cols = w.shape
        rb = BF16_ROWS
        while rows % rb or rows // rb > steps:
            rb += BF16_ROWS
        nb = rows // rb
        specs.append(pl.BlockSpec((rb, cols), lambda t, nb=nb: (jnp.minimum(t, nb - 1), 0)))
        shapes.append(jax.ShapeDtypeStruct(w.shape, BF16))
        n_blocks.append(nb)
    return specs, shapes, tuple(n_blocks)


def _cast_step(step, w_refs, wb_refs, n_blocks):
    for w_ref, wb_ref, nb in zip(w_refs, wb_refs, n_blocks):
        @pl.when(step < nb)
        def _(w_ref=w_ref, wb_ref=wb_ref):
            wb_ref[...] = w_ref[...].astype(wb_ref.dtype)


def _retention_tile(rq_ref, rk_ref, rv_ref, rg_ref, o_ref, state_ref,
                    dmat_ref, qdec_ref, kdec_ref, cdec_ref, gain, chunk):
    n_chunks = rq_ref.shape[0] // chunk
    n_pairs = rq_ref.shape[1] // LANES
    pair_v = 2 * RET_V_DIM
    rows = [slice(c * chunk, (c + 1) * chunk) for c in range(n_chunks)]
    key_lane = lax.broadcasted_iota(jnp.int32, (chunk, LANES), 1)
    val_lane = lax.broadcasted_iota(jnp.int32, (chunk, pair_v), 1)
    on_diagonal = ((lax.broadcasted_iota(jnp.int32, (LANES, pair_v), 0) < RET_KEY_DIM)
                   == (lax.broadcasted_iota(jnp.int32, (LANES, pair_v), 1) < RET_V_DIM))

    def stack_heads(x, lane, split):
        zero = jnp.zeros_like(x)
        return jnp.concatenate([jnp.where(lane < split, x, zero), jnp.where(lane >= split, x, zero)],
                               axis=0)

    scores, updates = {}, {}
    for c in range(n_chunks):
        for p in range(n_pairs):
            qp = rq_ref[rows[c], p * LANES:(p + 1) * LANES]
            kp = rk_ref[rows[c], p * LANES:(p + 1) * LANES]
            v2 = rv_ref[rows[c], p * pair_v:(p + 1) * pair_v]
            s = lax.dot_general(qp, stack_heads(kp, key_lane, RET_KEY_DIM), NT_DIMS,
                                preferred_element_type=F32)
            scores[c, p] = (s * dmat_ref[p]).astype(BF16)
            vd = (v2.astype(F32) * kdec_ref[p]).astype(BF16)
            updates[c, p] = lax.dot_general(kp, vd, TN_DIMS, preferred_element_type=F32)

    states = {}
    for p in range(n_pairs):
        st = state_ref[p]
        for c in range(n_chunks):
            states[c, p] = st.astype(BF16)
            st = cdec_ref[p] * st + jnp.where(on_diagonal, updates[c, p], 0.0)
        state_ref[p] = st

    yield

    for c in range(n_chunks):
        for p in range(n_pairs):
            qp = rq_ref[rows[c], p * LANES:(p + 1) * LANES]
            v2 = rv_ref[rows[c], p * pair_v:(p + 1) * pair_v]
            cross = _dot(qp, states[c, p]) * qdec_ref[p]
            o2 = _dot(scores[c, p], stack_heads(v2, val_lane, RET_V_DIM)) + cross
            for hh in range(2):
                hs = slice((2 * p + hh) * RET_V_DIM, (2 * p + hh + 1) * RET_V_DIM)
                o = o2[:, hh * RET_V_DIM:(hh + 1) * RET_V_DIM]
                mu = jnp.mean(o, axis=-1, keepdims=True)
                cen = o - mu
                var = jnp.mean(cen * cen, axis=-1, keepdims=True)
                y = cen * lax.rsqrt(var + HEAD_NORM_EPS) * gain[:, hs]
                gate = rg_ref[rows[c], hs]
                o_ref[rows[c], hs] = (gate * jax.nn.sigmoid(gate) * y).astype(o_ref.dtype)


def _inproj_body(x_ref, p_ref, w_ref, base_ref, off_ref,
                 dmat_ref, qdec_ref, kdec_ref, cdec_ref, *refs,
                 widths, tiles_per_seq, chunk, cast_blocks):
    n_cast = len(cast_blocks)
    cast_in, refs = refs[:n_cast], refs[n_cast:]
    (q_ref, k_ref, vt_ref, r_ref), refs = refs[:4], refs[4:]
    cast_out, refs = refs[:n_cast], refs[n_cast:]
    rq_ref, rk_ref, rv_ref, rg_ref, state_ref, wb_ref, wvt_ref = refs
    dq_w, dk_w, dv_w, rq_w, rk_w, rv_w, rg_w = widths
    tile_in_seq = pl.program_id(0) % tiles_per_seq
    _cast_step(pl.program_id(0), cast_in, cast_out, cast_blocks)

    @pl.when(pl.program_id(0) == 0)
    def _():
        c_v = dq_w + dk_w
        for c0 in range(0, w_ref.shape[1], dv_w):
            if c0 == c_v:
                wvt_ref[...] = w_ref[:, c0:c0 + dv_w].T.astype(wvt_ref.dtype)
            else:
                wb_ref[:, c0:c0 + dv_w] = w_ref[:, c0:c0 + dv_w].astype(wb_ref.dtype)

    @pl.when(tile_in_seq == 0)
    def _():
        state_ref[...] = jnp.zeros(state_ref.shape, F32)

    h = _rms(x_ref[...], _param_row(p_ref, P_ATTN_G), NORM_EPS).astype(BF16)
    lane = lax.broadcasted_iota(jnp.int32, off_ref.shape[1:], 1)
    first_half = (lane & (DIFF_HEAD_DIM // 2)) == 0
    even = (lane & 1) == 0

    def angle_tables(kind, negate_where):
        cb = base_ref[2 * kind, pl.ds(tile_in_seq, 1), :]
        sb = base_ref[2 * kind + 1, pl.ds(tile_in_seq, 1), :]
        co, so = off_ref[2 * kind], off_ref[2 * kind + 1]
        sin = sb * co + cb * so
        return cb * co - sb * so, jnp.where(negate_where, -sin, sin)

    ch, sh = angle_tables(0, first_half)
    ci, si = angle_tables(1, even)

    def rope(xc):
        sw = jnp.where(first_half, pltpu.roll(xc, LANES - 32, 1), pltpu.roll(xc, 32, 1))
        return xc * ch + sw * sh

    def pair_rot(xc):
        sw = jnp.where(even, pltpu.roll(xc, LANES - 1, 1), pltpu.roll(xc, 1, 1))
        return xc * ci + sw * si

    def emit(out_ref, c0, width, fn):
        p = _dot(h, wb_ref[:, c0:c0 + width])
        for j in range(width // LANES):
            sl = slice(j * LANES, (j + 1) * LANES)
            out_ref[:, sl] = fn(p[:, sl]).astype(out_ref.dtype)

    c_q, c_k = 0, dq_w
    c_rq = dq_w + dk_w + dv_w
    c_rk, c_rv, c_rg = c_rq + rq_w, c_rq + rq_w + rk_w, c_rq + rq_w + rk_w + rv_w
    emit(rq_ref, c_rq, rq_w, pair_rot)
    emit(rk_ref, c_rk, rk_w, lambda t: pair_rot(t) * (RET_KEY_DIM ** -0.5))
    emit(rv_ref, c_rv, rv_w, lambda t: t)
    emit(rg_ref, c_rg, rg_w, lambda t: t)
    retention = _retention_tile(rq_ref, rk_ref, rv_ref, rg_ref, r_ref, state_ref,
                                dmat_ref, qdec_ref, kdec_ref, cdec_ref,
                                _param_row(p_ref, P_RET_G, rv_w), chunk)
    next(retention)
    emit(q_ref, c_q, dq_w, lambda t: rope(t) * (DIFF_HEAD_DIM ** -0.5 * LOG2E))
    emit(k_ref, c_k, dk_w, rope)
    next(retention, None)
    vt = lax.dot_general(wvt_ref[...], h, NT_DIMS, preferred_element_type=F32)
    tkc = vt_ref.shape[-1]
    for j in range(vt_ref.shape[0]):
        vt_ref[j] = vt[:, j * tkc:(j + 1) * tkc].astype(vt_ref.dtype)


def _inproj(x2d, params, w, tabs, ret_consts, cast_weights, seq, widths, tm, tk, chunk):
    T, D = x2d.shape
    cast_specs, cast_shapes, cast_blocks = _cast_plan(cast_weights, T // tm)
    P = w.shape[1]
    nps = seq // tm
    dq_w, dk_w, dv_w, rq_w, rk_w, rv_w, rg_w = widths
    row = lambda t: (t, 0)
    const = lambda t: (0, 0)
    full = lambda a: pl.BlockSpec(a.shape, lambda t: (0,) * a.ndim)
    out_shape = [jax.ShapeDtypeStruct((T, dq_w), BF16),
                 jax.ShapeDtypeStruct((T, dk_w), BF16),
                 jax.ShapeDtypeStruct((T // seq, seq // tk, dv_w, tk), BF16),
                 jax.ShapeDtypeStruct((T, rv_w), BF16)]
    out_specs = [pl.BlockSpec((tm, dq_w), row),
                 pl.BlockSpec((tm, dk_w), row),
                 pl.BlockSpec((None, tm // tk, dv_w, tk), lambda t: (t // nps, t % nps, 0, 0)),
                 pl.BlockSpec((tm, rv_w), row)]
    outs = pl.pallas_call(
        functools.partial(_inproj_body, widths=widths, tiles_per_seq=nps, chunk=chunk,
                          cast_blocks=cast_blocks),
        grid=(T // tm,),
        in_specs=[pl.BlockSpec((tm, D), row),
                  pl.BlockSpec(params.shape, const),
                  pl.BlockSpec((D, P), const, pipeline_mode=pl.Buffered(1))]
                 + [full(a) for a in tabs] + [full(a) for a in ret_consts]
                 + cast_specs,
        out_specs=out_specs + cast_specs,
        out_shape=out_shape + cast_shapes,
        scratch_shapes=[pltpu.VMEM((tm, rq_w), BF16), pltpu.VMEM((tm, rk_w), BF16),
                        pltpu.VMEM((tm, rv_w), BF16), pltpu.VMEM((tm, rg_w), F32),
                        pltpu.VMEM((rq_w // LANES, LANES, 2 * RET_V_DIM), F32),
                        pltpu.VMEM((D, P), BF16), pltpu.VMEM((dv_w, D), BF16)],
        compiler_params=pltpu.CompilerParams(
            dimension_semantics=("arbitrary",), vmem_limit_bytes=VMEM_LIMIT_BYTES),
        name="inproj",
    )(x2d, params, w, *tabs, *ret_consts, *cast_weights)
    return outs[:4], outs[4:]


def _attn_body(p_ref, q_ref, k_ref, vt_ref, o_ref,
               q2_ref, m_ref, acc_ref, sa_ref, sb_ref, mxa_ref, mxb_ref, kc_ref,
               *, tq, gw, lam_init):
    i = pl.program_id(2)
    last_block = pl.num_programs(2) - 1
    tk = tq // 2
    mq = 2 * tq
    groups = [slice(g * gw, (g + 1) * gw) for g in range(mq // gw)]
    all_groups = tuple(range(len(groups)))
    lane = lax.broadcasted_iota(jnp.int32, (tq, LANES), 1)

    def load_queries(block):
        q = q_ref[pl.ds(pl.multiple_of(block * tq, tq), tq), :]
        zero = jnp.zeros_like(q)
        q2_ref[0:tq, :] = jnp.where(lane < DIFF_HEAD_DIM, q, zero)
        q2_ref[tq:mq, :] = jnp.where(lane >= DIFF_HEAD_DIM, q, zero)

    m_ref[...] = jnp.full(m_ref.shape, -jnp.inf, F32)
    acc_ref[...] = jnp.zeros(acc_ref.shape, F32)

    def visibility(r, g):
        q0 = (g * gw) % tq
        k0 = r * tk
        if k0 + tk - 1 <= q0:
            return "all", None
        if k0 > q0 + gw - 1:
            return "none", None
        return "some", k0 - q0

    def stage_keys(j):
        kc_ref[...] = k_ref[pl.ds(pl.multiple_of(j * tk, tk), tk), :]

    def visible_rows(shift):
        return tk if shift is None else min(tk, gw - shift)

    def produce(g, s_ref, mx_ref, shift=None):
        nr = visible_rows(shift)
        s = lax.dot_general(kc_ref[0:nr, :], q2_ref[groups[g], :], NT_DIMS,
                            preferred_element_type=F32)
        s_ref[g, 0:nr, :] = s
        if shift is None:
            mx_ref[0:1, groups[g]] = jnp.max(s, axis=0, keepdims=True)

    def values(j, nr):
        return jnp.concatenate([vt_ref[j, :, 0:nr], jnp.ones((BF16_ROWS, nr), BF16)], axis=0)

    def consume(j, va, g, s_ref, mx_ref, shift):
        cs = groups[g]
        nr = visible_rows(shift)
        m_old = m_ref[0:1, cs]
        if shift is None:
            s = s_ref[g]
            m_new = jnp.maximum(m_old, mx_ref[0:1, cs])
        else:
            visible = (lax.broadcasted_iota(jnp.int32, (nr, gw), 0) + shift
                       <= lax.broadcasted_iota(jnp.int32, (nr, gw), 1))
            s = jnp.where(visible, s_ref[g, 0:nr, :], -jnp.inf)
            m_new = jnp.maximum(m_old, jnp.max(s, axis=0, keepdims=True))
            va = va if nr == tk else values(j, nr)
        alpha = jnp.exp2(m_old - m_new)
        p = jnp.exp2(s - m_new).astype(BF16)
        m_ref[0:1, cs] = m_new
        acc_ref[g] = alpha * acc_ref[g] + _dot(va, p)

    def step(j, cur, nxt, diag=None):
        va = values(j, tk)
        if diag != 1:
            stage_keys(j + 1)
        for g in all_groups:
            if diag is None:
                produce(g, *nxt)
            elif diag == 0 and visibility(1, g)[0] != "none":
                produce(g, *nxt, shift=visibility(1, g)[1])
            kind, shift = ("all", None) if diag is None else visibility(diag, g)
            if kind != "none":
                consume(j, va, g, *cur, shift=shift)

    buf_a = (sa_ref, mxa_ref)
    buf_b = (sb_ref, mxb_ref)

    @pl.when(i == 0)
    def _():
        load_queries(0)
        stage_keys(0)
        for g in all_groups:
            produce(g, *buf_a)

    def pair(t):
        step(2 * t, buf_a, buf_b)
        step(2 * t + 1, buf_b, buf_a)

    def four_pairs_body(u, carry):
        for t in range(4):
            pair(4 * u + t)
        return carry

    lax.fori_loop(0, i // 4, four_pairs_body, 0)
    done = (i // 4) * 4
    for rest in range(4):
        @pl.when(i % 4 == rest)
        def _(rest=rest):
            for t in range(rest):
                pair(done + t)
            step(2 * i, buf_a, buf_b, diag=0)

    def finish():
        lam_vec = lambda row: _param_row(p_ref, row, DIFF_HEAD_DIM)
        lam = (jnp.exp(jnp.sum(lam_vec(P_LAMBDA_Q1) * lam_vec(P_LAMBDA_K1), axis=-1, keepdims=True))
               - jnp.exp(jnp.sum(lam_vec(P_LAMBDA_Q2) * lam_vec(P_LAMBDA_K2), axis=-1, keepdims=True))
               + lam_init)
        acc = jnp.concatenate([acc_ref[g] for g in all_groups], axis=1)
        num = acc[0:DIFF_V_DIM, :]
        den = acc[DIFF_V_DIM:DIFF_V_DIM + 1, :]
        o_t = num[:, :tq] / den[:, :tq] - lam * (num[:, tq:] / den[:, tq:])
        o = o_t.T
        gain = _param_row(p_ref, P_SUBLN_G, DIFF_V_DIM)
        o_ref[...] = (_rms(o, gain, HEAD_NORM_EPS) * (1.0 - lam_init)).astype(o_ref.dtype)

    @pl.when(i < last_block)
    def _():
        step(2 * i + 1, buf_b, buf_a, diag=1)
        load_queries(i + 1)
        stage_keys(0)
        for g in all_groups:
            produce(g, *buf_a)
        finish()

    @pl.when(i == last_block)
    def _():
        step(2 * i + 1, buf_b, buf_a, diag=1)
        finish()


def _diffattn(params, q, k, vt, tq, gw, lam_init):
    B, S, W = q.shape
    tk = vt.shape[3]
    assert tq == 2 * tk and tq % gw == 0
    mq = 2 * tq
    return pl.pallas_call(
        functools.partial(_attn_body, tq=tq, gw=gw, lam_init=lam_init),
        grid=(B, W // DIFF_V_DIM, S // tq),
        in_specs=[pl.BlockSpec(params.shape, lambda b, h, i: (0, 0)),
                  pl.BlockSpec((None, S, LANES), lambda b, h, i: (b, 0, h)),
                  pl.BlockSpec((None, S, LANES), lambda b, h, i: (b, 0, h)),
                  pl.BlockSpec((None, S // tk, DIFF_V_DIM, tk), lambda b, h, i: (b, 0, h, 0))],
        out_specs=pl.BlockSpec((None, tq, LANES), lambda b, h, i: (b, i, h)),
        out_shape=jax.ShapeDtypeStruct((B, S, W), BF16),
        scratch_shapes=[pltpu.VMEM((mq, LANES), BF16),
                        pltpu.VMEM((8, mq), F32),
                        pltpu.VMEM((mq // gw, DIFF_V_DIM + BF16_ROWS, gw), F32),
                        pltpu.VMEM((mq // gw, tk, gw), F32), pltpu.VMEM((mq // gw, tk, gw), F32),
                        pltpu.VMEM((8, mq), F32), pltpu.VMEM((8, mq), F32),
                        pltpu.VMEM((tk, LANES), BF16)],
        compiler_params=pltpu.CompilerParams(
            dimension_semantics=("arbitrary", "arbitrary", "arbitrary"),
            vmem_limit_bytes=VMEM_LIMIT_BYTES),
        name="diffattn",
    )(params, q, k, vt)


def _retention_consts(n_heads, chunk):
    log_g = np.log1p(-(2.0 ** (-5.0 - np.arange(n_heads, dtype=np.float64))))
    n = np.arange(chunk, dtype=np.float64)
    rel = n[:, None] - n[None, :]
    dmat = np.where(rel >= 0, np.exp(log_g[:, None, None] * np.maximum(rel, 0.0)), 0.0)
    qdec = np.exp(log_g[:, None] * (n + 1.0)[None, :])
    kdec = np.exp(log_g[:, None] * (chunk - 1.0 - n)[None, :])
    cdec = np.exp(log_g * chunk)
    pairs = n_heads // 2
    side_by_side = lambda a: a.reshape(pairs, 2, *a.shape[1:])
    dmat2 = np.concatenate(list(side_by_side(dmat).transpose(1, 0, 2, 3)), axis=-1)
    per_row = lambda a: np.repeat(side_by_side(a).transpose(0, 2, 1), RET_V_DIM, axis=-1)
    cdec2 = np.broadcast_to(np.repeat(cdec.reshape(pairs, 2), RET_KEY_DIM, axis=1)[:, :, None],
                            (pairs, 2 * RET_KEY_DIM, 2 * RET_V_DIM))
    return tuple(jnp.asarray(np.ascontiguousarray(a), F32)
                 for a in (dmat2, per_row(qdec), per_row(kdec), cdec2))


def _outffn_body(x_ref, a_ref, r_ref, p_ref, wo_ref, wg_ref, wu_ref, wd_ref, o_ref):
    mix = jnp.concatenate([a_ref[...], r_ref[...]], axis=1)
    x1 = x_ref[...] + _dot(mix, wo_ref[...])
    h = _rms(x1, _param_row(p_ref, P_FFN_G), NORM_EPS).astype(BF16)
    gate = _dot(h, wg_ref[...])
    up = _dot(h, wu_ref[...])
    act = (gate * jax.nn.sigmoid(gate) * up).astype(BF16)
    x2 = x1 + _dot(act, wd_ref[...])
    o_ref[...] = _rms(x2, _param_row(p_ref, P_FINAL_G), NORM_EPS)


def _outffn(x2d, a, r, params, wo, wg, wu, wd, tm):
    T, D = x2d.shape
    row = lambda t: (t, 0)
    const = lambda t: (0, 0)
    resident = lambda arr: pl.BlockSpec(arr.shape, const, pipeline_mode=pl.Buffered(1))
    return pl.pallas_call(
        _outffn_body,
        grid=(T // tm,),
        in_specs=[pl.BlockSpec((tm, D), row),
                  pl.BlockSpec((tm, a.shape[1]), row),
                  pl.BlockSpec((tm, r.shape[1]), row),
                  pl.BlockSpec(params.shape, const),
                  resident(wo), resident(wg), resident(wu), resident(wd)],
        out_specs=pl.BlockSpec((tm, D), row),
        out_shape=jax.ShapeDtypeStruct((T, D), F32),
        compiler_params=pltpu.CompilerParams(
            dimension_semantics=("arbitrary",), vmem_limit_bytes=VMEM_LIMIT_BYTES),
        name="outffn",
    )(x2d, a, r, params, wo, wg, wu, wd)


def _rotation_tables(seq, tm):
    d = DIFF_HEAD_DIM
    inv_h = ROPE_THETA ** (-np.arange(0, d, 2, dtype=np.float64) / d)
    lanes_h = np.tile(np.concatenate([inv_h, inv_h]), LANES // d)
    dr = RET_KEY_DIM
    inv_r = 1.0 / (RET_THETA ** np.linspace(0.0, 1.0, dr // 2, dtype=np.float64))
    lanes_r = np.tile(np.repeat(inv_r, 2), LANES // dr)

    def cos_sin(pos):
        out = []
        for freq in (lanes_h, lanes_r):
            ang = pos[:, None] * freq[None, :]
            out += [np.cos(ang), np.sin(ang)]
        return jnp.asarray(np.stack(out), F32)

    starts = np.arange(seq // tm, dtype=np.float64) * tm
    return cos_sin(starts), cos_sin(np.arange(tm, dtype=np.float64))


def kernel(x, attn_norm_g, w_in, lambda_q1, lambda_k1, lambda_q2, lambda_k2, diff_subln_g,
           ret_norm_g, w_out, ffn_norm_g, w_gate, w_up, w_down, final_norm_g):
    B, S, D = x.shape
    assert w_in.shape[0] == 1, "single-layer block only"
    l = 0
    diff_w = D // 2
    ret_w = D - diff_w
    ret_heads = ret_w // RET_V_DIM
    widths = (diff_w, diff_w, diff_w, ret_heads * RET_KEY_DIM, ret_heads * RET_KEY_DIM, ret_w, ret_w)
    tm_in = 512
    tabs = _rotation_tables(S, tm_in)
    ret_chunk = 128
    ret_consts = _retention_consts(ret_heads, ret_chunk)
    xs = x.reshape(B * S, D)
    lam_init = 0.8 - 0.6 * math.exp(-0.3 * l)
    params = _pack_params(D, [attn_norm_g[l], ffn_norm_g[l], final_norm_g, ret_norm_g[l],
                              diff_subln_g[l], lambda_q1[l], lambda_k1[l], lambda_q2[l], lambda_k2[l]])
    later_weights = [w_out[l], w_gate[l], w_up[l], w_down[l]]
    (q, k, vt, r), (wo_b, wg_b, wu_b, wd_b) = _inproj(
        xs, params, w_in[l], tabs, ret_consts, later_weights, S, widths,
        tm=tm_in, tk=512, chunk=ret_chunk)
    tok3 = lambda t: t.reshape(B, S, t.shape[-1])
    a = _diffattn(params, tok3(q), tok3(k), vt, tq=1024, gw=256, lam_init=lam_init)
    out = _outffn(xs, a.reshape(B * S, diff_w), r, params, wo_b, wg_b, wu_b, wd_b, tm=512)
    return out.reshape(B, S, D)
```
